```python
import math, functools
import jax, jax.numpy as jnp
from jax import lax
import numpy as np

D_MODEL = 1024
BATCH = 16
SEQ = 2048
DEPTH = 1
DEC_BATCH = 32
DEC_SEQ = 64
PAST_LEN = 2048

CHUNK = 64
LEFT_CHUNKS = 8
ATTN_REACH = LEFT_CHUNKS * CHUNK

N_HEADS_A = 8
HEAD_DIM_A = 64
ATTN_WIDTH = N_HEADS_A * HEAD_DIM_A
MAX_REL = 256
ATTN_SCALE = 1.0 / math.sqrt(HEAD_DIM_A)
NEG_INF = -1e30

SSM_HEADS = 16
SSM_HEAD_DIM = 64
D_INNER = SSM_HEADS * SSM_HEAD_DIM
SSM_GROUPS = 2
HEADS_PER_GROUP = SSM_HEADS // SSM_GROUPS
D_STATE = 128
CONV_WIDTH = 4
CONV_DIM = D_INNER + 2 * SSM_GROUPS * D_STATE

_IN_WIDTHS = (ATTN_WIDTH, ATTN_WIDTH, ATTN_WIDTH, D_INNER, CONV_DIM, SSM_HEADS, D_MODEL, D_MODEL)
IN_WIDTH = sum(_IN_WIDTHS)
IN_SPLITS = tuple(int(s) for s in np.cumsum(_IN_WIDTHS)[:-1])

N_EXPERT_GROUPS = 4
EXPERTS_PER_GROUP = 8
N_EXPERTS = N_EXPERT_GROUPS * EXPERTS_PER_GROUP
TOP_K = 2
D_EXPERT = 512
MOE_BLOCK = 128

EPS = 1e-6

kernel_name = "hybrid_streaming_band_attn_ssd_hiermoe_step"


def rms_norm(x, g):
    xf = x.astype(jnp.float32)
    y = xf * lax.rsqrt(jnp.mean(xf * xf, axis=-1, keepdims=True) + EPS)
    return (y * g.astype(jnp.float32)).astype(x.dtype)


def banded_softmax_attention(q, k, v, dist, valid, rel_table):
    bias = rel_table[:, jnp.clip(dist, -MAX_REL, MAX_REL) + MAX_REL].astype(jnp.float32)
    s = jnp.einsum('...qhd,...khd->...hqk', q, k, preferred_element_type=jnp.float32) * ATTN_SCALE + bias
    if valid is not None:
        s = jnp.where(valid, s, NEG_INF)
    p = jax.nn.softmax(s, axis=-1).astype(v.dtype)
    return jnp.einsum('...hqk,...khd->...qhd', p, v)


def prompt_band_attention(q, k, v, rel_table):
    b, l, h, dh = q.shape
    nc = l // CHUNK
    band = (LEFT_CHUNKS + 1) * CHUNK
    pad = ((0, 0), (ATTN_REACH, 0), (0, 0), (0, 0))
    kp, vp = jnp.pad(k, pad), jnp.pad(v, pad)
    idx = jnp.arange(nc)[:, None] * CHUNK + jnp.arange(band)[None, :]
    kb, vb = kp[:, idx], vp[:, idx]
    qc = q.reshape(b, nc, CHUNK, h, dh)
    valid = (idx >= ATTN_REACH)[None, :, None, None, :]
    dist = jnp.arange(CHUNK)[:, None] + ATTN_REACH - jnp.arange(band)[None, :]
    out = banded_softmax_attention(qc, kb, vb, dist, valid, rel_table).reshape(b, l, h, dh)
    keep = min(ATTN_REACH, l)
    return out, k[:, l - keep:], v[:, l - keep:]


def sample_band_attention(q, k, v, rel_table, cache_k, cache_v):
    s = q.shape[1]
    w = cache_k.shape[1]
    kk = jnp.concatenate([cache_k.astype(k.dtype), k], axis=1)
    vv = jnp.concatenate([cache_v.astype(v.dtype), v], axis=1)
    dist = jnp.arange(s)[:, None] + w - jnp.arange(w + s)[None, :]
    out = banded_softmax_attention(q, kk, vv, dist, None, rel_table)
    return out, kk[:, s:], vv[:, s:]


def ssd_chunked(xh, dt, a, bm, cm, h0):
    f32 = jnp.float32
    bt, l, g, e, p = xh.shape
    n = bm.shape[-1]
    t = min(CHUNK, l)
    nc = l // t
    xdt = (xh.astype(f32) * dt[..., None]).reshape(bt, nc, t, g, e, p)
    bc = bm.astype(f32).reshape(bt, nc, t, g, n)
    cc = cm.astype(f32).reshape(bt, nc, t, g, n)
    da = jnp.moveaxis((dt * a).reshape(bt, nc, t, g, e), 2, -1)
    cs = jnp.cumsum(da, axis=-1)
    causal = jnp.tril(jnp.ones((t, t), bool))
    decay = jnp.exp(jnp.where(causal, cs[..., :, None] - cs[..., None, :], -jnp.inf))
    cb = jnp.einsum('bctgn,bcsgn->bcgts', cc, bc)
    y_diag = jnp.einsum('bcgts,bcgets,bcsgep->bctgep', cb, decay, xdt)
    to_end = jnp.exp(cs[..., -1:] - cs)
    chunk_states = jnp.einsum('bctgn,bcget,bctgep->bcgepn', bc, to_end, xdt)
    chunk_decay = jnp.exp(cs[..., -1])

    def carry_state(h, inp):
        s_c, d_c = inp
        return h * d_c[..., None, None] + s_c, h

    h_last, h_in = lax.scan(carry_state, h0.astype(f32),
                            (jnp.moveaxis(chunk_states, 1, 0), jnp.moveaxis(chunk_decay, 1, 0)))
    h_in = jnp.moveaxis(h_in, 0, 1)
    y_off = jnp.einsum('bctgn,bcgepn,bcget->bctgep', cc, h_in, jnp.exp(cs))
    return (y_diag + y_off).reshape(bt, l, g, e, p), h_last


def ssm_branch(z, xbc, dt_raw, conv_state, ssm_state, conv_w, conv_b, dt_bias, a_log, d_skip, norm_w):
    f32 = jnp.float32
    bt, l, _ = xbc.shape
    xpad = jnp.concatenate([conv_state.astype(xbc.dtype), xbc], axis=1)
    conv = conv_b
    for j in range(CONV_WIDTH):
        conv = conv + xpad[:, j:j + l] * conv_w[j]
    new_conv = xpad[:, l:]
    u = jax.nn.silu(conv)
    xs_, bm, cm = jnp.split(u, (D_INNER, D_INNER + SSM_GROUPS * D_STATE), axis=-1)
    xh = xs_.reshape(bt, l, SSM_GROUPS, HEADS_PER_GROUP, SSM_HEAD_DIM)
    bm = bm.reshape(bt, l, SSM_GROUPS, D_STATE)
    cm = cm.reshape(bt, l, SSM_GROUPS, D_STATE)
    dt = jax.nn.softplus(dt_raw.astype(f32) + dt_bias.astype(f32)).reshape(bt, l, SSM_GROUPS, HEADS_PER_GROUP)
    a = -jnp.exp(a_log.astype(f32)).reshape(SSM_GROUPS, HEADS_PER_GROUP)
    h0 = ssm_state.reshape(bt, SSM_GROUPS, HEADS_PER_GROUP, SSM_HEAD_DIM, D_STATE)
    y, h_last = ssd_chunked(xh, dt, a, bm, cm, h0)
    y = y + xh.astype(f32) * d_skip.astype(f32).reshape(SSM_GROUPS, HEADS_PER_GROUP)[..., None]
    y = y.reshape(bt, l, D_INNER) * jax.nn.silu(z.astype(f32))
    yg = y.reshape(bt, l, SSM_GROUPS, D_INNER // SSM_GROUPS)
    yg = yg * lax.rsqrt(jnp.mean(yg * yg, axis=-1, keepdims=True) + EPS)
    y = yg.reshape(bt, l, D_INNER) * norm_w.astype(f32)
    return y.astype(z.dtype), new_conv, h_last.reshape(bt, SSM_HEADS, SSM_HEAD_DIM, D_STATE)


def routed_experts(xf, experts, weights, w_gate, w_up, w_down):
    t, d = xf.shape
    a = t * TOP_K
    flat_e = experts.reshape(a)
    flat_tok = jnp.repeat(jnp.arange(t, dtype=jnp.int32), TOP_K)
    flat_w = weights.reshape(a).astype(jnp.float32)
    order = jnp.argsort(flat_e)
    se = flat_e[order]
    counts = jnp.bincount(flat_e, length=N_EXPERTS)
    padded = (counts + MOE_BLOCK - 1) // MOE_BLOCK * MOE_BLOCK
    seg_start = jnp.cumsum(counts) - counts
    pad_end = jnp.cumsum(padded)
    pad_start = pad_end - padded
    dest = pad_start[se] + jnp.arange(a) - seg_start[se]
    n_blocks = -(-(a + N_EXPERTS * (MOE_BLOCK - 1)) // MOE_BLOCK)
    n_slots = n_blocks * MOE_BLOCK
    slot_tok = jnp.zeros((n_slots,), jnp.int32).at[dest].set(flat_tok[order])
    slot_w = jnp.zeros((n_slots,), jnp.float32).at[dest].set(flat_w[order])
    block_start = jnp.arange(n_blocks) * MOE_BLOCK
    block_expert = jnp.minimum(jnp.searchsorted(pad_end, block_start, side='right'), N_EXPERTS - 1)
    xb = xf[slot_tok].reshape(n_blocks, MOE_BLOCK, d)

    def expert_block(args):
        xblk, e = args
        hid = jax.nn.silu(xblk @ w_gate[e]) * (xblk @ w_up[e])
        return hid @ w_down[e]

    yb = lax.map(expert_block, (xb, block_expert)).reshape(n_slots, d)
    y = jax.ops.segment_sum(yb.astype(jnp.float32) * slot_w[:, None], slot_tok, num_segments=t)
    return y.astype(xf.dtype)


def hierarchical_moe(h, w_rg, b_rg, w_re, b_re, w_gate, w_up, w_down):
    bt, l, d = h.shape
    hf = h.reshape(bt * l, d)
    t = hf.shape[0]
    g_prob = jax.nn.softmax((hf @ w_rg).astype(jnp.float32) + b_rg.astype(jnp.float32), axis=-1)
    g_top, g_idx = lax.top_k(g_prob, 1)
    e_logits = ((hf @ w_re).astype(jnp.float32) + b_re.astype(jnp.float32)).reshape(t, N_EXPERT_GROUPS, EXPERTS_PER_GROUP)
    e_sel = jnp.take_along_axis(e_logits, g_idx[:, :, None], axis=1)[:, 0]
    e_top, e_idx = lax.top_k(jax.nn.softmax(e_sel, axis=-1), TOP_K)
    weights = g_top * e_top / jnp.sum(e_top, axis=-1, keepdims=True)
    experts = g_idx * EXPERTS_PER_GROUP + e_idx
    return routed_experts(hf, experts, weights, w_gate, w_up, w_down).reshape(bt, l, d)


def trunk_layer(x, attend, conv_state, ssm_state, norm_mix, w_in, rel_table, conv_w, conv_b, dt_bias,
                a_log, d_skip, ssm_norm, w_branch_a, w_branch_b, w_out, norm_ffn, w_rg, b_rg, w_re, b_re,
                w_gate, w_up, w_down):
    bt, l, _ = x.shape
    h = rms_norm(x, norm_mix)
    q, k, v, z, xbc, dt_raw, gate_a, gate_b = jnp.split(h @ w_in, IN_SPLITS, axis=-1)
    heads = lambda u: u.reshape(bt, l, N_HEADS_A, HEAD_DIM_A)
    attn, k_keep, v_keep = attend(heads(q), heads(k), heads(v), rel_table)
    ssm, conv_new, ssm_new = ssm_branch(z, xbc, dt_raw, conv_state, ssm_state, conv_w, conv_b,
                                        dt_bias, a_log, d_skip, ssm_norm)
    mixed = (jax.nn.sigmoid(gate_a) * (attn.reshape(bt, l, ATTN_WIDTH) @ w_branch_a)
             + jax.nn.sigmoid(gate_b) * (ssm @ w_branch_b))
    x = x + mixed @ w_out
    x = x + hierarchical_moe(rms_norm(x, norm_ffn), w_rg, b_rg, w_re, b_re, w_gate, w_up, w_down)
    return x, k_keep, v_keep, conv_new, ssm_new


def setup_inputs(seed: int = 0) -> dict:
    key = jax.random.key(seed)
    ks = jax.random.split(key, 32)
    f32 = jnp.float32
    nrm = lambda k, shape, scale: jax.random.normal(k, shape, f32) * scale
    cache_rows = min(ATTN_REACH, PAST_LEN)
    dt0 = jnp.exp(jax.random.uniform(ks[10], (DEPTH, SSM_HEADS), f32)
                  * (math.log(0.1) - math.log(0.001)) + math.log(0.001))
    return {
        "x_prompt": nrm(ks[0], (BATCH, SEQ, D_MODEL), 1.0),
        "x_sample": nrm(ks[1], (DEC_BATCH, DEC_SEQ, D_MODEL), 1.0),
        "cache_attn_k": nrm(ks[2], (DEPTH, DEC_BATCH, cache_rows, N_HEADS_A, HEAD_DIM_A), 1.0),
        "cache_attn_v": nrm(ks[3], (DEPTH, DEC_BATCH, cache_rows, N_HEADS_A, HEAD_DIM_A), 1.0),
        "state_conv": nrm(ks[4], (DEPTH, DEC_BATCH, CONV_WIDTH - 1, CONV_DIM), 1.0),
        "state_ssm": nrm(ks[5], (DEPTH, DEC_BATCH, SSM_HEADS, SSM_HEAD_DIM, D_STATE), 0.1),
        "norm_mix": 1.0 + nrm(ks[6], (DEPTH, D_MODEL), 0.05),
        "w_in": nrm(ks[7], (DEPTH, D_MODEL, IN_WIDTH), D_MODEL ** -0.5),
        "rel_bias": nrm(ks[8], (DEPTH, N_HEADS_A, 2 * MAX_REL + 1), 0.1),
        "conv_w": nrm(ks[9], (DEPTH, CONV_WIDTH, CONV_DIM), CONV_WIDTH ** -0.5),
        "conv_b": nrm(ks[11], (DEPTH, CONV_DIM), 0.01),
        "dt_bias": dt0 + jnp.log(-jnp.expm1(-dt0)),
        "a_log": jnp.log(jax.random.uniform(ks[12], (DEPTH, SSM_HEADS), f32, 1.0, 16.0)),
        "d_skip": 1.0 + nrm(ks[13], (DEPTH, SSM_HEADS), 0.1),
        "ssm_norm": 1.0 + nrm(ks[14], (DEPTH, D_INNER), 0.05),
        "w_branch_a": nrm(ks[15], (DEPTH, ATTN_WIDTH, D_MODEL), ATTN_WIDTH ** -0.5),
        "w_branch_b": nrm(ks[16], (DEPTH, D_INNER, D_MODEL), D_INNER ** -0.5),
        "w_out": nrm(ks[17], (DEPTH, D_MODEL, D_MODEL), D_MODEL ** -0.5),
        "norm_ffn": 1.0 + nrm(ks[18], (DEPTH, D_MODEL), 0.05),
        "w_router_group": nrm(ks[19], (DEPTH, D_MODEL, N_EXPERT_GROUPS), D_MODEL ** -0.5),
        "b_router_group": nrm(ks[20], (DEPTH, N_EXPERT_GROUPS), 0.01),
        "w_router_expert": nrm(ks[21], (DEPTH, D_MODEL, N_EXPERTS), D_MODEL ** -0.5),
        "b_router_expert": nrm(ks[22], (DEPTH, N_EXPERTS), 0.01),
        "w_gate": nrm(ks[23], (DEPTH, N_EXPERTS, D_MODEL, D_EXPERT), D_MODEL ** -0.5),
        "w_up": nrm(ks[24], (DEPTH, N_EXPERTS, D_MODEL, D_EXPERT), D_MODEL ** -0.5),
        "w_down": nrm(ks[25], (DEPTH, N_EXPERTS, D_EXPERT, D_MODEL), D_EXPERT ** -0.5),
        "norm_final": 1.0 + nrm(ks[26], (D_MODEL,), 0.05),
    }


def reference(x_prompt, x_sample, cache_attn_k, cache_attn_v, state_conv, state_ssm,
              norm_mix, w_in, rel_bias, conv_w, conv_b, dt_bias, a_log, d_skip, ssm_norm,
              w_branch_a, w_branch_b, w_out, norm_ffn, w_router_group, b_router_group,
              w_router_expert, b_router_expert, w_gate, w_up, w_down, norm_final):
    xp, xs = x_prompt, x_sample
    bp = xp.shape[0]
    kp_l, vp_l, cp_l, sp_l = [], [], [], []
    ks_l, vs_l, cs_l, ss_l = [], [], [], []
    for i in range(DEPTH):
        lw = (norm_mix[i], w_in[i], rel_bias[i], conv_w[i], conv_b[i], dt_bias[i], a_log[i], d_skip[i],
              ssm_norm[i], w_branch_a[i], w_branch_b[i], w_out[i], norm_ffn[i], w_router_group[i],
              b_router_group[i], w_router_expert[i], b_router_expert[i], w_gate[i], w_up[i], w_down[i])
        conv0 = jnp.zeros((bp, CONV_WIDTH - 1, CONV_DIM), xp.dtype)
        ssm0 = jnp.zeros((bp, SSM_HEADS, SSM_HEAD_DIM, D_STATE), jnp.float32)
        xp, kp, vp, cp, sp = trunk_layer(xp, prompt_band_attention, conv0, ssm0, *lw)
        attend_s = functools.partial(sample_band_attention, cache_k=cache_attn_k[i], cache_v=cache_attn_v[i])
        xs, kk, vv, cc, ss = trunk_layer(xs, attend_s, state_conv[i], state_ssm[i], *lw)
        kp_l.append(kp); vp_l.append(vp); cp_l.append(cp); sp_l.append(sp)
        ks_l.append(kk); vs_l.append(vv); cs_l.append(cc); ss_l.append(ss)
    y_prompt = rms_norm(xp, norm_final)
    y_sample = rms_norm(xs, norm_final)
    return (y_prompt, y_sample,
            jnp.stack(kp_l), jnp.stack(vp_l), jnp.stack(cp_l), jnp.stack(sp_l),
            jnp.stack(ks_l), jnp.stack(vs_l), jnp.stack(cs_l), jnp.stack(ss_l))
```

```python
import functools
import math

import numpy as np
import jax
import jax.numpy as jnp
from jax import lax
from jax.experimental import pallas as pl
from jax.experimental.pallas import tpu as pltpu

F32 = jnp.float32
BF16 = jnp.bfloat16

D_MODEL = 1024
CHUNK = 64
LEFT_CHUNKS = 8
ATTN_REACH = LEFT_CHUNKS * CHUNK
BAND = ATTN_REACH + CHUNK
N_HEADS_A = 8
HEAD_DIM_A = 64
ATTN_WIDTH = N_HEADS_A * HEAD_DIM_A
MAX_REL = 256
ATTN_SCALE = 1.0 / math.sqrt(HEAD_DIM_A)
NEG_INF = -1e30
SSM_HEADS = 16
SSM_HEAD_DIM = 64
D_INNER = SSM_HEADS * SSM_HEAD_DIM
SSM_GROUPS = 2
GROUP_WIDTH = D_INNER // SSM_GROUPS
D_STATE = 128
CONV_WIDTH = 4
CONV_DIM = D_INNER + 2 * SSM_GROUPS * D_STATE
N_EXPERT_GROUPS = 4
EXPERTS_PER_GROUP = 8
N_EXPERTS = N_EXPERT_GROUPS * EXPERTS_PER_GROUP
TOP_K = 2
D_EXPERT = 512
MOE_BLOCK = 128
EPS = 1e-6

LANES = 128
Z0, GA0, GB0, XBC0, Q0, K0, V0, DT0 = 0, 1024, 2048, 3072, 4608, 5120, 5632, 6144
PROJ_WIDTH = DT0 + LANES
PROJ_TILE = 896
ROW_TILE = 256
VMEM_LIMIT = 56 * 1024 * 1024

_NT = (((1,), (1,)), ((), ()))
_TN = (((0,), (0,)), ((), ()))


def _params(*sem):
    return pltpu.CompilerParams(dimension_semantics=sem, vmem_limit_bytes=VMEM_LIMIT)


def _const_spec(shape):
    nd = len(shape)
    return pl.BlockSpec(shape, lambda *_: (0,) * nd)


def _inproj_kernel(x_ref, g_ref, w_ref, o_ref):
    x = x_ref[...]
    ms = jnp.mean(x * x, axis=-1, keepdims=True)
    h = (x * lax.rsqrt(ms + EPS) * g_ref[...]).astype(BF16)
    for j in range(0, PROJ_WIDTH, PROJ_TILE):
        o_ref[:, j:j + PROJ_TILE] = jnp.dot(h, w_ref[:, j:j + PROJ_TILE], preferred_element_type=F32)


def _in_projection(x2d, norm_w, w_proj):
    t = x2d.shape[0]
    return pl.pallas_call(
        _inproj_kernel,
        grid=(t // ROW_TILE,),
        in_specs=[pl.BlockSpec((ROW_TILE, D_MODEL), lambda i: (i, 0)),
                  _const_spec((1, D_MODEL)),
                  _const_spec((D_MODEL, PROJ_WIDTH))],
        out_specs=pl.BlockSpec((ROW_TILE, PROJ_WIDTH), lambda i: (i, 0)),
        out_shape=jax.ShapeDtypeStruct((t, PROJ_WIDTH), F32),
        name="in_projection",
        compiler_params=_params("parallel"),
    )(x2d, norm_w, w_proj)


def _attn_kernel(q_ref, k_ref, v_ref, bias_ref, o_ref, kpad, vpad, *, pad_rows, lk):
    c = pl.program_id(1)

    @pl.when(c == 0)
    def _():
        if pad_rows:
            kpad[0:pad_rows, :] = jnp.zeros((pad_rows, ATTN_WIDTH), BF16)
            vpad[0:pad_rows, :] = jnp.zeros((pad_rows, ATTN_WIDTH), BF16)
        kpad[pad_rows:pad_rows + lk, :] = k_ref[...].astype(BF16)
        vpad[pad_rows:pad_rows + lk, :] = v_ref[...].astype(BF16)

    start = pl.multiple_of(c * CHUNK, CHUNK)
    kw = kpad[pl.ds(start, BAND), :]
    vw = vpad[pl.ds(start, BAND), :]
    q = (q_ref[...] * ATTN_SCALE).astype(BF16)
    low = lax.broadcasted_iota(jnp.int32, (CHUNK, LANES), 1) < HEAD_DIM_A
    valid = lax.broadcasted_iota(jnp.int32, (CHUNK, BAND), 1) + start >= pad_rows
    for hp in range(N_HEADS_A // 2):
        sl = slice(LANES * hp, LANES * (hp + 1))
        q2, k2, v2 = q[:, sl], kw[:, sl], vw[:, sl]
        outs = []
        for sub in range(2):
            qm = jnp.where(low if sub == 0 else jnp.logical_not(low), q2, jnp.zeros_like(q2))
            s = lax.dot_general(qm, k2, _NT, preferred_element_type=F32) + bias_ref[2 * hp + sub]
            s = jnp.where(valid, s, NEG_INF)
            m = jnp.max(s, axis=-1, keepdims=True)
            p = jnp.exp(s - m)
            l = jnp.sum(p, axis=-1, keepdims=True)
            o = jnp.dot(p.astype(BF16), v2, preferred_element_type=F32)
            outs.append(o / l)
        o_ref[:, sl] = jnp.where(low, outs[0], outs[1])


def _band_attention(q_arr, q_col, k_arr, v_arr, kv_spec, bias, *, batch, nc, lk, pad_rows):
    kern = functools.partial(_attn_kernel, pad_rows=pad_rows, lk=lk)
    return pl.pallas_call(
        kern,
        grid=(batch, nc),
        in_specs=[pl.BlockSpec((CHUNK, ATTN_WIDTH), lambda b, c: (b * nc + c, q_col)),
                  kv_spec[0], kv_spec[1],
                  _const_spec((N_HEADS_A, CHUNK, BAND))],
        out_specs=pl.BlockSpec((CHUNK, ATTN_WIDTH), lambda b, c: (b * nc + c, 0)),
        out_shape=jax.ShapeDtypeStruct((batch * nc * CHUNK, ATTN_WIDTH), F32),
        scratch_shapes=[pltpu.VMEM((pad_rows + lk, ATTN_WIDTH), BF16),
                        pltpu.VMEM((pad_rows + lk, ATTN_WIDTH), BF16)],
        name="band_attention",
        compiler_params=_params("parallel", "arbitrary"),
    )(q_arr, k_arr, v_arr, bias)


def _split_bf16(x, parts):
    out = []
    for _ in range(parts - 1):
        h = x.astype(BF16)
        out.append(h)
        x = x - h.astype(F32)
    out.append(x.astype(BF16))
    return out


def _ssd_kernel(xbc_ref, z_ref, dt_ref, conv0_ref, st0_ref, cw_ref, cb_ref, dtb_ref, alog_ref,
                dexp_ref, nw_ref, tri_ref, exp_ref, sel_ref, eye_ref,
                y_ref, stout_ref, xwin, st, *, has_init):
    c = pl.program_id(1)
    nc = pl.num_programs(1)

    @pl.when(c == 0)
    def _():
        if has_init:
            xwin[0:8, :] = conv0_ref[...]
            st[...] = st0_ref[...]
        else:
            xwin[0:8, :] = jnp.zeros((8, CONV_DIM), F32)
            st[...] = jnp.zeros((D_STATE, D_INNER), F32)

    xwin[8:8 + CHUNK, :] = xbc_ref[...]
    conv = cb_ref[...]
    for j in range(CONV_WIDTH):
        conv = conv + xwin[5 + j:5 + j + CHUNK, :] * cw_ref[j:j + 1, :]
    xwin[5:8, :] = xwin[5 + CHUNK:8 + CHUNK, :]
    u = conv * jax.nn.sigmoid(conv)
    xs = u[:, 0:D_INNER]

    dt_in = dt_ref[...] + dtb_ref[...]
    dt = jnp.maximum(dt_in, 0.0) + jnp.log1p(jnp.exp(-jnp.abs(dt_in)))
    da = dt * (-jnp.exp(alog_ref[...]))
    tri = tri_ref[...]
    cs3 = jnp.dot(tri, jnp.concatenate(_split_bf16(da, 3), axis=1), preferred_element_type=F32)
    cs = cs3[:, 0:LANES] + cs3[:, LANES:2 * LANES] + cs3[:, 2 * LANES:3 * LANES]
    cs_last = cs[CHUNK - 1:CHUNK, :]
    ecs = jnp.exp(cs)
    to_end = jnp.exp(cs_last - cs)

    stack = jnp.concatenate(_split_bf16(dt, 2) + _split_bf16(ecs, 2) + _split_bf16(to_end, 2)
                            + _split_bf16(cs, 3), axis=0)
    wide = jnp.dot(stack, exp_ref[...], preferred_element_type=F32)
    rows = [wide[i * CHUNK:(i + 1) * CHUNK, :] for i in range(9)]
    dt_w = rows[0] + rows[1]
    ecs_w = rows[2] + rows[3]
    to_end_w = rows[4] + rows[5]
    cs_w = rows[6] + rows[7] + rows[8]

    lane = lax.broadcasted_iota(jnp.int32, (CHUNK, LANES), 1)
    even = (lane % 2) == 0
    csm = jnp.concatenate([jnp.where(even, cs, 0.0), jnp.where(even, 0.0, cs)], axis=0)
    srow3 = lax.dot_general(sel_ref[...], jnp.concatenate(_split_bf16(csm, 3), axis=0), _NT,
                            preferred_element_type=F32)
    srow = srow3[:, 0:LANES] + srow3[:, LANES:2 * LANES] + srow3[:, 2 * LANES:3 * LANES]

    xdt = xs * dt_w
    xdt_b = xdt.astype(BF16)
    xw_b = (xdt * to_end_w).astype(BF16)
    low = lane < SSM_HEAD_DIM
    t_idx = lax.broadcasted_iota(jnp.int32, (CHUNK, LANES), 0)
    causal = t_idx >= (lane % SSM_HEAD_DIM)
    eye = eye_ref[...]

    y_parts = []
    for g in range(SSM_GROUPS):
        b_g = u[:, D_INNER + g * D_STATE:D_INNER + (g + 1) * D_STATE].astype(BF16)
        c_g = u[:, D_INNER + (SSM_GROUPS + g) * D_STATE:D_INNER + (SSM_GROUPS + g + 1) * D_STATE].astype(BF16)
        gsl = slice(g * GROUP_WIDTH, (g + 1) * GROUP_WIDTH)
        st_g = st[:, gsl]
        y_off = jnp.dot(c_g, st_g.astype(BF16), preferred_element_type=F32) * ecs_w[:, gsl]
        b_t = lax.dot_general(eye, b_g, _NT, preferred_element_type=F32).astype(BF16)
        st[:, gsl] = st_g * ecs_w[CHUNK - 1:CHUNK, gsl] + jnp.dot(b_t, xw_b[:, gsl], preferred_element_type=F32)
        cb2 = lax.dot_general(c_g, jnp.concatenate([b_g, b_g], axis=0), _NT, preferred_element_type=F32)
        for kk in range(SSM_HEADS // SSM_GROUPS // 2):
            k = g * (SSM_HEADS // SSM_GROUPS // 2) + kk
            psl = slice(k * LANES, (k + 1) * LANES)
            decay = jnp.where(causal, jnp.exp(cs_w[:, psl] - srow[k:k + 1, :]), 0.0)
            gmat = (cb2 * decay).astype(BF16)
            xp = xdt_b[:, psl]
            xblk = jnp.concatenate([jnp.where(low, xp, jnp.zeros_like(xp)),
                                    jnp.where(low, jnp.zeros_like(xp), xp)], axis=0)
            y_parts.append(jnp.dot(gmat, xblk, preferred_element_type=F32)
                           + y_off[:, kk * LANES:(kk + 1) * LANES])
    y = jnp.concatenate(y_parts, axis=1) + xs * dexp_ref[...]
    zz = z_ref[...]
    y = y * (zz * jax.nn.sigmoid(zz))
    outs = []
    for g in range(SSM_GROUPS):
        yg = y[:, g * GROUP_WIDTH:(g + 1) * GROUP_WIDTH]
        outs.append(yg * lax.rsqrt(jnp.mean(yg * yg, axis=-1, keepdims=True) + EPS))
    y_ref[...] = jnp.concatenate(outs, axis=1) * nw_ref[...]

    @pl.when(c == nc - 1)
    def _():
        stout_ref[...] = st[...]


def _ssd_constants():
    tri = np.tril(np.ones((CHUNK, CHUNK), np.float32))
    expand = np.zeros((LANES, D_INNER), np.float32)
    for e in range(SSM_HEADS):
        expand[e, e * SSM_HEAD_DIM:(e + 1) * SSM_HEAD_DIM] = 1.0
    sel = np.zeros((16, LANES), np.float32)
    for e in range(SSM_HEADS):
        sel[e // 2, e] = 1.0
    eye = np.eye(D_STATE, dtype=np.float32)
    return (jnp.asarray(tri, BF16), jnp.asarray(expand, BF16), jnp.asarray(sel, BF16), jnp.asarray(eye, BF16))


def _ssd_mixer(proj, conv0, st0, conv_w, conv_b, dt_bias_p, a_log_p, d_exp, norm_w, *, batch, nc):
    has_init = conv0 is not None
    if not has_init:
        conv0 = jnp.zeros((1, 8, CONV_DIM), F32)
        st0 = jnp.zeros((1, D_STATE, D_INNER), F32)
        init_map = lambda b, c: (0, 0, 0)
    else:
        init_map = lambda b, c: (b, 0, 0)
    tri, expand, sel, eye = _ssd_constants()
    row = lambda col: (lambda b, c: (b * nc + c, col))
    kern = functools.partial(_ssd_kernel, has_init=has_init)
    return pl.pallas_call(
        kern,
        grid=(batch, nc),
        in_specs=[pl.BlockSpec((CHUNK, CONV_DIM), row(XBC0 // CONV_DIM)),
                  pl.BlockSpec((CHUNK, D_INNER), row(Z0 // D_INNER)),
                  pl.BlockSpec((CHUNK, LANES), row(DT0 // LANES)),
                  pl.BlockSpec((None, 8, CONV_DIM), init_map),
                  pl.BlockSpec((None, D_STATE, D_INNER), init_map),
                  _const_spec((CONV_WIDTH, CONV_DIM)), _const_spec((1, CONV_DIM)),
                  _const_spec((1, LANES)), _const_spec((1, LANES)),
                  _const_spec((1, D_INNER)), _const_spec((1, D_INNER)),
                  _const_spec((CHUNK, CHUNK)), _const_spec((LANES, D_INNER)),
                  _const_spec((16, LANES)), _const_spec((D_STATE, D_STATE))],
        out_specs=[pl.BlockSpec((CHUNK, D_INNER), lambda b, c: (b * nc + c, 0)),
                   pl.BlockSpec((None, D_STATE, D_INNER), lambda b, c: (b, 0, 0))],
        out_shape=[jax.ShapeDtypeStruct((batch * nc * CHUNK, D_INNER), F32),
                   jax.ShapeDtypeStruct((batch, D_STATE, D_INNER), F32)],
        scratch_shapes=[pltpu.VMEM((8 + CHUNK, CONV_DIM), F32), pltpu.VMEM((D_STATE, D_INNER), F32)],
        name="ssd_mixer",
        compiler_params=_params("parallel", "arbitrary"),
    )(proj, proj, proj, conv0, st0, conv_w, conv_b, dt_bias_p, a_log_p, d_exp, norm_w, tri, expand, sel, eye)


ROW_SUBLANES = D_MODEL // LANES


def _store_rows(ref, x):
    for s in range(ROW_SUBLANES):
        ref[:, s, :] = x[:, s * LANES:(s + 1) * LANES]


def _load_rows(ref):
    return jnp.concatenate([ref[:, s, :] for s in range(ROW_SUBLANES)], axis=1)


def _mix_kernel(x_ref, attn_ref, ssm_ref, ga_ref, gb_ref, wa_ref, wb_ref, wo_ref, g_ref, wr_ref, br_ref,
                x1_ref, hn_ref, route_ref):
    a = jnp.dot(attn_ref[...].astype(BF16), wa_ref[...], preferred_element_type=F32)
    s = jnp.dot(ssm_ref[...].astype(BF16), wb_ref[...], preferred_element_type=F32)
    mixed = jax.nn.sigmoid(ga_ref[...]) * a + jax.nn.sigmoid(gb_ref[...]) * s
    x1 = x_ref[...] + jnp.dot(mixed.astype(BF16), wo_ref[...], preferred_element_type=F32)
    x1_ref[...] = x1
    ms = jnp.mean(x1 * x1, axis=-1, keepdims=True)
    hn = x1 * lax.rsqrt(ms + EPS) * g_ref[...]
    _store_rows(hn_ref, hn)

    logits = jnp.dot(hn, wr_ref[...], preferred_element_type=F32, precision=lax.Precision.HIGHEST) + br_ref[...]
    lane = lax.broadcasted_iota(jnp.int32, logits.shape, 1)
    lane_f = lane.astype(F32)
    big = float(LANES)
    is_g = lane < N_EXPERT_GROUPS
    gl = jnp.where(is_g, logits, -jnp.inf)
    gmax = jnp.max(gl, axis=-1, keepdims=True)
    gidx = jnp.min(jnp.where(gl == gmax, lane_f, big), axis=-1, keepdims=True)
    gsum = jnp.sum(jnp.where(is_g, jnp.exp(gl - gmax), 0.0), axis=-1, keepdims=True)
    g_top = 1.0 / gsum
    first = N_EXPERT_GROUPS + gidx * EXPERTS_PER_GROUP
    in_group = (lane_f >= first) & (lane_f < first + EXPERTS_PER_GROUP)
    el = jnp.where(in_group, logits, -jnp.inf)
    m1 = jnp.max(el, axis=-1, keepdims=True)
    i1 = jnp.min(jnp.where(el == m1, lane_f, big), axis=-1, keepdims=True)
    el2 = jnp.where(lane_f == i1, -jnp.inf, el)
    m2 = jnp.max(el2, axis=-1, keepdims=True)
    i2 = jnp.min(jnp.where(el2 == m2, lane_f, big), axis=-1, keepdims=True)
    r = jnp.exp(m2 - m1)
    w1 = g_top / (1.0 + r)
    w2 = g_top * r / (1.0 + r)
    route = jnp.where(lane == 0, i1 - N_EXPERT_GROUPS,
                      jnp.where(lane == 1, i2 - N_EXPERT_GROUPS,
                                jnp.where(lane == 2, w1, jnp.where(lane == 3, w2, 0.0))))
    route_ref[...] = route


def _mix_and_route(x2d, attn, ssm, proj, wa, wb, wo, norm_w, w_router, b_router):
    t = x2d.shape[0]
    tile = lambda w, col: pl.BlockSpec((ROW_TILE, w), lambda i: (i, col))
    return pl.pallas_call(
        _mix_kernel,
        grid=(t // ROW_TILE,),
        in_specs=[tile(D_MODEL, 0), tile(ATTN_WIDTH, 0), tile(D_INNER, 0),
                  tile(D_MODEL, GA0 // D_MODEL), tile(D_MODEL, GB0 // D_MODEL),
                  _const_spec((ATTN_WIDTH, D_MODEL)), _const_spec((D_INNER, D_MODEL)),
                  _const_spec((D_MODEL, D_MODEL)), _const_spec((1, D_MODEL)),
                  _const_spec((D_MODEL, LANES)), _const_spec((1, LANES))],
        out_specs=[tile(D_MODEL, 0), pl.BlockSpec((ROW_TILE, ROW_SUBLANES, LANES), lambda i: (i, 0, 0)),
                   tile(LANES, 0)],
        out_shape=[jax.ShapeDtypeStruct((t, D_MODEL), F32),
                   jax.ShapeDtypeStruct((t, ROW_SUBLANES, LANES), F32),
                   jax.ShapeDtypeStruct((t, LANES), F32)],
        name="mix_and_route",
        compiler_params=_params("parallel"),
    )(x2d, attn, ssm, proj, proj, wa, wb, wo, norm_w, w_router, b_router)


def _expert_kernel(bexp_ref, bval_ref, tok_ref, tok_next_ref, hn_hbm, wg_ref, wu_ref, wd_ref, y_ref, xbuf, sem):
    i = pl.program_id(0)
    n = pl.num_programs(0)
    slot = i % 2

    def gather(tok, dst_slot):
        def body(r, carry):
            pltpu.make_async_copy(hn_hbm.at[tok[0, r]], xbuf.at[dst_slot, r], sem.at[dst_slot]).start()
            return carry
        lax.fori_loop(0, MOE_BLOCK, body, 0, unroll=8)

    @pl.when((i == 0) & (bval_ref[0] > 0))
    def _():
        gather(tok_ref, 0)

    @pl.when((i + 1 < n) & (bval_ref[jnp.minimum(i + 1, n - 1)] > 0))
    def _():
        gather(tok_next_ref, 1 - slot)

    @pl.when(bval_ref[i] > 0)
    def _():
        pltpu.make_async_copy(hn_hbm.at[pl.ds(0, MOE_BLOCK)], xbuf.at[slot], sem.at[slot]).wait()
        xb = _load_rows(xbuf.at[slot]).astype(BF16)
        gate = jnp.dot(xb, wg_ref[...], preferred_element_type=F32)
        up = jnp.dot(xb, wu_ref[...], preferred_element_type=F32)
        hid = (gate * jax.nn.sigmoid(gate) * up).astype(BF16)
        _store_rows(y_ref, jnp.dot(hid, wd_ref[...], preferred_element_type=F32))

    @pl.when(bval_ref[i] == 0)
    def _():
        y_ref[...] = jnp.zeros((MOE_BLOCK, ROW_SUBLANES, LANES), F32)


def _routed_experts(hn, slot_tok, block_expert, block_valid, wg, wu, wd):
    n_blocks = block_expert.shape[0]
    tok3 = slot_tok.reshape(n_blocks, 1, MOE_BLOCK)
    grid_spec = pltpu.PrefetchScalarGridSpec(
        num_scalar_prefetch=2,
        grid=(n_blocks,),
        in_specs=[pl.BlockSpec((None, 1, MOE_BLOCK), lambda i, be, bv: (i, 0, 0), memory_space=pltpu.SMEM),
                  pl.BlockSpec((None, 1, MOE_BLOCK), lambda i, be, bv: (jnp.minimum(i + 1, n_blocks - 1), 0, 0),
                               memory_space=pltpu.SMEM),
                  pl.BlockSpec(memory_space=pl.ANY),
                  pl.BlockSpec((None, D_MODEL, D_EXPERT), lambda i, be, bv: (be[i], 0, 0)),
                  pl.BlockSpec((None, D_MODEL, D_EXPERT), lambda i, be, bv: (be[i], 0, 0)),
                  pl.BlockSpec((None, D_EXPERT, D_MODEL), lambda i, be, bv: (be[i], 0, 0))],
        out_specs=pl.BlockSpec((MOE_BLOCK, ROW_SUBLANES, LANES), lambda i, be, bv: (i, 0, 0)),
        scratch_shapes=[pltpu.VMEM((2, MOE_BLOCK, ROW_SUBLANES, LANES), F32), pltpu.SemaphoreType.DMA((2,))],
    )
    return pl.pallas_call(
        _expert_kernel,
        grid_spec=grid_spec,
        out_shape=jax.ShapeDtypeStruct((n_blocks * MOE_BLOCK, ROW_SUBLANES, LANES), F32),
        name="routed_experts",
        compiler_params=_params("arbitrary"),
    )(block_expert, block_valid, tok3, tok3, hn, wg, wu, wd)


def _combine_kernel(dest_ref, x1_ref, route_ref, g_ref, yb_hbm, o_ref, ybuf, sem):
    def body(r, carry):
        for k in range(TOP_K):
            pltpu.make_async_copy(yb_hbm.at[dest_ref[0, TOP_K * r + k]], ybuf.at[k, r], sem.at[k]).start()
        return carry
    lax.fori_loop(0, ROW_TILE, body, 0, unroll=8)
    for k in range(TOP_K):
        pltpu.make_async_copy(yb_hbm.at[pl.ds(0, ROW_TILE)], ybuf.at[k], sem.at[k]).wait()
    route = route_ref[...]
    y = _load_rows(ybuf.at[0]) * route[:, 2:3] + _load_rows(ybuf.at[1]) * route[:, 3:4]
    x2 = x1_ref[...] + y
    ms = jnp.mean(x2 * x2, axis=-1, keepdims=True)
    o_ref[...] = x2 * lax.rsqrt(ms + EPS) * g_ref[...]


def _combine_and_norm(dest, x1, route, norm_w, yb):
    t = x1.shape[0]
    n_tiles = t // ROW_TILE
    dest3 = dest.reshape(n_tiles, 1, ROW_TILE * TOP_K)
    return pl.pallas_call(
        _combine_kernel,
        grid=(n_tiles,),
        in_specs=[pl.BlockSpec((None, 1, ROW_TILE * TOP_K), lambda i: (i, 0, 0), memory_space=pltpu.SMEM),
                  pl.BlockSpec((ROW_TILE, D_MODEL), lambda i: (i, 0)),
                  pl.BlockSpec((ROW_TILE, LANES), lambda i: (i, 0)),
                  _const_spec((1, D_MODEL)),
                  pl.BlockSpec(memory_space=pl.ANY)],
        out_specs=pl.BlockSpec((ROW_TILE, D_MODEL), lambda i: (i, 0)),
        out_shape=jax.ShapeDtypeStruct((t, D_MODEL), F32),
        scratch_shapes=[pltpu.VMEM((TOP_K, ROW_TILE, ROW_SUBLANES, LANES), F32),
                        pltpu.SemaphoreType.DMA((TOP_K,))],
        name="combine_and_norm",
        compiler_params=_params("arbitrary"),
    )(dest3, x1, route, norm_w, yb)


def _dispatch_plan(route):
    t = route.shape[0]
    a = t * TOP_K
    flat_e = route[:, 0:TOP_K].astype(jnp.int32).reshape(a)
    onehot = (flat_e[:, None] == jnp.arange(N_EXPERTS, dtype=jnp.int32)[None, :]).astype(jnp.int32)
    rank = jnp.take_along_axis(jnp.cumsum(onehot, axis=0), flat_e[:, None], axis=1)[:, 0] - 1
    counts = jnp.sum(onehot, axis=0)
    padded = (counts + MOE_BLOCK - 1) // MOE_BLOCK * MOE_BLOCK
    pad_end = jnp.cumsum(padded)
    pad_start = pad_end - padded
    dest = pad_start[flat_e] + rank
    n_blocks = -(-(a + N_EXPERTS * (MOE_BLOCK - 1)) // MOE_BLOCK)
    flat_tok = jnp.arange(a, dtype=jnp.int32) // TOP_K
    slot_tok = jnp.zeros((n_blocks * MOE_BLOCK,), jnp.int32).at[dest].set(flat_tok)
    block_start = jnp.arange(n_blocks, dtype=jnp.int32) * MOE_BLOCK
    block_expert = jnp.minimum(jnp.sum((pad_end[None, :] <= block_start[:, None]).astype(jnp.int32), axis=1),
                               N_EXPERTS - 1)
    block_valid = (block_start < pad_end[-1]).astype(jnp.int32)
    return dest.astype(jnp.int32), slot_tok, block_expert, block_valid


def _rel_bias_band(rel_table):
    dist = jnp.arange(CHUNK)[:, None] + ATTN_REACH - jnp.arange(BAND)[None, :]
    return rel_table[:, jnp.clip(dist, -MAX_REL, MAX_REL) + MAX_REL].astype(F32)


def _pad_lanes(v, width=LANES):
    return jnp.pad(v.astype(F32), (0, width - v.shape[0])).reshape(1, width)


def _trunk(x, kv_cache, conv_state, ssm_state, w):
    batch, l, _ = x.shape
    nc = l // CHUNK
    t = batch * l
    x2d = x.reshape(t, D_MODEL)
    proj = _in_projection(x2d, w["norm_mix"], w["w_proj"])
    k_new = proj[:, K0:K0 + ATTN_WIDTH].reshape(batch, l, ATTN_WIDTH)
    v_new = proj[:, V0:V0 + ATTN_WIDTH].reshape(batch, l, ATTN_WIDTH)
    if kv_cache is None:
        kv_spec = [pl.BlockSpec((l, ATTN_WIDTH), lambda b, c: (b, K0 // ATTN_WIDTH)),
                   pl.BlockSpec((l, ATTN_WIDTH), lambda b, c: (b, V0 // ATTN_WIDTH))]
        attn = _band_attention(proj, Q0 // ATTN_WIDTH, proj, proj, kv_spec, w["bias"],
                               batch=batch, nc=nc, lk=l, pad_rows=ATTN_REACH)
        keep = min(ATTN_REACH, l)
        k_keep, v_keep = k_new[:, l - keep:], v_new[:, l - keep:]
    else:
        cache_k, cache_v = kv_cache
        wlen = cache_k.shape[1]
        kk = jnp.concatenate([cache_k.reshape(batch, wlen, ATTN_WIDTH), k_new], axis=1)
        vv = jnp.concatenate([cache_v.reshape(batch, wlen, ATTN_WIDTH), v_new], axis=1)
        spec = pl.BlockSpec((None, wlen + l, ATTN_WIDTH), lambda b, c: (b, 0, 0))
        attn = _band_attention(proj, Q0 // ATTN_WIDTH, kk, vv, [spec, spec], w["bias"],
                               batch=batch, nc=nc, lk=wlen + l, pad_rows=BAND - (wlen + l))
        k_keep, v_keep = kk[:, l:], vv[:, l:]
    xbc = proj[:, XBC0:XBC0 + CONV_DIM].reshape(batch, l, CONV_DIM)
    if conv_state is None:
        conv0 = st0 = None
        conv_new = xbc[:, l - (CONV_WIDTH - 1):]
    else:
        conv0 = jnp.pad(conv_state, ((0, 0), (8 - (CONV_WIDTH - 1), 0), (0, 0)))
        st0 = jnp.transpose(ssm_state.reshape(batch, D_INNER, D_STATE), (0, 2, 1))
        conv_new = jnp.concatenate([conv_state, xbc], axis=1)[:, l:]
    ssm, st_t = _ssd_mixer(proj, conv0, st0, w["conv_w"], w["conv_b"], w["dt_bias"], w["a_log"], w["d_exp"],
                           w["ssm_norm"], batch=batch, nc=nc)
    ssm_new = jnp.transpose(st_t, (0, 2, 1)).reshape(batch, SSM_HEADS, SSM_HEAD_DIM, D_STATE)
    x1, hn, route = _mix_and_route(x2d, attn, ssm, proj, w["wa"], w["wb"], w["wo"], w["norm_ffn"],
                                   w["w_router"], w["b_router"])
    dest, slot_tok, block_expert, block_valid = _dispatch_plan(route)
    yb = _routed_experts(hn, slot_tok, block_expert, block_valid, w["w_gate"], w["w_up"], w["w_down"])
    y = _combine_and_norm(dest, x1, route, w["norm_final"], yb)
    heads = lambda u: u.reshape(batch, u.shape[1], N_HEADS_A, HEAD_DIM_A)
    return y.reshape(batch, l, D_MODEL), heads(k_keep), heads(v_keep), conv_new, ssm_new


def kernel(x_prompt, x_sample, cache_attn_k, cache_attn_v, state_conv, state_ssm, norm_mix, w_in, rel_bias, conv_w, conv_b, dt_bias, a_log, d_skip, ssm_norm, w_branch_a, w_branch_b, w_out, norm_ffn, w_router_group, b_router_group, w_router_expert, b_router_expert, w_gate, w_up, w_down, norm_final):
    assert norm_mix.shape[0] == 1, "single-layer trunk"
    q, k, v, z, xbc, dtc, ga, gb = jnp.split(w_in[0], np.cumsum(
        (ATTN_WIDTH, ATTN_WIDTH, ATTN_WIDTH, D_INNER, CONV_DIM, SSM_HEADS, D_MODEL, D_MODEL))[:-1].tolist(), axis=1)
    w_proj = jnp.concatenate([z, ga, gb, xbc, q, k, v, jnp.pad(dtc, ((0, 0), (0, LANES - SSM_HEADS)))],
                             axis=1).astype(BF16)
    w_router = jnp.pad(jnp.concatenate([w_router_group[0], w_router_expert[0]], axis=1),
                       ((0, 0), (0, LANES - N_EXPERT_GROUPS - N_EXPERTS)))
    b_router = _pad_lanes(jnp.concatenate([b_router_group[0], b_router_expert[0]]))
    w = dict(
        norm_mix=norm_mix[0].reshape(1, D_MODEL), w_proj=w_proj, bias=_rel_bias_band(rel_bias[0]),
        conv_w=conv_w[0], conv_b=conv_b[0].reshape(1, CONV_DIM), dt_bias=_pad_lanes(dt_bias[0]),
        a_log=_pad_lanes(a_log[0]), d_exp=jnp.repeat(d_skip[0], SSM_HEAD_DIM).reshape(1, D_INNER),
        ssm_norm=ssm_norm[0].reshape(1, D_INNER),
        wa=w_branch_a[0].astype(BF16), wb=w_branch_b[0].astype(BF16), wo=w_out[0].astype(BF16),
        norm_ffn=norm_ffn[0].reshape(1, D_MODEL), w_router=w_router, b_router=b_router,
        w_gate=w_gate[0].astype(BF16), w_up=w_up[0].astype(BF16), w_down=w_down[0].astype(BF16),
        norm_final=norm_final.reshape(1, D_MODEL),
    )
    yp, kp, vp, cp, sp = _trunk(x_prompt, None, None, None, w)
    ys, ks, vs, cs, ss = _trunk(x_sample, (cache_attn_k[0], cache_attn_v[0]), state_conv[0], state_ssm[0], w)
    return (yp, ys, kp[None], vp[None], cp[None], sp[None], ks[None], vs[None], cs[None], ss[None])
```

```python
import functools
import math

import numpy as np
import jax
import jax.numpy as jnp
from jax import lax
from jax.experimental import pallas as pl
from jax.experimental.pallas import tpu as pltpu

F32 = jnp.float32
BF16 = jnp.bfloat16

D_MODEL = 1024
CHUNK = 64
LEFT_CHUNKS = 8
ATTN_REACH = LEFT_CHUNKS * CHUNK
BAND = ATTN_REACH + CHUNK
N_HEADS_A = 8
HEAD_DIM_A = 64
ATTN_WIDTH = N_HEADS_A * HEAD_DIM_A
MAX_REL = 256
ATTN_SCALE = 1.0 / math.sqrt(HEAD_DIM_A)
NEG_INF = -1e30
SSM_HEADS = 16
SSM_HEAD_DIM = 64
D_INNER = SSM_HEADS * SSM_HEAD_DIM
SSM_GROUPS = 2
GROUP_WIDTH = D_INNER // SSM_GROUPS
D_STATE = 128
CONV_WIDTH = 4
CONV_DIM = D_INNER + 2 * SSM_GROUPS * D_STATE
N_EXPERT_GROUPS = 4
EXPERTS_PER_GROUP = 8
N_EXPERTS = N_EXPERT_GROUPS * EXPERTS_PER_GROUP
TOP_K = 2
D_EXPERT = 512
MOE_BLOCK = 128
EPS = 1e-6

LANES = 128
SUBLANES = 8
Z0, GA0, GB0, XBC0, Q0, K0, V0, DT0 = 0, 1024, 2048, 3072, 4608, 5120, 5632, 6144
PROJ_WIDTH = DT0 + LANES
PROJ_TILE = 896
ROW_TILE = 256
ATTN_CHUNKS_PER_STEP = 2
VMEM_LIMIT = 56 * 1024 * 1024

_NT = (((1,), (1,)), ((), ()))


def _params(*sem, **kw):
    return pltpu.CompilerParams(dimension_semantics=sem, vmem_limit_bytes=VMEM_LIMIT, **kw)


def _const_spec(shape):
    nd = len(shape)
    return pl.BlockSpec(shape, lambda *_: (0,) * nd)


def _inproj_kernel(x_ref, g_ref, w_ref, o_ref):
    x = x_ref[...]
    ms = jnp.mean(x * x, axis=-1, keepdims=True)
    h = (x * lax.rsqrt(ms + EPS) * g_ref[...]).astype(BF16)
    for j in range(0, PROJ_WIDTH, PROJ_TILE):
        o_ref[:, j:j + PROJ_TILE] = jnp.dot(h, w_ref[:, j:j + PROJ_TILE], preferred_element_type=F32)


def _in_projection(x2d, norm_w, w_proj):
    t = x2d.shape[0]
    return pl.pallas_call(
        _inproj_kernel,
        grid=(t // ROW_TILE,),
        in_specs=[pl.BlockSpec((ROW_TILE, D_MODEL), lambda i: (i, 0)),
                  _const_spec((1, D_MODEL)),
                  _const_spec((D_MODEL, PROJ_WIDTH))],
        out_specs=pl.BlockSpec((ROW_TILE, PROJ_WIDTH), lambda i: (i, 0)),
        out_shape=jax.ShapeDtypeStruct((t, PROJ_WIDTH), F32),
        name="in_projection",
        compiler_params=_params("parallel"),
    )(x2d, norm_w, w_proj)


LOG2E = math.log2(math.e)


def _attn_kernel(q_ref, k_ref, v_ref, bias_ref, o_ref, kpad, vpad, *, pad_rows, lk, qc):
    rows = qc * CHUNK
    window = ATTN_REACH + rows
    c = pl.program_id(1)

    @pl.when(c == 0)
    def _():
        if pad_rows:
            kpad[0:pad_rows, :] = jnp.zeros((pad_rows, ATTN_WIDTH), BF16)
            vpad[0:pad_rows, :] = jnp.zeros((pad_rows, ATTN_WIDTH), BF16)
        kpad[pad_rows:pad_rows + lk, :] = k_ref[...].astype(BF16)
        vpad[pad_rows:pad_rows + lk, :] = v_ref[...].astype(BF16)

    start = pl.multiple_of(c * rows, rows)

    def body(mask_start):
        kw = kpad[pl.ds(start, window), :]
        vw = vpad[pl.ds(start, window), :]
        q = (q_ref[...] * (ATTN_SCALE * LOG2E)).astype(BF16)
        low = lax.broadcasted_iota(jnp.int32, (rows, LANES), 1) < HEAD_DIM_A
        if mask_start:
            valid = lax.broadcasted_iota(jnp.int32, (rows, window), 1) + start >= pad_rows
        for hp in range(N_HEADS_A // 2):
            sl = slice(LANES * hp, LANES * (hp + 1))
            q2, k2, v2 = q[:, sl], kw[:, sl], vw[:, sl]
            outs = []
            for sub in range(2):
                qm = jnp.where(low if sub == 0 else jnp.logical_not(low), q2, jnp.zeros_like(q2))
                s = lax.dot_general(qm, k2, _NT, preferred_element_type=F32) + bias_ref[2 * hp + sub]
                if mask_start:
                    s = jnp.where(valid, s, NEG_INF)
                m = jnp.max(s, axis=-1, keepdims=True)
                p = jnp.exp2(s - m)
                l = jnp.sum(p, axis=-1, keepdims=True)
                o = jnp.dot(p.astype(BF16), v2, preferred_element_type=F32)
                outs.append(o / l)
            o_ref[:, sl] = jnp.where(low, outs[0], outs[1])

    if pad_rows:
        pl.when(start < pad_rows)(lambda: body(True))
        pl.when(start >= pad_rows)(lambda: body(False))
    else:
        body(False)


def _band_attention(q_arr, q_col, k_arr, v_arr, kv_spec, bias, *, batch, nc, lk, pad_rows, qc):
    steps = nc // qc
    rows = qc * CHUNK
    kern = functools.partial(_attn_kernel, pad_rows=pad_rows, lk=lk, qc=qc)
    return pl.pallas_call(
        kern,
        grid=(batch, steps),
        in_specs=[pl.BlockSpec((rows, ATTN_WIDTH), lambda b, c: (b * steps + c, q_col)),
                  kv_spec[0], kv_spec[1],
                  _const_spec((N_HEADS_A, rows, ATTN_REACH + rows))],
        out_specs=pl.BlockSpec((rows, ATTN_WIDTH), lambda b, c: (b * steps + c, 0)),
        out_shape=jax.ShapeDtypeStruct((batch * nc * CHUNK, ATTN_WIDTH), F32),
        scratch_shapes=[pltpu.VMEM((pad_rows + lk, ATTN_WIDTH), BF16),
                        pltpu.VMEM((pad_rows + lk, ATTN_WIDTH), BF16)],
        name="band_attention",
        compiler_params=_params("parallel", "arbitrary"),
    )(q_arr, k_arr, v_arr, bias)


def _split_bf16(x, parts):
    out = []
    for _ in range(parts - 1):
        h = x.astype(BF16)
        out.append(h)
        x = x - h.astype(F32)
    out.append(x.astype(BF16))
    return out


def _ssd_kernel(xbc_ref, z_ref, dt_ref, conv0_ref, st0_ref, cw_ref, cb_ref, dtb_ref, alog_ref,
                dexp_ref, nw_ref, tri_ref, exp_ref, sel_ref, eye_ref,
                y_ref, stout_ref, xwin, st, *, has_init):
    c = pl.program_id(1)
    nc = pl.num_programs(1)

    @pl.when(c == 0)
    def _():
        if has_init:
            xwin[0:8, :] = conv0_ref[...]
            st[...] = st0_ref[...]
        else:
            xwin[0:8, :] = jnp.zeros((8, CONV_DIM), F32)
            st[...] = jnp.zeros((D_STATE, D_INNER), F32)

    xwin[8:8 + CHUNK, :] = xbc_ref[...]
    conv = cb_ref[...]
    for j in range(CONV_WIDTH):
        conv = conv + xwin[5 + j:5 + j + CHUNK, :] * cw_ref[j:j + 1, :]
    xwin[5:8, :] = xwin[5 + CHUNK:8 + CHUNK, :]
    u = conv * jax.nn.sigmoid(conv)
    xs = u[:, 0:D_INNER]

    dt_in = dt_ref[...] + dtb_ref[...]
    dt = jnp.maximum(dt_in, 0.0) + jnp.log1p(jnp.exp(-jnp.abs(dt_in)))
    da = dt * (-jnp.exp(alog_ref[...]))
    tri = tri_ref[...]
    cs3 = jnp.dot(tri, jnp.concatenate(_split_bf16(da, 3), axis=1), preferred_element_type=F32)
    cs = cs3[:, 0:LANES] + cs3[:, LANES:2 * LANES] + cs3[:, 2 * LANES:3 * LANES]
    cs_last = cs[CHUNK - 1:CHUNK, :]
    ecs = jnp.exp(cs)
    to_end = jnp.exp(cs_last - cs)

    stack = jnp.concatenate(_split_bf16(dt, 2) + _split_bf16(ecs, 2) + _split_bf16(to_end, 2)
                            + _split_bf16(cs, 3), axis=0)
    wide = jnp.dot(stack, exp_ref[...], preferred_element_type=F32)
    rows = [wide[i * CHUNK:(i + 1) * CHUNK, :] for i in range(9)]
    dt_w = rows[0] + rows[1]
    ecs_w = rows[2] + rows[3]
    to_end_w = rows[4] + rows[5]
    cs_w = rows[6] + rows[7] + rows[8]

    lane = lax.broadcasted_iota(jnp.int32, (CHUNK, LANES), 1)
    even = (lane % 2) == 0
    csm = jnp.concatenate([jnp.where(even, cs, 0.0), jnp.where(even, 0.0, cs)], axis=0)
    srow3 = lax.dot_general(sel_ref[...], jnp.concatenate(_split_bf16(csm, 3), axis=0), _NT,
                            preferred_element_type=F32)
    srow = srow3[:, 0:LANES] + srow3[:, LANES:2 * LANES] + srow3[:, 2 * LANES:3 * LANES]

    xdt = xs * dt_w
    xdt_b = xdt.astype(BF16)
    xw_b = (xdt * to_end_w).astype(BF16)
    low = lane < SSM_HEAD_DIM
    t_idx = lax.broadcasted_iota(jnp.int32, (CHUNK, LANES), 0)
    causal = t_idx >= (lane % SSM_HEAD_DIM)
    eye = eye_ref[...]

    y_parts = []
    for g in range(SSM_GROUPS):
        b_g = u[:, D_INNER + g * D_STATE:D_INNER + (g + 1) * D_STATE].astype(BF16)
        c_g = u[:, D_INNER + (SSM_GROUPS + g) * D_STATE:D_INNER + (SSM_GROUPS + g + 1) * D_STATE].astype(BF16)
        gsl = slice(g * GROUP_WIDTH, (g + 1) * GROUP_WIDTH)
        st_g = st[:, gsl]
        y_off = jnp.dot(c_g, st_g.astype(BF16), preferred_element_type=F32) * ecs_w[:, gsl]
        b_t = lax.dot_general(eye, b_g, _NT, preferred_element_type=F32).astype(BF16)
        st[:, gsl] = st_g * ecs_w[CHUNK - 1:CHUNK, gsl] + jnp.dot(b_t, xw_b[:, gsl], preferred_element_type=F32)
        cb2 = lax.dot_general(c_g, jnp.concatenate([b_g, b_g], axis=0), _NT, preferred_element_type=F32)
        for kk in range(SSM_HEADS // SSM_GROUPS // 2):
            k = g * (SSM_HEADS // SSM_GROUPS // 2) + kk
            psl = slice(k * LANES, (k + 1) * LANES)
            decay = jnp.where(causal, jnp.exp(cs_w[:, psl] - srow[k:k + 1, :]), 0.0)
            gmat = (cb2 * decay).astype(BF16)
            xp = xdt_b[:, psl]
            xblk = jnp.concatenate([jnp.where(low, xp, jnp.zeros_like(xp)),
                                    jnp.where(low, jnp.zeros_like(xp), xp)], axis=0)
            y_parts.append(jnp.dot(gmat, xblk, preferred_element_type=F32)
                           + y_off[:, kk * LANES:(kk + 1) * LANES])
    y = jnp.concatenate(y_parts, axis=1) + xs * dexp_ref[...]
    zz = z_ref[...]
    y = y * (zz * jax.nn.sigmoid(zz))
    outs = []
    for g in range(SSM_GROUPS):
        yg = y[:, g * GROUP_WIDTH:(g + 1) * GROUP_WIDTH]
        outs.append(yg * lax.rsqrt(jnp.mean(yg * yg, axis=-1, keepdims=True) + EPS))
    y_ref[...] = jnp.concatenate(outs, axis=1) * nw_ref[...]

    @pl.when(c == nc - 1)
    def _():
        stout_ref[...] = st[...]


def _ssd_constants():
    tri = np.tril(np.ones((CHUNK, CHUNK), np.float32))
    expand = np.zeros((LANES, D_INNER), np.float32)
    for e in range(SSM_HEADS):
        expand[e, e * SSM_HEAD_DIM:(e + 1) * SSM_HEAD_DIM] = 1.0
    sel = np.zeros((16, LANES), np.float32)
    for e in range(SSM_HEADS):
        sel[e // 2, e] = 1.0
    eye = np.eye(D_STATE, dtype=np.float32)
    return (jnp.asarray(tri, BF16), jnp.asarray(expand, BF16), jnp.asarray(sel, BF16), jnp.asarray(eye, BF16))


def _ssd_mixer(proj, conv0, st0, conv_w, conv_b, dt_bias_p, a_log_p, d_exp, norm_w, *, batch, nc):
    has_init = conv0 is not None
    if not has_init:
        conv0 = jnp.zeros((1, 8, CONV_DIM), F32)
        st0 = jnp.zeros((1, D_STATE, D_INNER), F32)
        init_map = lambda b, c: (0, 0, 0)
    else:
        init_map = lambda b, c: (b, 0, 0)
    tri, expand, sel, eye = _ssd_constants()
    row = lambda col: (lambda b, c: (b * nc + c, col))
    kern = functools.partial(_ssd_kernel, has_init=has_init)
    return pl.pallas_call(
        kern,
        grid=(batch, nc),
        in_specs=[pl.BlockSpec((CHUNK, CONV_DIM), row(XBC0 // CONV_DIM)),
                  pl.BlockSpec((CHUNK, D_INNER), row(Z0 // D_INNER)),
                  pl.BlockSpec((CHUNK, LANES), row(DT0 // LANES)),
                  pl.BlockSpec((None, 8, CONV_DIM), init_map),
                  pl.BlockSpec((None, D_STATE, D_INNER), init_map),
                  _const_spec((CONV_WIDTH, CONV_DIM)), _const_spec((1, CONV_DIM)),
                  _const_spec((1, LANES)), _const_spec((1, LANES)),
                  _const_spec((1, D_INNER)), _const_spec((1, D_INNER)),
                  _const_spec((CHUNK, CHUNK)), _const_spec((LANES, D_INNER)),
                  _const_spec((16, LANES)), _const_spec((D_STATE, D_STATE))],
        out_specs=[pl.BlockSpec((CHUNK, D_INNER), lambda b, c: (b * nc + c, 0)),
                   pl.BlockSpec((None, D_STATE, D_INNER), lambda b, c: (b, 0, 0))],
        out_shape=[jax.ShapeDtypeStruct((batch * nc * CHUNK, D_INNER), F32),
                   jax.ShapeDtypeStruct((batch, D_STATE, D_INNER), F32)],
        scratch_shapes=[pltpu.VMEM((8 + CHUNK, CONV_DIM), F32), pltpu.VMEM((D_STATE, D_INNER), F32)],
        name="ssd_mixer",
        compiler_params=_params("parallel", "arbitrary"),
    )(proj, proj, proj, conv0, st0, conv_w, conv_b, dt_bias_p, a_log_p, d_exp, norm_w, tri, expand, sel, eye)


ROW_SUBLANES = D_MODEL // LANES
assert ROW_SUBLANES == SUBLANES


def _store_rows(ref, x, rows, offset=0):
    for s in range(ROW_SUBLANES):
        ref[pl.ds(offset + s, rows, stride=ROW_SUBLANES), :] = x[:, s * LANES:(s + 1) * LANES]


def _load_rows(ref, rows, offset=0):
    return jnp.concatenate([ref[pl.ds(offset + s, rows, stride=ROW_SUBLANES), :] for s in range(ROW_SUBLANES)],
                           axis=1)


def _row_tile(ref, idx8):
    return ref.at[pl.ds(pl.multiple_of(idx8, ROW_SUBLANES), ROW_SUBLANES)]


def _mix_kernel(x_ref, attn_ref, ssm_ref, ga_ref, gb_ref, wa_ref, wb_ref, wo_ref, g_ref, wr_ref, br_ref, slt_ref,
                x1_ref, hn_ref, route_ref, counts_ref, carry):
    @pl.when(pl.program_id(0) == 0)
    def _():
        carry[...] = jnp.zeros((1, LANES), F32)

    a = jnp.dot(attn_ref[...].astype(BF16), wa_ref[...], preferred_element_type=F32)
    s = jnp.dot(ssm_ref[...].astype(BF16), wb_ref[...], preferred_element_type=F32)
    mixed = jax.nn.sigmoid(ga_ref[...]) * a + jax.nn.sigmoid(gb_ref[...]) * s
    x1 = x_ref[...] + jnp.dot(mixed.astype(BF16), wo_ref[...], preferred_element_type=F32)
    x1_ref[...] = x1
    ms = jnp.mean(x1 * x1, axis=-1, keepdims=True)
    hn = x1 * lax.rsqrt(ms + EPS) * g_ref[...]
    _store_rows(hn_ref, hn, ROW_TILE)

    hi = hn.astype(BF16)
    lo = (hn - hi.astype(F32)).astype(BF16)
    both = jnp.dot(hi, wr_ref[...], preferred_element_type=F32)
    logits = (both[:, 0:LANES] + both[:, LANES:2 * LANES]
              + jnp.dot(lo, wr_ref[:, 0:LANES], preferred_element_type=F32) + br_ref[...])
    lane = lax.broadcasted_iota(jnp.int32, logits.shape, 1)
    lane_f = lane.astype(F32)
    big = float(LANES)
    is_g = lane < N_EXPERT_GROUPS
    gl = jnp.where(is_g, logits, -jnp.inf)
    gmax = jnp.max(gl, axis=-1, keepdims=True)
    gidx = jnp.min(jnp.where(gl == gmax, lane_f, big), axis=-1, keepdims=True)
    gsum = jnp.sum(jnp.where(is_g, jnp.exp(gl - gmax), 0.0), axis=-1, keepdims=True)
    g_top = 1.0 / gsum
    first = N_EXPERT_GROUPS + gidx * EXPERTS_PER_GROUP
    in_group = (lane_f >= first) & (lane_f < first + EXPERTS_PER_GROUP)
    el = jnp.where(in_group, logits, -jnp.inf)
    m1 = jnp.max(el, axis=-1, keepdims=True)
    i1 = jnp.min(jnp.where(el == m1, lane_f, big), axis=-1, keepdims=True)
    el2 = jnp.where(lane_f == i1, -jnp.inf, el)
    m2 = jnp.max(el2, axis=-1, keepdims=True)
    i2 = jnp.min(jnp.where(el2 == m2, lane_f, big), axis=-1, keepdims=True)
    r = jnp.exp(m2 - m1)
    w1 = g_top / (1.0 + r)
    w2 = g_top * r / (1.0 + r)
    e1 = i1 - N_EXPERT_GROUPS
    e2 = i2 - N_EXPERT_GROUPS

    oh1 = lane_f == e1
    oh2 = lane_f == e2
    hot1 = jnp.where(oh1, 1.0, 0.0)
    hot2 = jnp.where(oh2, 1.0, 0.0)
    onehots = jnp.concatenate([hot1, hot2], axis=1).astype(BF16)
    before = jnp.dot(slt_ref[...], onehots, preferred_element_type=F32)
    cnt1 = jnp.sum(hot1, axis=0, keepdims=True)
    cnt2 = jnp.sum(hot2, axis=0, keepdims=True)
    base = carry[...]
    rank1 = jnp.sum(jnp.where(oh1, before[:, 0:LANES] + base, 0.0), axis=-1, keepdims=True)
    rank2 = jnp.sum(jnp.where(oh2, before[:, LANES:2 * LANES] + (base + cnt1), 0.0), axis=-1, keepdims=True)
    carry[...] = base + cnt1 + cnt2
    counts_ref[...] = carry[...]

    route = jnp.zeros_like(logits)
    for col, val in enumerate((e1, e2, w1, w2, rank1, rank2)):
        route = jnp.where(lane == col, val, route)
    route_ref[...] = route


def _mix_and_route(x2d, attn, ssm, proj, wa, wb, wo, norm_w, w_router, b_router):
    t = x2d.shape[0]
    tile = lambda w, col: pl.BlockSpec((ROW_TILE, w), lambda i: (i, col))
    slt = jnp.asarray(np.tril(np.ones((ROW_TILE, ROW_TILE), np.float32), k=-1), BF16)
    return pl.pallas_call(
        _mix_kernel,
        grid=(t // ROW_TILE,),
        in_specs=[tile(D_MODEL, 0), tile(ATTN_WIDTH, 0), tile(D_INNER, 0),
                  tile(D_MODEL, GA0 // D_MODEL), tile(D_MODEL, GB0 // D_MODEL),
                  _const_spec((ATTN_WIDTH, D_MODEL)), _const_spec((D_INNER, D_MODEL)),
                  _const_spec((D_MODEL, D_MODEL)), _const_spec((1, D_MODEL)),
                  _const_spec((D_MODEL, 2 * LANES)), _const_spec((1, LANES)),
                  _const_spec((ROW_TILE, ROW_TILE))],
        out_specs=[tile(D_MODEL, 0), pl.BlockSpec((ROW_TILE * ROW_SUBLANES, LANES), lambda i: (i, 0)),
                   tile(LANES, 0), _const_spec((1, LANES))],
        out_shape=[jax.ShapeDtypeStruct((t, D_MODEL), F32),
                   jax.ShapeDtypeStruct((t * ROW_SUBLANES, LANES), F32),
                   jax.ShapeDtypeStruct((t, LANES), F32),
                   jax.ShapeDtypeStruct((1, LANES), F32)],
        scratch_shapes=[pltpu.VMEM((1, LANES), F32)],
        name="mix_and_route",
        compiler_params=_params("arbitrary"),
    )(x2d, attn, ssm, proj, proj, wa, wb, wo, norm_w, w_router, b_router, slt)


TILE_ROW_COPIES = ROW_TILE * TOP_K
INDEX_BATCH = 8


def _tile_wait(ref_a, ref_b, sem):
    n = TILE_ROW_COPIES * ROW_SUBLANES
    pltpu.make_async_copy(ref_a.at[pl.ds(0, n)], ref_b.at[pl.ds(0, n)], sem).wait()


def _dispatch_kernel(fill_start_ref, fill_n_ref, dest_ref, hn_hbm, zero_hbm, xs_hbm, sem, fill_sem):
    i = pl.program_id(0)
    n = pl.num_programs(0)
    slot = i % 2
    base = i * ROW_TILE
    for r0 in range(0, ROW_TILE, INDEX_BATCH):
        dests = [dest_ref[0, j] for j in range(TOP_K * r0, TOP_K * (r0 + INDEX_BATCH))]
        for r in range(r0, r0 + INDEX_BATCH):
            src = _row_tile(hn_hbm, (base + r) * ROW_SUBLANES)
            for k in range(TOP_K):
                dst = _row_tile(xs_hbm, dests[TOP_K * (r - r0) + k])
                pltpu.make_async_copy(src, dst, sem.at[slot]).start()

    @pl.when(i > 0)
    def _():
        _tile_wait(xs_hbm, xs_hbm, sem.at[1 - slot])

    @pl.when(i == n - 1)
    def _():
        _tile_wait(xs_hbm, xs_hbm, sem.at[slot])

        def per_range(e, carry):
            first = fill_start_ref[e]
            count = fill_n_ref[e]

            def start(j, c):
                pltpu.make_async_copy(zero_hbm, _row_tile(xs_hbm, (first + j) * ROW_SUBLANES),
                                      fill_sem.at[0]).start()
                return c

            def wait(j, c):
                pltpu.make_async_copy(zero_hbm, _row_tile(xs_hbm, 0), fill_sem.at[0]).wait()
                return c

            lax.fori_loop(0, count, start, 0)
            lax.fori_loop(0, count, wait, 0)
            return carry

        lax.fori_loop(0, N_EXPERTS + 1, per_range, 0)


def _dispatch(dest8, fill_start, fill_n, hn, n_slots):
    n_tiles = dest8.shape[0]
    grid_spec = pltpu.PrefetchScalarGridSpec(
        num_scalar_prefetch=2,
        grid=(n_tiles,),
        in_specs=[pl.BlockSpec((None, 1, TILE_ROW_COPIES), lambda i, fs, fn: (i, 0, 0), memory_space=pltpu.SMEM),
                  pl.BlockSpec(memory_space=pl.ANY),
                  pl.BlockSpec(memory_space=pl.ANY)],
        out_specs=pl.BlockSpec(memory_space=pl.ANY),
        scratch_shapes=[pltpu.SemaphoreType.DMA((2,)), pltpu.SemaphoreType.DMA((1,))],
    )
    return pl.pallas_call(
        _dispatch_kernel,
        grid_spec=grid_spec,
        out_shape=jax.ShapeDtypeStruct((n_slots * ROW_SUBLANES, LANES), F32),
        name="dispatch_rows",
        compiler_params=_params("arbitrary", disable_bounds_checks=True),
    )(fill_start, fill_n, dest8, hn, jnp.zeros((ROW_SUBLANES, LANES), F32))


def _expert_kernel(bexp_ref, bval_ref, xs_ref, wg_ref, wu_ref, wd_ref, y_ref, wg_b, wu_b, wd_b):
    i = pl.program_id(0)

    @pl.when((i == 0) | (bexp_ref[i] != bexp_ref[jnp.maximum(i - 1, 0)]))
    def _():
        wg_b[...] = wg_ref[...].astype(BF16)
        wu_b[...] = wu_ref[...].astype(BF16)
        wd_b[...] = wd_ref[...].astype(BF16)

    @pl.when(bval_ref[i] > 0)
    def _():
        xb = _load_rows(xs_ref, MOE_BLOCK).astype(BF16)
        gate = jnp.dot(xb, wg_b[...], preferred_element_type=F32)
        up = jnp.dot(xb, wu_b[...], preferred_element_type=F32)
        hid = (gate * jax.nn.sigmoid(gate) * up).astype(BF16)
        _store_rows(y_ref, jnp.dot(hid, wd_b[...], preferred_element_type=F32), MOE_BLOCK)

    @pl.when(bval_ref[i] == 0)
    def _():
        y_ref[...] = jnp.zeros((MOE_BLOCK * ROW_SUBLANES, LANES), F32)


def _routed_experts(xs, block_expert, block_valid, wg, wu, wd):
    n_blocks = block_expert.shape[0]
    blk = pl.BlockSpec((MOE_BLOCK * ROW_SUBLANES, LANES), lambda i, be, bv: (i, 0))
    grid_spec = pltpu.PrefetchScalarGridSpec(
        num_scalar_prefetch=2,
        grid=(n_blocks,),
        in_specs=[blk,
                  pl.BlockSpec((None, D_MODEL, D_EXPERT), lambda i, be, bv: (be[i], 0, 0)),
                  pl.BlockSpec((None, D_MODEL, D_EXPERT), lambda i, be, bv: (be[i], 0, 0)),
                  pl.BlockSpec((None, D_EXPERT, D_MODEL), lambda i, be, bv: (be[i], 0, 0))],
        out_specs=blk,
        scratch_shapes=[pltpu.VMEM((D_MODEL, D_EXPERT), BF16), pltpu.VMEM((D_MODEL, D_EXPERT), BF16),
                        pltpu.VMEM((D_EXPERT, D_MODEL), BF16)],
    )
    return pl.pallas_call(
        _expert_kernel,
        grid_spec=grid_spec,
        out_shape=jax.ShapeDtypeStruct(xs.shape, F32),
        name="routed_experts",
        compiler_params=_params("arbitrary"),
    )(block_expert, block_valid, xs, wg, wu, wd)


def _combine_kernel(dest_ref, dest_next_ref, x1_ref, route_ref, g_ref, yb_hbm, o_ref, ybuf, sem):
    i = pl.program_id(0)
    n = pl.num_programs(0)
    slot = i % 2

    def row_copy(idx8, r, k, s):
        dst = ybuf.at[s, pl.ds((k * ROW_TILE + r) * ROW_SUBLANES, ROW_SUBLANES)]
        return pltpu.make_async_copy(_row_tile(yb_hbm, idx8), dst, sem.at[s])

    @pl.when(i == 0)
    def _():
        def body(r, c):
            idx = [dest_ref[0, TOP_K * r + k] for k in range(TOP_K)]
            for k in range(TOP_K):
                row_copy(idx[k], r, k, 0).start()
            return c
        lax.fori_loop(0, ROW_TILE, body, 0)

    @pl.when(i + 1 < n)
    def _():
        for r0 in range(0, ROW_TILE, INDEX_BATCH):
            idx = [dest_next_ref[0, j] for j in range(TOP_K * r0, TOP_K * (r0 + INDEX_BATCH))]
            for r in range(r0, r0 + INDEX_BATCH):
                for k in range(TOP_K):
                    row_copy(idx[TOP_K * (r - r0) + k], r, k, 1 - slot).start()

    _tile_wait(yb_hbm, ybuf.at[slot], sem.at[slot])
    route = route_ref[...]
    buf = ybuf.at[slot]
    y = (_load_rows(buf, ROW_TILE) * route[:, 2:3]
         + _load_rows(buf, ROW_TILE, offset=ROW_TILE * ROW_SUBLANES) * route[:, 3:4])
    x2 = x1_ref[...] + y
    ms = jnp.mean(x2 * x2, axis=-1, keepdims=True)
    o_ref[...] = x2 * lax.rsqrt(ms + EPS) * g_ref[...]


def _combine_and_norm(dest8, x1, route, norm_w, yb):
    t = x1.shape[0]
    n_tiles = t // ROW_TILE
    dest_spec = lambda f: pl.BlockSpec((None, 1, TILE_ROW_COPIES), lambda i: (f(i), 0, 0), memory_space=pltpu.SMEM)
    return pl.pallas_call(
        _combine_kernel,
        grid=(n_tiles,),
        in_specs=[dest_spec(lambda i: i), dest_spec(lambda i: jnp.minimum(i + 1, n_tiles - 1)),
                  pl.BlockSpec((ROW_TILE, D_MODEL), lambda i: (i, 0)),
                  pl.BlockSpec((ROW_TILE, LANES), lambda i: (i, 0)),
                  _const_spec((1, D_MODEL)),
                  pl.BlockSpec(memory_space=pl.ANY)],
        out_specs=pl.BlockSpec((ROW_TILE, D_MODEL), lambda i: (i, 0)),
        out_shape=jax.ShapeDtypeStruct((t, D_MODEL), F32),
        scratch_shapes=[pltpu.VMEM((2, TILE_ROW_COPIES * ROW_SUBLANES, LANES), F32),
                        pltpu.SemaphoreType.DMA((2,))],
        name="combine_and_norm",
        compiler_params=_params("arbitrary", disable_bounds_checks=True),
    )(dest8, dest8, x1, route, norm_w, yb)


def _dispatch_plan(route, counts):
    t = route.shape[0]
    counts = counts[0, 0:N_EXPERTS].astype(jnp.int32)
    padded = (counts + MOE_BLOCK - 1) // MOE_BLOCK * MOE_BLOCK
    pad_end = jnp.cumsum(padded)
    pad_start = pad_end - padded
    expert = route[:, 0:TOP_K].astype(jnp.int32)
    rank = route[:, 4:4 + TOP_K].astype(jnp.int32)
    ids = jnp.arange(N_EXPERTS, dtype=jnp.int32)
    start_of = jnp.sum(jnp.where(expert[:, :, None] == ids, pad_start, 0), axis=-1)
    dest8 = ((start_of + rank) * ROW_SUBLANES).reshape(t // ROW_TILE, 1, TILE_ROW_COPIES)
    n_blocks = -(-(t * TOP_K + N_EXPERTS * (MOE_BLOCK - 1)) // MOE_BLOCK)
    block_start = jnp.arange(n_blocks, dtype=jnp.int32) * MOE_BLOCK
    block_expert = jnp.minimum(jnp.sum((pad_end[None, :] <= block_start[:, None]).astype(jnp.int32), axis=1),
                               N_EXPERTS - 1)
    block_valid = (block_start < pad_end[-1]).astype(jnp.int32)
    n_slots = n_blocks * MOE_BLOCK
    fill_start = jnp.concatenate([pad_start + counts, pad_end[-1:]])
    fill_n = jnp.concatenate([padded - counts, n_slots - pad_end[-1:]])
    return dest8, fill_start, fill_n, block_expert, block_valid, n_slots


def _rel_bias_band(rel_table, qc):
    rows = qc * CHUNK
    i = jnp.arange(rows)[:, None]
    j = jnp.arange(ATTN_REACH + rows)[None, :]
    bias = rel_table[:, jnp.clip(i + ATTN_REACH - j, -MAX_REL, MAX_REL) + MAX_REL].astype(F32) * LOG2E
    first = (i // CHUNK) * CHUNK
    return jnp.where((j >= first) & (j < first + BAND), bias, NEG_INF)


def _pad_lanes(v, width=LANES):
    return jnp.pad(v.astype(F32), (0, width - v.shape[0])).reshape(1, width)


def _trunk(x, kv_cache, conv_state, ssm_state, w):
    batch, l, _ = x.shape
    nc = l // CHUNK
    t = batch * l
    x2d = x.reshape(t, D_MODEL)
    proj = _in_projection(x2d, w["norm_mix"], w["w_proj"])
    proj3 = proj.reshape(batch, l, PROJ_WIDTH)
    if kv_cache is None:
        qc = ATTN_CHUNKS_PER_STEP if nc % ATTN_CHUNKS_PER_STEP == 0 else 1
        kv_spec = [pl.BlockSpec((l, ATTN_WIDTH), lambda b, c: (b, K0 // ATTN_WIDTH)),
                   pl.BlockSpec((l, ATTN_WIDTH), lambda b, c: (b, V0 // ATTN_WIDTH))]
        attn = _band_attention(proj, Q0 // ATTN_WIDTH, proj, proj, kv_spec, _rel_bias_band(w["rel_table"], qc),
                               batch=batch, nc=nc, lk=l, pad_rows=ATTN_REACH, qc=qc)
        keep = min(ATTN_REACH, l)
        k_keep = proj3[:, l - keep:, K0:K0 + ATTN_WIDTH]
        v_keep = proj3[:, l - keep:, V0:V0 + ATTN_WIDTH]
    else:
        assert nc == 1
        cache_k, cache_v = kv_cache
        wlen = cache_k.shape[1]
        kk = jnp.concatenate([cache_k.reshape(batch, wlen, ATTN_WIDTH), proj3[:, :, K0:K0 + ATTN_WIDTH]], axis=1)
        vv = jnp.concatenate([cache_v.reshape(batch, wlen, ATTN_WIDTH), proj3[:, :, V0:V0 + ATTN_WIDTH]], axis=1)
        spec = pl.BlockSpec((None, wlen + l, ATTN_WIDTH), lambda b, c: (b, 0, 0))
        attn = _band_attention(proj, Q0 // ATTN_WIDTH, kk, vv, [spec, spec], _rel_bias_band(w["rel_table"], 1),
                               batch=batch, nc=nc, lk=wlen + l, pad_rows=BAND - (wlen + l), qc=1)
        k_keep, v_keep = kk[:, l:], vv[:, l:]
    tail = proj3[:, l - (CONV_WIDTH - 1):, XBC0:XBC0 + CONV_DIM]
    if conv_state is None:
        conv0 = st0 = None
        conv_new = tail
    else:
        conv0 = jnp.pad(conv_state, ((0, 0), (8 - (CONV_WIDTH - 1), 0), (0, 0)))
        st0 = jnp.transpose(ssm_state.reshape(batch, D_INNER, D_STATE), (0, 2, 1))
        conv_new = jnp.concatenate([conv_state, tail], axis=1)[:, -(CONV_WIDTH - 1):]
    ssm, st_t = _ssd_mixer(proj, conv0, st0, w["conv_w"], w["conv_b"], w["dt_bias"], w["a_log"], w["d_exp"],
                           w["ssm_norm"], batch=batch, nc=nc)
    ssm_new = jnp.transpose(st_t, (0, 2, 1)).reshape(batch, SSM_HEADS, SSM_HEAD_DIM, D_STATE)
    x1, hn, route, counts = _mix_and_route(x2d, attn, ssm, proj, w["wa"], w["wb"], w["wo"], w["norm_ffn"],
                                           w["w_router"], w["b_router"])
    dest8, fill_start, fill_n, block_expert, block_valid, n_slots = _dispatch_plan(route, counts)
    xs = _dispatch(dest8, fill_start, fill_n, hn, n_slots)
    yb = _routed_experts(xs, block_expert, block_valid, w["w_gate"], w["w_up"], w["w_down"])
    y = _combine_and_norm(dest8, x1, route, w["norm_final"], yb)
    heads = lambda u: u.reshape(batch, u.shape[1], N_HEADS_A, HEAD_DIM_A)
    return y.reshape(batch, l, D_MODEL), heads(k_keep), heads(v_keep), conv_new, ssm_new


def kernel(x_prompt, x_sample, cache_attn_k, cache_attn_v, state_conv, state_ssm, norm_mix, w_in, rel_bias, conv_w, conv_b, dt_bias, a_log, d_skip, ssm_norm, w_branch_a, w_branch_b, w_out, norm_ffn, w_router_group, b_router_group, w_router_expert, b_router_expert, w_gate, w_up, w_down, norm_final):
    assert norm_mix.shape[0] == 1, "single-layer trunk"
    q, k, v, z, xbc, dtc, ga, gb = jnp.split(w_in[0], np.cumsum(
        (ATTN_WIDTH, ATTN_WIDTH, ATTN_WIDTH, D_INNER, CONV_DIM, SSM_HEADS, D_MODEL, D_MODEL))[:-1].tolist(), axis=1)
    w_proj = jnp.concatenate([z, ga, gb, xbc, q, k, v, jnp.pad(dtc, ((0, 0), (0, LANES - SSM_HEADS)))],
                             axis=1).astype(BF16)
    w_router = jnp.pad(jnp.concatenate([w_router_group[0], w_router_expert[0]], axis=1),
                       ((0, 0), (0, LANES - N_EXPERT_GROUPS - N_EXPERTS)))
    w_router_hi = w_router.astype(BF16)
    w_router_lo = (w_router - w_router_hi.astype(F32)).astype(BF16)
    b_router = _pad_lanes(jnp.concatenate([b_router_group[0], b_router_expert[0]]))
    w = dict(
        norm_mix=norm_mix[0].reshape(1, D_MODEL), w_proj=w_proj, rel_table=rel_bias[0],
        conv_w=conv_w[0], conv_b=conv_b[0].reshape(1, CONV_DIM), dt_bias=_pad_lanes(dt_bias[0]),
        a_log=_pad_lanes(a_log[0]), d_exp=jnp.repeat(d_skip[0], SSM_HEAD_DIM).reshape(1, D_INNER),
        ssm_norm=ssm_norm[0].reshape(1, D_INNER),
        wa=w_branch_a[0].astype(BF16), wb=w_branch_b[0].astype(BF16), wo=w_out[0].astype(BF16),
        norm_ffn=norm_ffn[0].reshape(1, D_MODEL),
        w_router=jnp.concatenate([w_router_hi, w_router_lo], axis=1), b_router=b_router,
        w_gate=w_gate[0], w_up=w_up[0], w_down=w_down[0],
        norm_final=norm_final.reshape(1, D_MODEL),
    )
    yp, kp, vp, cp, sp = _trunk(x_prompt, None, None, None, w)
    ys, ks, vs, cs, ss = _trunk(x_sample, (cache_attn_k[0], cache_attn_v[0]), state_conv[0], state_ssm[0], w)
    return (yp, ys, kp[None], vp[None], cp[None], sp[None], ks[None], vs[None], cs[None], ss[None])
```

```python
import functools
import math

import numpy as np
import jax
import jax.numpy as jnp
from jax import lax
from jax.experimental import pallas as pl
from jax.experimental.pallas import tpu as pltpu

F32 = jnp.float32
BF16 = jnp.bfloat16

D_MODEL = 1024
CHUNK = 64
LEFT_CHUNKS = 8
ATTN_REACH = LEFT_CHUNKS * CHUNK
BAND = ATTN_REACH + CHUNK
N_HEADS_A = 8
HEAD_DIM_A = 64
ATTN_WIDTH = N_HEADS_A * HEAD_DIM_A
MAX_REL = 256
ATTN_SCALE = 1.0 / math.sqrt(HEAD_DIM_A)
NEG_INF = -1e30
SSM_HEADS = 16
SSM_HEAD_DIM = 64
D_INNER = SSM_HEADS * SSM_HEAD_DIM
SSM_GROUPS = 2
GROUP_WIDTH = D_INNER // SSM_GROUPS
D_STATE = 128
CONV_WIDTH = 4
CONV_DIM = D_INNER + 2 * SSM_GROUPS * D_STATE
N_EXPERT_GROUPS = 4
EXPERTS_PER_GROUP = 8
N_EXPERTS = N_EXPERT_GROUPS * EXPERTS_PER_GROUP
TOP_K = 2
D_EXPERT = 512
MOE_BLOCK = 128
EPS = 1e-6

LANES = 128
SUBLANES = 8
Z0, GA0, GB0, XBC0, Q0, K0, V0, DT0 = 0, 1024, 2048, 3072, 4608, 5120, 5632, 6144
PROJ_WIDTH = DT0 + LANES
PROJ_TILE = 896
ROW_TILE = 256
ATTN_CHUNKS_PER_STEP = 2
SSD_STREAMS = 2
VMEM_LIMIT = 56 * 1024 * 1024

_NT = (((1,), (1,)), ((), ()))


def _params(*sem, **kw):
    return pltpu.CompilerParams(dimension_semantics=sem, vmem_limit_bytes=VMEM_LIMIT, **kw)


def _const_spec(shape):
    nd = len(shape)
    return pl.BlockSpec(shape, lambda *_: (0,) * nd)


def _inproj_kernel(x_ref, g_ref, w_ref, o_ref):
    x = x_ref[...]
    ms = jnp.mean(x * x, axis=-1, keepdims=True)
    h = (x * lax.rsqrt(ms + EPS) * g_ref[...]).astype(BF16)
    for j in range(0, PROJ_WIDTH, PROJ_TILE):
        o_ref[:, j:j + PROJ_TILE] = jnp.dot(h, w_ref[:, j:j + PROJ_TILE], preferred_element_type=F32)


def _in_projection(x2d, norm_w, w_proj):
    t = x2d.shape[0]
    return pl.pallas_call(
        _inproj_kernel,
        grid=(t // ROW_TILE,),
        in_specs=[pl.BlockSpec((ROW_TILE, D_MODEL), lambda i: (i, 0)),
                  _const_spec((1, D_MODEL)),
                  _const_spec((D_MODEL, PROJ_WIDTH))],
        out_specs=pl.BlockSpec((ROW_TILE, PROJ_WIDTH), lambda i: (i, 0)),
        out_shape=jax.ShapeDtypeStruct((t, PROJ_WIDTH), F32),
        name="in_projection",
        compiler_params=_params("parallel"),
    )(x2d, norm_w, w_proj)


LOG2E = math.log2(math.e)


def _attn_kernel(q_ref, k_ref, v_ref, bias_ref, o_ref, kpad, vpad, *, pad_rows, lk, qc):
    rows = qc * CHUNK
    window = ATTN_REACH + rows
    c = pl.program_id(1)

    @pl.when(c == 0)
    def _():
        if pad_rows:
            kpad[0:pad_rows, :] = jnp.zeros((pad_rows, ATTN_WIDTH), BF16)
            vpad[0:pad_rows, :] = jnp.zeros((pad_rows, ATTN_WIDTH), BF16)
        kpad[pad_rows:pad_rows + lk, :] = k_ref[...].astype(BF16)
        vpad[pad_rows:pad_rows + lk, :] = v_ref[...].astype(BF16)

    start = pl.multiple_of(c * rows, rows)

    def body(mask_start):
        kw = kpad[pl.ds(start, window), :]
        vw = vpad[pl.ds(start, window), :]
        q = (q_ref[...] * (ATTN_SCALE * LOG2E)).astype(BF16)
        low = lax.broadcasted_iota(jnp.int32, (rows, LANES), 1) < HEAD_DIM_A
        if mask_start:
            valid = lax.broadcasted_iota(jnp.int32, (rows, window), 1) + start >= pad_rows
        for hp in range(N_HEADS_A // 2):
            sl = slice(LANES * hp, LANES * (hp + 1))
            q2, k2, v2 = q[:, sl], kw[:, sl], vw[:, sl]
            outs = []
            for sub in range(2):
                qm = jnp.where(low if sub == 0 else jnp.logical_not(low), q2, jnp.zeros_like(q2))
                s = lax.dot_general(qm, k2, _NT, preferred_element_type=F32) + bias_ref[2 * hp + sub]
                if mask_start:
                    s = jnp.where(valid, s, NEG_INF)
                m = jnp.max(s, axis=-1, keepdims=True)
                p = jnp.exp2(s - m)
                l = jnp.sum(p, axis=-1, keepdims=True)
                o = jnp.dot(p.astype(BF16), v2, preferred_element_type=F32)
                outs.append(o / l)
            o_ref[:, sl] = jnp.where(low, outs[0], outs[1])

    if pad_rows:
        pl.when(start < pad_rows)(lambda: body(True))
        pl.when(start >= pad_rows)(lambda: body(False))
    else:
        body(False)


def _band_attention(q_arr, q_col, k_arr, v_arr, kv_spec, bias, *, batch, nc, lk, pad_rows, qc):
    steps = nc // qc
    rows = qc * CHUNK
    kern = functools.partial(_attn_kernel, pad_rows=pad_rows, lk=lk, qc=qc)
    return pl.pallas_call(
        kern,
        grid=(batch, steps),
        in_specs=[pl.BlockSpec((rows, ATTN_WIDTH), lambda b, c: (b * steps + c, q_col)),
                  kv_spec[0], kv_spec[1],
                  _const_spec((N_HEADS_A, rows, ATTN_REACH + rows))],
        out_specs=pl.BlockSpec((rows, ATTN_WIDTH), lambda b, c: (b * steps + c, 0)),
        out_shape=jax.ShapeDtypeStruct((batch * nc * CHUNK, ATTN_WIDTH), F32),
        scratch_shapes=[pltpu.VMEM((pad_rows + lk, ATTN_WIDTH), BF16),
                        pltpu.VMEM((pad_rows + lk, ATTN_WIDTH), BF16)],
        name="band_attention",
        compiler_params=_params("parallel", "arbitrary"),
    )(q_arr, k_arr, v_arr, bias)


def _split_bf16(x, parts):
    out = []
    for _ in range(parts - 1):
        h = x.astype(BF16)
        out.append(h)
        x = x - h.astype(F32)
    out.append(x.astype(BF16))
    return out


def _ssd_kernel(xbc_ref, z_ref, dt_ref, conv0_ref, st0_ref, cw_ref, cb_ref, dtb_ref, alog_ref,
                dexp_ref, nw_ref, tri_ref, exp_ref, sel_ref, eye_ref,
                y_ref, stout_ref, xwin, st, *, has_init, streams):
    c = pl.program_id(1)

    @pl.when(c == 0)
    def _():
        if has_init:
            xwin[:, 0:8, :] = conv0_ref[...]
            st[...] = st0_ref[...]
        else:
            xwin[:, 0:8, :] = jnp.zeros((streams, 8, CONV_DIM), F32)
            st[...] = jnp.zeros((streams, D_STATE, D_INNER), F32)

    for s in range(streams):
        _ssd_chunk(xbc_ref.at[s], z_ref.at[s], dt_ref.at[s], cw_ref, cb_ref, dtb_ref, alog_ref, dexp_ref, nw_ref,
                   tri_ref, exp_ref, sel_ref, eye_ref, y_ref.at[s], xwin.at[s], st.at[s])

    @pl.when(c == pl.num_programs(1) - 1)
    def _():
        stout_ref[...] = st[...]


def _ssd_chunk(xbc_ref, z_ref, dt_ref, cw_ref, cb_ref, dtb_ref, alog_ref, dexp_ref, nw_ref,
               tri_ref, exp_ref, sel_ref, eye_ref, y_ref, xwin, st):
    xwin[8:8 + CHUNK, :] = xbc_ref[...]
    conv = cb_ref[...]
    for j in range(CONV_WIDTH):
        conv = conv + xwin[5 + j:5 + j + CHUNK, :] * cw_ref[j:j + 1, :]
    xwin[5:8, :] = xwin[5 + CHUNK:8 + CHUNK, :]
    u = conv * jax.nn.sigmoid(conv)
    xs = u[:, 0:D_INNER]

    dt_in = dt_ref[...] + dtb_ref[...]
    dt = jnp.maximum(dt_in, 0.0) + jnp.log1p(jnp.exp(-jnp.abs(dt_in)))
    da = dt * (-jnp.exp(alog_ref[...]))
    tri = tri_ref[...]
    cs3 = jnp.dot(tri, jnp.concatenate(_split_bf16(da, 3), axis=1), preferred_element_type=F32)
    cs = cs3[:, 0:LANES] + cs3[:, LANES:2 * LANES] + cs3[:, 2 * LANES:3 * LANES]
    cs_last = cs[CHUNK - 1:CHUNK, :]
    ecs = jnp.exp(cs)
    to_end = jnp.exp(cs_last - cs)

    stack = jnp.concatenate(_split_bf16(dt, 2) + _split_bf16(ecs, 2) + _split_bf16(to_end, 2)
                            + _split_bf16(cs, 3), axis=0)
    wide = jnp.dot(stack, exp_ref[...], preferred_element_type=F32)
    rows = [wide[i * CHUNK:(i + 1) * CHUNK, :] for i in range(9)]
    dt_w = rows[0] + rows[1]
    ecs_w = rows[2] + rows[3]
    to_end_w = rows[4] + rows[5]
    cs_w = rows[6] + rows[7] + rows[8]

    lane = lax.broadcasted_iota(jnp.int32, (CHUNK, LANES), 1)
    even = (lane % 2) == 0
    csm = jnp.concatenate([jnp.where(even, cs, 0.0), jnp.where(even, 0.0, cs)], axis=0)
    srow3 = lax.dot_general(sel_ref[...], jnp.concatenate(_split_bf16(csm, 3), axis=0), _NT,
                            preferred_element_type=F32)
    srow = srow3[:, 0:LANES] + srow3[:, LANES:2 * LANES] + srow3[:, 2 * LANES:3 * LANES]

    xdt = xs * dt_w
    xdt_b = xdt.astype(BF16)
    xw_b = (xdt * to_end_w).astype(BF16)
    low = lane < SSM_HEAD_DIM
    t_idx = lax.broadcasted_iota(jnp.int32, (CHUNK, LANES), 0)
    causal = t_idx >= (lane % SSM_HEAD_DIM)
    eye = eye_ref[...]

    y_parts = []
    for g in range(SSM_GROUPS):
        b_g = u[:, D_INNER + g * D_STATE:D_INNER + (g + 1) * D_STATE].astype(BF16)
        c_g = u[:, D_INNER + (SSM_GROUPS + g) * D_STATE:D_INNER + (SSM_GROUPS + g + 1) * D_STATE].astype(BF16)
        gsl = slice(g * GROUP_WIDTH, (g + 1) * GROUP_WIDTH)
        st_g = st[:, gsl]
        y_off = jnp.dot(c_g, st_g.astype(BF16), preferred_element_type=F32) * ecs_w[:, gsl]
        b_t = lax.dot_general(eye, b_g, _NT, preferred_element_type=F32).astype(BF16)
        st[:, gsl] = st_g * ecs_w[CHUNK - 1:CHUNK, gsl] + jnp.dot(b_t, xw_b[:, gsl], preferred_element_type=F32)
        cb2 = lax.dot_general(c_g, jnp.concatenate([b_g, b_g], axis=0), _NT, preferred_element_type=F32)
        for kk in range(SSM_HEADS // SSM_GROUPS // 2):
            k = g * (SSM_HEADS // SSM_GROUPS // 2) + kk
            psl = slice(k * LANES, (k + 1) * LANES)
            decay = jnp.where(causal, jnp.exp(cs_w[:, psl] - srow[k:k + 1, :]), 0.0)
            gmat = (cb2 * decay).astype(BF16)
            xp = xdt_b[:, psl]
            xblk = jnp.concatenate([jnp.where(low, xp, jnp.zeros_like(xp)),
                                    jnp.where(low, jnp.zeros_like(xp), xp)], axis=0)
            y_parts.append(jnp.dot(gmat, xblk, preferred_element_type=F32)
                           + y_off[:, kk * LANES:(kk + 1) * LANES])
    y = jnp.concatenate(y_parts, axis=1) + xs * dexp_ref[...]
    zz = z_ref[...]
    y = y * (zz * jax.nn.sigmoid(zz))
    outs = []
    for g in range(SSM_GROUPS):
        yg = y[:, g * GROUP_WIDTH:(g + 1) * GROUP_WIDTH]
        outs.append(yg * lax.rsqrt(jnp.mean(yg * yg, axis=-1, keepdims=True) + EPS))
    y_ref[...] = jnp.concatenate(outs, axis=1) * nw_ref[...]


def _ssd_constants():
    tri = np.tril(np.ones((CHUNK, CHUNK), np.float32))
    expand = np.zeros((LANES, D_INNER), np.float32)
    for e in range(SSM_HEADS):
        expand[e, e * SSM_HEAD_DIM:(e + 1) * SSM_HEAD_DIM] = 1.0
    sel = np.zeros((16, LANES), np.float32)
    for e in range(SSM_HEADS):
        sel[e // 2, e] = 1.0
    eye = np.eye(D_STATE, dtype=np.float32)
    return (jnp.asarray(tri, BF16), jnp.asarray(expand, BF16), jnp.asarray(sel, BF16), jnp.asarray(eye, BF16))


def _ssd_mixer(proj, conv0, st0, conv_w, conv_b, dt_bias_p, a_log_p, d_exp, norm_w, *, batch, nc):
    streams = SSD_STREAMS if batch % SSD_STREAMS == 0 else 1
    groups = batch // streams
    l = nc * CHUNK
    has_init = conv0 is not None
    if not has_init:
        conv0 = jnp.zeros((1, streams, 8, CONV_DIM), F32)
        st0 = jnp.zeros((1, streams, D_STATE, D_INNER), F32)
        init_map = lambda b, c: (0, 0, 0, 0)
    else:
        conv0 = conv0.reshape(groups, streams, 8, CONV_DIM)
        st0 = st0.reshape(groups, streams, D_STATE, D_INNER)
        init_map = lambda b, c: (b, 0, 0, 0)
    tri, expand, sel, eye = _ssd_constants()
    proj4 = proj.reshape(groups, streams, l, PROJ_WIDTH)
    chunk_of = lambda w, col: pl.BlockSpec((None, streams, CHUNK, w), lambda b, c: (b, 0, c, col))
    state_spec = lambda m: pl.BlockSpec((None, streams, D_STATE, D_INNER), m)
    kern = functools.partial(_ssd_kernel, has_init=has_init, streams=streams)
    y, st_out = pl.pallas_call(
        kern,
        grid=(groups, nc),
        in_specs=[chunk_of(CONV_DIM, XBC0 // CONV_DIM), chunk_of(D_INNER, Z0 // D_INNER), chunk_of(LANES, DT0 // LANES),
                  pl.BlockSpec((None, streams, 8, CONV_DIM), init_map), state_spec(init_map),
                  _const_spec((CONV_WIDTH, CONV_DIM)), _const_spec((1, CONV_DIM)),
                  _const_spec((1, LANES)), _const_spec((1, LANES)),
                  _const_spec((1, D_INNER)), _const_spec((1, D_INNER)),
                  _const_spec((CHUNK, CHUNK)), _const_spec((LANES, D_INNER)),
                  _const_spec((16, LANES)), _const_spec((D_STATE, D_STATE))],
        out_specs=[chunk_of(D_INNER, 0), state_spec(lambda b, c: (b, 0, 0, 0))],
        out_shape=[jax.ShapeDtypeStruct((groups, streams, l, D_INNER), F32),
                   jax.ShapeDtypeStruct((groups, streams, D_STATE, D_INNER), F32)],
        scratch_shapes=[pltpu.VMEM((streams, 8 + CHUNK, CONV_DIM), F32),
                        pltpu.VMEM((streams, D_STATE, D_INNER), F32)],
        name="ssd_mixer",
        compiler_params=_params("parallel", "arbitrary"),
    )(proj4, proj4, proj4, conv0, st0, conv_w, conv_b, dt_bias_p, a_log_p, d_exp, norm_w, tri, expand, sel, eye)
    return y.reshape(batch * l, D_INNER), st_out.reshape(batch, D_STATE, D_INNER)


ROW_SUBLANES = D_MODEL // LANES
assert ROW_SUBLANES == SUBLANES


def _store_rows(ref, x, rows, offset=0):
    for s in range(ROW_SUBLANES):
        ref[pl.ds(offset + s, rows, stride=ROW_SUBLANES), :] = x[:, s * LANES:(s + 1) * LANES]


def _load_rows(ref, rows, offset=0):
    return jnp.concatenate([ref[pl.ds(offset + s, rows, stride=ROW_SUBLANES), :] for s in range(ROW_SUBLANES)],
                           axis=1)


def _row_tile(ref, idx8):
    return ref.at[pl.ds(pl.multiple_of(idx8, ROW_SUBLANES), ROW_SUBLANES)]


def _mix_kernel(x_ref, attn_ref, ssm_ref, ga_ref, gb_ref, wa_ref, wb_ref, wo_ref, g_ref, wr_ref, br_ref, slt_ref,
                x1_ref, hn_ref, route_ref, counts_ref, carry):
    @pl.when(pl.program_id(0) == 0)
    def _():
        carry[...] = jnp.zeros((1, LANES), F32)

    a = jnp.dot(attn_ref[...].astype(BF16), wa_ref[...], preferred_element_type=F32)
    s = jnp.dot(ssm_ref[...].astype(BF16), wb_ref[...], preferred_element_type=F32)
    mixed = jax.nn.sigmoid(ga_ref[...]) * a + jax.nn.sigmoid(gb_ref[...]) * s
    x1 = x_ref[...] + jnp.dot(mixed.astype(BF16), wo_ref[...], preferred_element_type=F32)
    x1_ref[...] = x1
    ms = jnp.mean(x1 * x1, axis=-1, keepdims=True)
    hn = x1 * lax.rsqrt(ms + EPS) * g_ref[...]
    _store_rows(hn_ref, hn, ROW_TILE)

    hi = hn.astype(BF16)
    lo = (hn - hi.astype(F32)).astype(BF16)
    both = jnp.dot(hi, wr_ref[...], preferred_element_type=F32)
    logits = (both[:, 0:LANES] + both[:, LANES:2 * LANES]
              + jnp.dot(lo, wr_ref[:, 0:LANES], preferred_element_type=F32) + br_ref[...])
    lane = lax.broadcasted_iota(jnp.int32, logits.shape, 1)
    lane_f = lane.astype(F32)
    big = float(LANES)
    is_g = lane < N_EXPERT_GROUPS
    gl = jnp.where(is_g, logits, -jnp.inf)
    gmax = jnp.max(gl, axis=-1, keepdims=True)
    gidx = jnp.min(jnp.where(gl == gmax, lane_f, big), axis=-1, keepdims=True)
    gsum = jnp.sum(jnp.where(is_g, jnp.exp(gl - gmax), 0.0), axis=-1, keepdims=True)
    g_top = 1.0 / gsum
    first = N_EXPERT_GROUPS + gidx * EXPERTS_PER_GROUP
    in_group = (lane_f >= first) & (lane_f < first + EXPERTS_PER_GROUP)
    el = jnp.where(in_group, logits, -jnp.inf)
    m1 = jnp.max(el, axis=-1, keepdims=True)
    i1 = jnp.min(jnp.where(el == m1, lane_f, big), axis=-1, keepdims=True)
    el2 = jnp.where(lane_f == i1, -jnp.inf, el)
    m2 = jnp.max(el2, axis=-1, keepdims=True)
    i2 = jnp.min(jnp.where(el2 == m2, lane_f, big), axis=-1, keepdims=True)
    r = jnp.exp(m2 - m1)
    w1 = g_top / (1.0 + r)
    w2 = g_top * r / (1.0 + r)
    e1 = i1 - N_EXPERT_GROUPS
    e2 = i2 - N_EXPERT_GROUPS

    oh1 = lane_f == e1
    oh2 = lane_f == e2
    hot1 = jnp.where(oh1, 1.0, 0.0)
    hot2 = jnp.where(oh2, 1.0, 0.0)
    onehots = jnp.concatenate([hot1, hot2], axis=1).astype(BF16)
    before = jnp.dot(slt_ref[...], onehots, preferred_element_type=F32)
    cnt1 = jnp.sum(hot1, axis=0, keepdims=True)
    cnt2 = jnp.sum(hot2, axis=0, keepdims=True)
    base = carry[...]
    rank1 = jnp.sum(jnp.where(oh1, before[:, 0:LANES] + base, 0.0), axis=-1, keepdims=True)
    rank2 = jnp.sum(jnp.where(oh2, before[:, LANES:2 * LANES] + (base + cnt1), 0.0), axis=-1, keepdims=True)
    carry[...] = base + cnt1 + cnt2
    counts_ref[...] = carry[...]

    route = jnp.zeros_like(logits)
    for col, val in enumerate((e1, e2, w1, w2, rank1, rank2)):
        route = jnp.where(lane == col, val, route)
    route_ref[...] = route


def _mix_and_route(x2d, attn, ssm, proj, wa, wb, wo, norm_w, w_router, b_router):
    t = x2d.shape[0]
    tile = lambda w, col: pl.BlockSpec((ROW_TILE, w), lambda i: (i, col))
    slt = jnp.asarray(np.tril(np.ones((ROW_TILE, ROW_TILE), np.float32), k=-1), BF16)
    return pl.pallas_call(
        _mix_kernel,
        grid=(t // ROW_TILE,),
        in_specs=[tile(D_MODEL, 0), tile(ATTN_WIDTH, 0), tile(D_INNER, 0),
                  tile(D_MODEL, GA0 // D_MODEL), tile(D_MODEL, GB0 // D_MODEL),
                  _const_spec((ATTN_WIDTH, D_MODEL)), _const_spec((D_INNER, D_MODEL)),
                  _const_spec((D_MODEL, D_MODEL)), _const_spec((1, D_MODEL)),
                  _const_spec((D_MODEL, 2 * LANES)), _const_spec((1, LANES)),
                  _const_spec((ROW_TILE, ROW_TILE))],
        out_specs=[tile(D_MODEL, 0), pl.BlockSpec((ROW_TILE * ROW_SUBLANES, LANES), lambda i: (i, 0)),
                   tile(LANES, 0), _const_spec((1, LANES))],
        out_shape=[jax.ShapeDtypeStruct((t, D_MODEL), F32),
                   jax.ShapeDtypeStruct((t * ROW_SUBLANES, LANES), F32),
                   jax.ShapeDtypeStruct((t, LANES), F32),
                   jax.ShapeDtypeStruct((1, LANES), F32)],
        scratch_shapes=[pltpu.VMEM((1, LANES), F32)],
        name="mix_and_route",
        compiler_params=_params("arbitrary"),
    )(x2d, attn, ssm, proj, proj, wa, wb, wo, norm_w, w_router, b_router, slt)


TILE_ROW_COPIES = ROW_TILE * TOP_K
INDEX_BATCH = 8


def _tile_wait(ref_a, ref_b, sem):
    n = TILE_ROW_COPIES * ROW_SUBLANES
    pltpu.make_async_copy(ref_a.at[pl.ds(0, n)], ref_b.at[pl.ds(0, n)], sem).wait()


DISPATCH_BUFFERS = 3


def _dispatch_kernel(fill_start_ref, fill_n_ref, dest_ref, hn_hbm, xs_hbm, hbuf, zbuf, in_sem, out_sem, fill_sem):
    i = pl.program_id(0)
    n = pl.num_programs(0)
    tile_rows = ROW_TILE * ROW_SUBLANES

    def tile_in(j):
        b = j % DISPATCH_BUFFERS
        src = hn_hbm.at[pl.ds(pl.multiple_of(j * tile_rows, tile_rows), tile_rows)]
        return pltpu.make_async_copy(src, hbuf.at[b], in_sem.at[b])

    @pl.when(i == 0)
    def _():
        zbuf[...] = jnp.zeros((ROW_SUBLANES, LANES), F32)
        tile_in(0).start()

        @pl.when(n > 1)
        def _():
            tile_in(1).start()

    tile_in(i).wait()
    buf = i % DISPATCH_BUFFERS
    for r0 in range(0, ROW_TILE, INDEX_BATCH):
        dests = [dest_ref[0, j] for j in range(TOP_K * r0, TOP_K * (r0 + INDEX_BATCH))]
        for r in range(r0, r0 + INDEX_BATCH):
            src = hbuf.at[buf, pl.ds(r * ROW_SUBLANES, ROW_SUBLANES)]
            for k in range(TOP_K):
                dst = _row_tile(xs_hbm, dests[TOP_K * (r - r0) + k])
                pltpu.make_async_copy(src, dst, out_sem.at[buf]).start()

    @pl.when(i > 0)
    def _():
        _tile_wait(xs_hbm, xs_hbm, out_sem.at[(i - 1) % DISPATCH_BUFFERS])

    @pl.when(i + 2 < n)
    def _():
        tile_in(i + 2).start()

    @pl.when(i == n - 1)
    def _():
        _tile_wait(xs_hbm, xs_hbm, out_sem.at[buf])

        def per_range(e, carry):
            first = fill_start_ref[e]
            count = fill_n_ref[e]

            def start(j, c):
                pltpu.make_async_copy(zbuf, _row_tile(xs_hbm, (first + j) * ROW_SUBLANES), fill_sem.at[0]).start()
                return c

            def wait(j, c):
                pltpu.make_async_copy(zbuf, _row_tile(xs_hbm, 0), fill_sem.at[0]).wait()
                return c

            lax.fori_loop(0, count, start, 0)
            lax.fori_loop(0, count, wait, 0)
            return carry

        lax.fori_loop(0, N_EXPERTS + 1, per_range, 0)


def _dispatch(dest8, fill_start, fill_n, hn, n_slots):
    n_tiles = dest8.shape[0]
    grid_spec = pltpu.PrefetchScalarGridSpec(
        num_scalar_prefetch=2,
        grid=(n_tiles,),
        in_specs=[pl.BlockSpec((None, 1, TILE_ROW_COPIES), lambda i, fs, fn: (i, 0, 0), memory_space=pltpu.SMEM),
                  pl.BlockSpec(memory_space=pl.ANY)],
        out_specs=pl.BlockSpec(memory_space=pl.ANY),
        scratch_shapes=[pltpu.VMEM((DISPATCH_BUFFERS, ROW_TILE * ROW_SUBLANES, LANES), F32),
                        pltpu.VMEM((ROW_SUBLANES, LANES), F32),
                        pltpu.SemaphoreType.DMA((DISPATCH_BUFFERS,)), pltpu.SemaphoreType.DMA((DISPATCH_BUFFERS,)),
                        pltpu.SemaphoreType.DMA((1,))],
    )
    return pl.pallas_call(
        _dispatch_kernel,
        grid_spec=grid_spec,
        out_shape=jax.ShapeDtypeStruct((n_slots * ROW_SUBLANES, LANES), F32),
        name="dispatch_rows",
        compiler_params=_params("arbitrary", disable_bounds_checks=True),
    )(fill_start, fill_n, dest8, hn)


def _expert_kernel(bexp_ref, bval_ref, xs_ref, wg_ref, wu_ref, wd_ref, y_ref, wg_b, wu_b, wd_b):
    i = pl.program_id(0)

    @pl.when((i == 0) | (bexp_ref[i] != bexp_ref[jnp.maximum(i - 1, 0)]))
    def _():
        wg_b[...] = wg_ref[...].astype(BF16)
        wu_b[...] = wu_ref[...].astype(BF16)
        wd_b[...] = wd_ref[...].astype(BF16)

    @pl.when(bval_ref[i] > 0)
    def _():
        xb = _load_rows(xs_ref, MOE_BLOCK).astype(BF16)
        gate = jnp.dot(xb, wg_b[...], preferred_element_type=F32)
        up = jnp.dot(xb, wu_b[...], preferred_element_type=F32)
        hid = (gate * jax.nn.sigmoid(gate) * up).astype(BF16)
        _store_rows(y_ref, jnp.dot(hid, wd_b[...], preferred_element_type=F32), MOE_BLOCK)

    @pl.when(bval_ref[i] == 0)
    def _():
        y_ref[...] = jnp.zeros((MOE_BLOCK * ROW_SUBLANES, LANES), F32)


def _routed_experts(xs, block_expert, block_valid, wg, wu, wd):
    n_blocks = block_expert.shape[0]
    blk = pl.BlockSpec((MOE_BLOCK * ROW_SUBLANES, LANES), lambda i, be, bv: (i, 0))
    grid_spec = pltpu.PrefetchScalarGridSpec(
        num_scalar_prefetch=2,
        grid=(n_blocks,),
        in_specs=[blk,
                  pl.BlockSpec((None, D_MODEL, D_EXPERT), lambda i, be, bv: (be[i], 0, 0)),
                  pl.BlockSpec((None, D_MODEL, D_EXPERT), lambda i, be, bv: (be[i], 0, 0)),
                  pl.BlockSpec((None, D_EXPERT, D_MODEL), lambda i, be, bv: (be[i], 0, 0))],
        out_specs=blk,
        scratch_shapes=[pltpu.VMEM((D_MODEL, D_EXPERT), BF16), pltpu.VMEM((D_MODEL, D_EXPERT), BF16),
                        pltpu.VMEM((D_EXPERT, D_MODEL), BF16)],
    )
    return pl.pallas_call(
        _expert_kernel,
        grid_spec=grid_spec,
        out_shape=jax.ShapeDtypeStruct(xs.shape, F32),
        name="routed_experts",
        compiler_params=_params("arbitrary"),
    )(block_expert, block_valid, xs, wg, wu, wd)


def _combine_kernel(dest_ref, dest_next_ref, x1_ref, route_ref, g_ref, yb_hbm, o_ref, ybuf, sem):
    i = pl.program_id(0)
    n = pl.num_programs(0)
    slot = i % 2

    def row_copy(idx8, r, k, s):
        dst = ybuf.at[s, pl.ds((k * ROW_TILE + r) * ROW_SUBLANES, ROW_SUBLANES)]
        return pltpu.make_async_copy(_row_tile(yb_hbm, idx8), dst, sem.at[s])

    @pl.when(i == 0)
    def _():
        def body(r, c):
            idx = [dest_ref[0, TOP_K * r + k] for k in range(TOP_K)]
            for k in range(TOP_K):
                row_copy(idx[k], r, k, 0).start()
            return c
        lax.fori_loop(0, ROW_TILE, body, 0)

    @pl.when(i + 1 < n)
    def _():
        for r0 in range(0, ROW_TILE, INDEX_BATCH):
            idx = [dest_next_ref[0, j] for j in range(TOP_K * r0, TOP_K * (r0 + INDEX_BATCH))]
            for r in range(r0, r0 + INDEX_BATCH):
                for k in range(TOP_K):
                    row_copy(idx[TOP_K * (r - r0) + k], r, k, 1 - slot).start()

    _tile_wait(yb_hbm, ybuf.at[slot], sem.at[slot])
    route = route_ref[...]
    buf = ybuf.at[slot]
    y = (_load_rows(buf, ROW_TILE) * route[:, 2:3]
         + _load_rows(buf, ROW_TILE, offset=ROW_TILE * ROW_SUBLANES) * route[:, 3:4])
    x2 = x1_ref[...] + y
    ms = jnp.mean(x2 * x2, axis=-1, keepdims=True)
    o_ref[...] = x2 * lax.rsqrt(ms + EPS) * g_ref[...]


def _combine_and_norm(dest8, x1, route, norm_w, yb):
    t = x1.shape[0]
    n_tiles = t // ROW_TILE
    dest_spec = lambda f: pl.BlockSpec((None, 1, TILE_ROW_COPIES), lambda i: (f(i), 0, 0), memory_space=pltpu.SMEM)
    return pl.pallas_call(
        _combine_kernel,
        grid=(n_tiles,),
        in_specs=[dest_spec(lambda i: i), dest_spec(lambda i: jnp.minimum(i + 1, n_tiles - 1)),
                  pl.BlockSpec((ROW_TILE, D_MODEL), lambda i: (i, 0)),
                  pl.BlockSpec((ROW_TILE, LANES), lambda i: (i, 0)),
                  _const_spec((1, D_MODEL)),
                  pl.BlockSpec(memory_space=pl.ANY)],
        out_specs=pl.BlockSpec((ROW_TILE, D_MODEL), lambda i: (i, 0)),
        out_shape=jax.ShapeDtypeStruct((t, D_MODEL), F32),
        scratch_shapes=[pltpu.VMEM((2, TILE_ROW_COPIES * ROW_SUBLANES, LANES), F32),
                        pltpu.SemaphoreType.DMA((2,))],
        name="combine_and_norm",
        compiler_params=_params("arbitrary", disable_bounds_checks=True),
    )(dest8, dest8, x1, route, norm_w, yb)


def _dispatch_plan(route, counts):
    t = route.shape[0]
    counts = counts[0, 0:N_EXPERTS].astype(jnp.int32)
    padded = (counts + MOE_BLOCK - 1) // MOE_BLOCK * MOE_BLOCK
    pad_end = jnp.cumsum(padded)
    pad_start = pad_end - padded
    expert = route[:, 0:TOP_K].astype(jnp.int32)
    rank = route[:, 4:4 + TOP_K].astype(jnp.int32)
    ids = jnp.arange(N_EXPERTS, dtype=jnp.int32)
    start_of = jnp.sum(jnp.where(expert[:, :, None] == ids, pad_start, 0), axis=-1)
    dest8 = ((start_of + rank) * ROW_SUBLANES).reshape(t // ROW_TILE, 1, TILE_ROW_COPIES)
    n_blocks = -(-(t * TOP_K + N_EXPERTS * (MOE_BLOCK - 1)) // MOE_BLOCK)
    block_start = jnp.arange(n_blocks, dtype=jnp.int32) * MOE_BLOCK
    block_expert = jnp.minimum(jnp.sum((pad_end[None, :] <= block_start[:, None]).astype(jnp.int32), axis=1),
                               N_EXPERTS - 1)
    block_valid = (block_start < pad_end[-1]).astype(jnp.int32)
    n_slots = n_blocks * MOE_BLOCK
    fill_start = jnp.concatenate([pad_start + counts, pad_end[-1:]])
    fill_n = jnp.concatenate([padded - counts, n_slots - pad_end[-1:]])
    return dest8, fill_start, fill_n, block_expert, block_valid, n_slots


def _rel_bias_band(rel_table, qc):
    rows = qc * CHUNK
    window = ATTN_REACH + rows
    n = window + rows - 1
    diag = rel_table[:, np.clip(ATTN_REACH + rows - 1 - np.arange(n), -MAX_REL, MAX_REL) + MAX_REL].astype(F32)
    shifted = jnp.tile(diag, (1, rows + 1))[:, :rows * (n + 1)].reshape(N_HEADS_A, rows, n + 1)[:, :, :window]
    bias = shifted[:, ::-1, :] * LOG2E
    i = np.arange(rows)[:, None]
    j = np.arange(window)[None, :]
    first = (i // CHUNK) * CHUNK
    return jnp.where((j >= first) & (j < first + BAND), bias, NEG_INF)


def _pad_lanes(v, width=LANES):
    return jnp.pad(v.astype(F32), (0, width - v.shape[0])).reshape(1, width)


def _trunk(x, kv_cache, conv_state, ssm_state, w):
    batch, l, _ = x.shape
    nc = l // CHUNK
    t = batch * l
    x2d = x.reshape(t, D_MODEL)
    proj = _in_projection(x2d, w["norm_mix"], w["w_proj"])
    proj3 = proj.reshape(batch, l, PROJ_WIDTH)
    if kv_cache is None:
        qc = ATTN_CHUNKS_PER_STEP if nc % ATTN_CHUNKS_PER_STEP == 0 else 1
        kv_spec = [pl.BlockSpec((l, ATTN_WIDTH), lambda b, c: (b, K0 // ATTN_WIDTH)),
                   pl.BlockSpec((l, ATTN_WIDTH), lambda b, c: (b, V0 // ATTN_WIDTH))]
        attn = _band_attention(proj, Q0 // ATTN_WIDTH, proj, proj, kv_spec, _rel_bias_band(w["rel_table"], qc),
                               batch=batch, nc=nc, lk=l, pad_rows=ATTN_REACH, qc=qc)
        keep = min(ATTN_REACH, l)
        k_keep = proj3[:, l - keep:, K0:K0 + ATTN_WIDTH]
        v_keep = proj3[:, l - keep:, V0:V0 + ATTN_WIDTH]
    else:
        assert nc == 1
        cache_k, cache_v = kv_cache
        wlen = cache_k.shape[1]
        kk = jnp.concatenate([cache_k.reshape(batch, wlen, ATTN_WIDTH), proj3[:, :, K0:K0 + ATTN_WIDTH]], axis=1)
        vv = jnp.concatenate([cache_v.reshape(batch, wlen, ATTN_WIDTH), proj3[:, :, V0:V0 + ATTN_WIDTH]], axis=1)
        spec = pl.BlockSpec((None, wlen + l, ATTN_WIDTH), lambda b, c: (b, 0, 0))
        attn = _band_attention(proj, Q0 // ATTN_WIDTH, kk, vv, [spec, spec], _rel_bias_band(w["rel_table"], 1),
                               batch=batch, nc=nc, lk=wlen + l, pad_rows=BAND - (wlen + l), qc=1)
        k_keep, v_keep = kk[:, l:], vv[:, l:]
    tail = proj3[:, l - (CONV_WIDTH - 1):, XBC0:XBC0 + CONV_DIM]
    if conv_state is None:
        conv0 = st0 = None
        conv_new = tail
    else:
        conv0 = jnp.pad(conv_state, ((0, 0), (8 - (CONV_WIDTH - 1), 0), (0, 0)))
        st0 = jnp.transpose(ssm_state.reshape(batch, D_INNER, D_STATE), (0, 2, 1))
        conv_new = jnp.concatenate([conv_state, tail], axis=1)[:, -(CONV_WIDTH - 1):]
    ssm, st_t = _ssd_mixer(proj, conv0, st0, w["conv_w"], w["conv_b"], w["dt_bias"], w["a_log"], w["d_exp"],
                           w["ssm_norm"], batch=batch, nc=nc)
    ssm_new = jnp.transpose(st_t, (0, 2, 1)).reshape(batch, SSM_HEADS, SSM_HEAD_DIM, D_STATE)
    x1, hn, route, counts = _mix_and_route(x2d, attn, ssm, proj, w["wa"], w["wb"], w["wo"], w["norm_ffn"],
                                           w["w_router"], w["b_router"])
    dest8, fill_start, fill_n, block_expert, block_valid, n_slots = _dispatch_plan(route, counts)
    xs = _dispatch(dest8, fill_start, fill_n, hn, n_slots)
    yb = _routed_experts(xs, block_expert, block_valid, w["w_gate"], w["w_up"], w["w_down"])
    y = _combine_and_norm(dest8, x1, route, w["norm_final"], yb)
    heads = lambda u: u.reshape(batch, u.shape[1], N_HEADS_A, HEAD_DIM_A)
    return y.reshape(batch, l, D_MODEL), heads(k_keep), heads(v_keep), conv_new, ssm_new


def kernel(x_prompt, x_sample, cache_attn_k, cache_attn_v, state_conv, state_ssm, norm_mix, w_in, rel_bias, conv_w, conv_b, dt_bias, a_log, d_skip, ssm_norm, w_branch_a, w_branch_b, w_out, norm_ffn, w_router_group, b_router_group, w_router_expert, b_router_expert, w_gate, w_up, w_down, norm_final):
    assert norm_mix.shape[0] == 1, "single-layer trunk"
    q, k, v, z, xbc, dtc, ga, gb = jnp.split(w_in[0], np.cumsum(
        (ATTN_WIDTH, ATTN_WIDTH, ATTN_WIDTH, D_INNER, CONV_DIM, SSM_HEADS, D_MODEL, D_MODEL))[:-1].tolist(), axis=1)
    w_proj = jnp.concatenate([z, ga, gb, xbc, q, k, v, jnp.pad(dtc, ((0, 0), (0, LANES - SSM_HEADS)))],
                             axis=1).astype(BF16)
    w_router = jnp.pad(jnp.concatenate([w_router_group[0], w_router_expert[0]], axis=1),
                       ((0, 0), (0, LANES - N_EXPERT_GROUPS - N_EXPERTS)))
    w_router_hi = w_router.astype(BF16)
    w_router_lo = (w_router - w_router_hi.astype(F32)).astype(BF16)
    b_router = _pad_lanes(jnp.concatenate([b_router_group[0], b_router_expert[0]]))
    w = dict(
        norm_mix=norm_mix[0].reshape(1, D_MODEL), w_proj=w_proj, rel_table=rel_bias[0],
        conv_w=conv_w[0], conv_b=conv_b[0].reshape(1, CONV_DIM), dt_bias=_pad_lanes(dt_bias[0]),
        a_log=_pad_lanes(a_log[0]), d_exp=jnp.repeat(d_skip[0], SSM_HEAD_DIM).reshape(1, D_INNER),
        ssm_norm=ssm_norm[0].reshape(1, D_INNER),
        wa=w_branch_a[0].astype(BF16), wb=w_branch_b[0].astype(BF16), wo=w_out[0].astype(BF16),
        norm_ffn=norm_ffn[0].reshape(1, D_MODEL),
        w_router=jnp.concatenate([w_router_hi, w_router_lo], axis=1), b_router=b_router,
        w_gate=w_gate[0], w_up=w_up[0], w_down=w_down[0],
        norm_final=norm_final.reshape(1, D_MODEL),
    )
    yp, kp, vp, cp, sp = _trunk(x_prompt, None, None, None, w)
    ys, ks, vs, cs, ss = _trunk(x_sample, (cache_attn_k[0], cache_attn_v[0]), state_conv[0], state_ssm[0], w)
    return (yp, ys, kp[None], vp[None], cp[None], sp[None], ks[None], vs[None], cs[None], ss[None])
```

```python
import functools
import math

import numpy as np
import jax
import jax.numpy as jnp
from jax import lax
from jax.experimental import pallas as pl
from jax.experimental.pallas import tpu as pltpu

F32 = jnp.float32
BF16 = jnp.bfloat16

D_MODEL = 1024
CHUNK = 64
LEFT_CHUNKS = 8
ATTN_REACH = LEFT_CHUNKS * CHUNK
BAND = ATTN_REACH + CHUNK
N_HEADS_A = 8
HEAD_DIM_A = 64
ATTN_WIDTH = N_HEADS_A * HEAD_DIM_A
MAX_REL = 256
ATTN_SCALE = 1.0 / math.sqrt(HEAD_DIM_A)
NEG_INF = -1e30
SSM_HEADS = 16
SSM_HEAD_DIM = 64
D_INNER = SSM_HEADS * SSM_HEAD_DIM
SSM_GROUPS = 2
GROUP_WIDTH = D_INNER // SSM_GROUPS
D_STATE = 128
CONV_WIDTH = 4
CONV_DIM = D_INNER + 2 * SSM_GROUPS * D_STATE
N_EXPERT_GROUPS = 4
EXPERTS_PER_GROUP = 8
N_EXPERTS = N_EXPERT_GROUPS * EXPERTS_PER_GROUP
TOP_K = 2
D_EXPERT = 512
MOE_BLOCK = 128
EPS = 1e-6

LANES = 128
SUBLANES = 8
Z0, GA0, GB0, XBC0, Q0, K0, V0 = 0, 1024, 2048, 3072, 4608, 5120, 5632
PROJ_WIDTH = V0 + ATTN_WIDTH
PROJ_TILE = 1024
ROW_TILE = 256
ATTN_CHUNKS_PER_STEP = 2
SSD_STREAMS = 4
VMEM_LIMIT = 56 * 1024 * 1024

_NT = (((1,), (1,)), ((), ()))


def _params(*sem, **kw):
    return pltpu.CompilerParams(dimension_semantics=sem, vmem_limit_bytes=VMEM_LIMIT, **kw)


def _const_spec(shape):
    nd = len(shape)
    return pl.BlockSpec(shape, lambda *_: (0,) * nd)


def _inproj_kernel(x_ref, g_ref, w_ref, o_ref, dt_ref):
    x = x_ref[...]
    ms = jnp.mean(x * x, axis=-1, keepdims=True)
    h = (x * lax.rsqrt(ms + EPS) * g_ref[...]).astype(BF16)
    for j in range(0, PROJ_WIDTH, PROJ_TILE):
        o_ref[:, j:j + PROJ_TILE] = jnp.dot(h, w_ref[:, j:j + PROJ_TILE], preferred_element_type=F32).astype(BF16)
    dt_ref[...] = jnp.dot(h, w_ref[:, PROJ_WIDTH:PROJ_WIDTH + LANES], preferred_element_type=F32)


def _in_projection(x2d, norm_w, w_proj):
    t = x2d.shape[0]
    return pl.pallas_call(
        _inproj_kernel,
        grid=(t // ROW_TILE,),
        in_specs=[pl.BlockSpec((ROW_TILE, D_MODEL), lambda i: (i, 0)),
                  _const_spec((1, D_MODEL)),
                  _const_spec((D_MODEL, PROJ_WIDTH + LANES))],
        out_specs=[pl.BlockSpec((ROW_TILE, PROJ_WIDTH), lambda i: (i, 0)),
                   pl.BlockSpec((ROW_TILE, LANES), lambda i: (i, 0))],
        out_shape=[jax.ShapeDtypeStruct((t, PROJ_WIDTH), BF16), jax.ShapeDtypeStruct((t, LANES), F32)],
        name="in_projection",
        compiler_params=_params("parallel"),
    )(x2d, norm_w, w_proj)


LOG2E = math.log2(math.e)


def _attn_kernel(q_ref, k_ref, v_ref, bias_ref, o_ref, kpad, vpad, *, pad_rows, lk, qc):
    rows = qc * CHUNK
    window = ATTN_REACH + rows
    c = pl.program_id(1)

    @pl.when(c == 0)
    def _():
        if pad_rows:
            kpad[0:pad_rows, :] = jnp.zeros((pad_rows, ATTN_WIDTH), BF16)
            vpad[0:pad_rows, :] = jnp.zeros((pad_rows, ATTN_WIDTH), BF16)
        kpad[pad_rows:pad_rows + lk, :] = k_ref[...].astype(BF16)
        vpad[pad_rows:pad_rows + lk, :] = v_ref[...].astype(BF16)

    start = pl.multiple_of(c * rows, rows)

    def body(mask_start):
        kw = kpad[pl.ds(start, window), :]
        vw = vpad[pl.ds(start, window), :]
        q = (q_ref[...].astype(F32) * (ATTN_SCALE * LOG2E)).astype(BF16)
        low = lax.broadcasted_iota(jnp.int32, (rows, LANES), 1) < HEAD_DIM_A
        if mask_start:
            valid = lax.broadcasted_iota(jnp.int32, (rows, window), 1) + start >= pad_rows
        for hp in range(N_HEADS_A // 2):
            sl = slice(LANES * hp, LANES * (hp + 1))
            q2, k2, v2 = q[:, sl], kw[:, sl], vw[:, sl]
            outs = []
            for sub in range(2):
                qm = jnp.where(low if sub == 0 else jnp.logical_not(low), q2, jnp.zeros_like(q2))
                s = lax.dot_general(qm, k2, _NT, preferred_element_type=F32) + bias_ref[2 * hp + sub]
                if mask_start:
                    s = jnp.where(valid, s, NEG_INF)
                m = jnp.max(s, axis=-1, keepdims=True)
                p = jnp.exp2(s - m)
                l = jnp.sum(p, axis=-1, keepdims=True)
                o = jnp.dot(p.astype(BF16), v2, preferred_element_type=F32)
                outs.append(o / l)
            o_ref[:, sl] = jnp.where(low, outs[0], outs[1]).astype(o_ref.dtype)

    if pad_rows:
        pl.when(start < pad_rows)(lambda: body(True))
        pl.when(start >= pad_rows)(lambda: body(False))
    else:
        body(False)


def _band_attention(q_arr, q_col, k_arr, v_arr, kv_spec, bias, *, batch, nc, lk, pad_rows, qc):
    steps = nc // qc
    rows = qc * CHUNK
    kern = functools.partial(_attn_kernel, pad_rows=pad_rows, lk=lk, qc=qc)
    return pl.pallas_call(
        kern,
        grid=(batch, steps),
        in_specs=[pl.BlockSpec((rows, ATTN_WIDTH), lambda b, c: (b * steps + c, q_col)),
                  kv_spec[0], kv_spec[1],
                  _const_spec((N_HEADS_A, rows, ATTN_REACH + rows))],
        out_specs=pl.BlockSpec((rows, ATTN_WIDTH), lambda b, c: (b * steps + c, 0)),
        out_shape=jax.ShapeDtypeStruct((batch * nc * CHUNK, ATTN_WIDTH), BF16),
        scratch_shapes=[pltpu.VMEM((pad_rows + lk, ATTN_WIDTH), BF16),
                        pltpu.VMEM((pad_rows + lk, ATTN_WIDTH), BF16)],
        name="band_attention",
        compiler_params=_params("parallel", "arbitrary"),
    )(q_arr, k_arr, v_arr, bias)


CARRY_ROWS = SUBLANES
SHIFT_TAPS = CONV_WIDTH - 1
SHIFT_K = 2 * CARRY_ROWS + CHUNK


def _split_bf16(x, parts):
    out = []
    for _ in range(parts - 1):
        h = x.astype(BF16)
        out.append(h)
        x = x - h.astype(F32)
    out.append(x.astype(BF16))
    return out


def _ssd_kernel(xbc_ref, z_ref, dt_ref, conv0_ref, st0_ref, cw_ref, cb_ref, dtb_ref, alog_ref,
                dexp_ref, nw_ref, tri_ref, exp_ref, sel_ref, eye_ref, shift_ref,
                y_ref, stout_ref, carry, st, *, has_init, streams, nc):
    c = pl.program_id(1)

    @pl.when(c == 0)
    def _():
        if has_init:
            carry[...] = conv0_ref[...]
            st[...] = st0_ref[...]
        else:
            carry[...] = jnp.zeros((streams, CARRY_ROWS, CONV_DIM), F32)
            st[...] = jnp.zeros((streams, D_STATE, D_INNER), F32)

    chunks = [_ssd_chunk(xbc_ref.at[s], z_ref.at[s], dt_ref.at[s], cw_ref, cb_ref, dtb_ref, alog_ref, dexp_ref,
                         nw_ref, tri_ref, exp_ref, sel_ref, eye_ref, shift_ref, y_ref.at[s], carry.at[s], st.at[s])
              for s in range(streams)]
    while chunks:
        chunks = [g for g in chunks if next(g, True) is None]

    @pl.when(c == nc - 1)
    def _():
        stout_ref[...] = st[...]


def _ssd_chunk(xbc_ref, z_ref, dt_ref, cw_ref, cb_ref, dtb_ref, alog_ref, dexp_ref, nw_ref,
               tri_ref, exp_ref, sel_ref, eye_ref, shift_ref, y_ref, carry, st):
    x_f = xbc_ref[...].astype(F32)
    prev = carry[...]
    prev_hi = prev.astype(BF16).astype(F32)
    stacked = jnp.concatenate([prev_hi, prev - prev_hi, x_f], axis=0).astype(BF16)
    taps = jnp.dot(shift_ref[...], stacked, preferred_element_type=F32)

    dt_in = dt_ref[...] + dtb_ref[...]
    dt = jnp.maximum(dt_in, 0.0) + jnp.log1p(jnp.exp(-jnp.abs(dt_in)))
    da = dt * (-jnp.exp(alog_ref[...]))
    tri = tri_ref[...]
    cs3 = jnp.dot(tri, jnp.concatenate(_split_bf16(da, 3), axis=1), preferred_element_type=F32)
    yield

    conv = cb_ref[...] + x_f * cw_ref[CONV_WIDTH - 1:CONV_WIDTH, :]
    for j in range(SHIFT_TAPS):
        conv = conv + taps[j * CHUNK:(j + 1) * CHUNK, :] * cw_ref[j:j + 1, :]
    u = conv * jax.nn.sigmoid(conv)
    xs = u[:, 0:D_INNER]
    cs = cs3[:, 0:LANES] + cs3[:, LANES:2 * LANES] + cs3[:, 2 * LANES:3 * LANES]
    cs_last = cs[CHUNK - 1:CHUNK, :]
    ecs = jnp.exp(cs)
    to_end = jnp.exp(cs_last - cs)

    stack = jnp.concatenate(_split_bf16(dt, 2) + _split_bf16(ecs, 2) + _split_bf16(to_end, 2)
                            + _split_bf16(cs, 3), axis=0)
    wide = jnp.dot(stack, exp_ref[...], preferred_element_type=F32)

    lane = lax.broadcasted_iota(jnp.int32, (CHUNK, LANES), 1)
    even = (lane % 2) == 0
    csm = jnp.concatenate([jnp.where(even, cs, 0.0), jnp.where(even, 0.0, cs)], axis=0)
    srow3 = lax.dot_general(sel_ref[...], jnp.concatenate(_split_bf16(csm, 3), axis=0), _NT,
                            preferred_element_type=F32)
    yield

    rows = [wide[i * CHUNK:(i + 1) * CHUNK, :] for i in range(9)]
    dt_w = rows[0] + rows[1]
    ecs_w = rows[2] + rows[3]
    to_end_w = rows[4] + rows[5]
    cs_w = rows[6] + rows[7] + rows[8]
    srow = srow3[:, 0:LANES] + srow3[:, LANES:2 * LANES] + srow3[:, 2 * LANES:3 * LANES]

    xdt = xs * dt_w
    xdt_b = xdt.astype(BF16)
    xw_b = (xdt * to_end_w).astype(BF16)
    low = lane < SSM_HEAD_DIM
    t_idx = lax.broadcasted_iota(jnp.int32, (CHUNK, LANES), 0)
    causal = t_idx >= (lane % SSM_HEAD_DIM)
    eye = eye_ref[...]

    y_parts = []
    for g in range(SSM_GROUPS):
        b_g = u[:, D_INNER + g * D_STATE:D_INNER + (g + 1) * D_STATE].astype(BF16)
        c_g = u[:, D_INNER + (SSM_GROUPS + g) * D_STATE:D_INNER + (SSM_GROUPS + g + 1) * D_STATE].astype(BF16)
        gsl = slice(g * GROUP_WIDTH, (g + 1) * GROUP_WIDTH)
        st_g = st[:, gsl]
        y_off = jnp.dot(c_g, st_g.astype(BF16), preferred_element_type=F32) * ecs_w[:, gsl]
        b_t = lax.dot_general(eye, b_g, _NT, preferred_element_type=F32).astype(BF16)
        st[:, gsl] = st_g * ecs_w[CHUNK - 1:CHUNK, gsl] + jnp.dot(b_t, xw_b[:, gsl], preferred_element_type=F32)
        cb2 = lax.dot_general(c_g, jnp.concatenate([b_g, b_g], axis=0), _NT, preferred_element_type=F32)
        yield
        for kk in range(SSM_HEADS // SSM_GROUPS // 2):
            k = g * (SSM_HEADS // SSM_GROUPS // 2) + kk
            psl = slice(k * LANES, (k + 1) * LANES)
            decay = jnp.where(causal, jnp.exp(cs_w[:, psl] - srow[k:k + 1, :]), 0.0)
            gmat = (cb2 * decay).astype(BF16)
            xp = xdt_b[:, psl]
            xblk = jnp.concatenate([jnp.where(low, xp, jnp.zeros_like(xp)),
                                    jnp.where(low, jnp.zeros_like(xp), xp)], axis=0)
            y_parts.append(jnp.dot(gmat, xblk, preferred_element_type=F32)
                           + y_off[:, kk * LANES:(kk + 1) * LANES])
        yield
    carry[...] = x_f[CHUNK - CARRY_ROWS:CHUNK, :]
    y = jnp.concatenate(y_parts, axis=1) + xs * dexp_ref[...]
    zz = z_ref[...].astype(F32)
    y = y * (zz * jax.nn.sigmoid(zz))
    outs = []
    for g in range(SSM_GROUPS):
        yg = y[:, g * GROUP_WIDTH:(g + 1) * GROUP_WIDTH]
        outs.append(yg * lax.rsqrt(jnp.mean(yg * yg, axis=-1, keepdims=True) + EPS))
    y_ref[...] = (jnp.concatenate(outs, axis=1) * nw_ref[...]).astype(y_ref.dtype)


def _ssd_constants():
    tri = np.tril(np.ones((CHUNK, CHUNK), np.float32))
    expand = np.zeros((LANES, D_INNER), np.float32)
    for e in range(SSM_HEADS):
        expand[e, e * SSM_HEAD_DIM:(e + 1) * SSM_HEAD_DIM] = 1.0
    sel = np.zeros((16, LANES), np.float32)
    for e in range(SSM_HEADS):
        sel[e // 2, e] = 1.0
    eye = np.eye(D_STATE, dtype=np.float32)
    shift = np.zeros((SHIFT_TAPS * CHUNK, SHIFT_K), np.float32)
    for j in range(SHIFT_TAPS):
        for t in range(CHUNK):
            m = t - (CONV_WIDTH - 1) + j
            if m >= 0:
                shift[j * CHUNK + t, 2 * CARRY_ROWS + m] = 1.0
            else:
                shift[j * CHUNK + t, CARRY_ROWS + m] = 1.0
                shift[j * CHUNK + t, 2 * CARRY_ROWS + m] = 1.0
    return tuple(jnp.asarray(a, BF16) for a in (tri, expand, sel, eye, shift))


def _ssd_mixer(proj, dt_raw, conv0, st0, conv_w, conv_b, dt_bias_p, a_log_p, d_exp, norm_w, *, batch, nc):
    streams = SSD_STREAMS if batch % SSD_STREAMS == 0 else 1
    groups = batch // streams
    l = nc * CHUNK
    has_init = conv0 is not None
    if not has_init:
        conv0 = jnp.zeros((1, streams, CARRY_ROWS, CONV_DIM), F32)
        st0 = jnp.zeros((1, streams, D_STATE, D_INNER), F32)
        init_map = lambda b, c: (0, 0, 0, 0)
    else:
        conv0 = conv0.reshape(groups, streams, CARRY_ROWS, CONV_DIM)
        st0 = st0.reshape(groups, streams, D_STATE, D_INNER)
        init_map = lambda b, c: (b, 0, 0, 0)
    tri, expand, sel, eye, shift = _ssd_constants()
    proj4 = proj.reshape(groups, streams, l, PROJ_WIDTH)
    dt4 = dt_raw.reshape(groups, streams, l, LANES)
    chunk_of = lambda w, col: pl.BlockSpec((None, streams, CHUNK, w), lambda b, c: (b, 0, c, col))
    state_spec = lambda m: pl.BlockSpec((None, streams, D_STATE, D_INNER), m)
    kern = functools.partial(_ssd_kernel, has_init=has_init, streams=streams, nc=nc)
    y, st_out = pl.pallas_call(
        kern,
        grid=(groups, nc),
        in_specs=[chunk_of(CONV_DIM, XBC0 // CONV_DIM), chunk_of(D_INNER, Z0 // D_INNER), chunk_of(LANES, 0),
                  pl.BlockSpec((None, streams, CARRY_ROWS, CONV_DIM), init_map), state_spec(init_map),
                  _const_spec((CONV_WIDTH, CONV_DIM)), _const_spec((1, CONV_DIM)),
                  _const_spec((1, LANES)), _const_spec((1, LANES)),
                  _const_spec((1, D_INNER)), _const_spec((1, D_INNER)),
                  _const_spec((CHUNK, CHUNK)), _const_spec((LANES, D_INNER)),
                  _const_spec((16, LANES)), _const_spec((D_STATE, D_STATE)),
                  _const_spec((SHIFT_TAPS * CHUNK, SHIFT_K))],
        out_specs=[chunk_of(D_INNER, 0), state_spec(lambda b, c: (b, 0, 0, 0))],
        out_shape=[jax.ShapeDtypeStruct((groups, streams, l, D_INNER), BF16),
                   jax.ShapeDtypeStruct((groups, streams, D_STATE, D_INNER), F32)],
        scratch_shapes=[pltpu.VMEM((streams, CARRY_ROWS, CONV_DIM), F32),
                        pltpu.VMEM((streams, D_STATE, D_INNER), F32)],
        name="ssd_mixer",
        compiler_params=_params("parallel", "arbitrary"),
    )(proj4, proj4, dt4, conv0, st0, conv_w, conv_b, dt_bias_p, a_log_p, d_exp, norm_w, tri, expand, sel, eye, shift)
    return y.reshape(batch * l, D_INNER), st_out.reshape(batch, D_STATE, D_INNER)


ROW_SUBLANES = D_MODEL // LANES
assert ROW_SUBLANES == SUBLANES


def _store_rows(ref, x, rows, offset=0):
    for s in range(ROW_SUBLANES):
        ref[pl.ds(offset + s, rows, stride=ROW_SUBLANES), :] = x[:, s * LANES:(s + 1) * LANES]


def _load_rows(ref, rows, offset=0):
    return jnp.concatenate([ref[pl.ds(offset + s, rows, stride=ROW_SUBLANES), :] for s in range(ROW_SUBLANES)],
                           axis=1)


def _row_tile(ref, idx8):
    return ref.at[pl.ds(pl.multiple_of(idx8, ROW_SUBLANES), ROW_SUBLANES)]


def _mix_kernel(x_ref, attn_ref, ssm_ref, ga_ref, gb_ref, wa_ref, wb_ref, wo_ref, g_ref, wr_ref, br_ref, slt_ref,
                count0_ref, x1_ref, hn_ref, route_ref, counts_ref, carry):
    @pl.when(pl.program_id(0) == 0)
    def _():
        carry[...] = count0_ref[...]

    a = jnp.dot(attn_ref[...], wa_ref[...], preferred_element_type=F32)
    s = jnp.dot(ssm_ref[...], wb_ref[...], preferred_element_type=F32)
    mixed = jax.nn.sigmoid(ga_ref[...].astype(F32)) * a + jax.nn.sigmoid(gb_ref[...].astype(F32)) * s
    x1 = x_ref[...] + jnp.dot(mixed.astype(BF16), wo_ref[...], preferred_element_type=F32)
    x1_ref[...] = x1
    ms = jnp.mean(x1 * x1, axis=-1, keepdims=True)
    hn = x1 * lax.rsqrt(ms + EPS) * g_ref[...]
    _store_rows(hn_ref, hn, ROW_TILE)

    hi = hn.astype(BF16)
    lo = (hn - hi.astype(F32)).astype(BF16)
    both = jnp.dot(hi, wr_ref[...], preferred_element_type=F32)
    logits = (both[:, 0:LANES] + both[:, LANES:2 * LANES]
              + jnp.dot(lo, wr_ref[:, 0:LANES], preferred_element_type=F32) + br_ref[...])
    lane = lax.broadcasted_iota(jnp.int32, logits.shape, 1)
    lane_f = lane.astype(F32)
    big = float(LANES)
    is_g = lane < N_EXPERT_GROUPS
    gl = jnp.where(is_g, logits, -jnp.inf)
    gmax = jnp.max(gl, axis=-1, keepdims=True)
    gidx = jnp.min(jnp.where(gl == gmax, lane_f, big), axis=-1, keepdims=True)
    gsum = jnp.sum(jnp.where(is_g, jnp.exp(gl - gmax), 0.0), axis=-1, keepdims=True)
    g_top = 1.0 / gsum
    first = N_EXPERT_GROUPS + gidx * EXPERTS_PER_GROUP
    in_group = (lane_f >= first) & (lane_f < first + EXPERTS_PER_GROUP)
    el = jnp.where(in_group, logits, -jnp.inf)
    m1 = jnp.max(el, axis=-1, keepdims=True)
    i1 = jnp.min(jnp.where(el == m1, lane_f, big), axis=-1, keepdims=True)
    el2 = jnp.where(lane_f == i1, -jnp.inf, el)
    m2 = jnp.max(el2, axis=-1, keepdims=True)
    i2 = jnp.min(jnp.where(el2 == m2, lane_f, big), axis=-1, keepdims=True)
    r = jnp.exp(m2 - m1)
    w1 = g_top / (1.0 + r)
    w2 = g_top * r / (1.0 + r)
    e1 = i1 - N_EXPERT_GROUPS
    e2 = i2 - N_EXPERT_GROUPS

    oh1 = lane_f == e1
    oh2 = lane_f == e2
    hot1 = jnp.where(oh1, 1.0, 0.0)
    hot2 = jnp.where(oh2, 1.0, 0.0)
    onehots = jnp.concatenate([hot1, hot2], axis=1).astype(BF16)
    before = jnp.dot(slt_ref[...], onehots, preferred_element_type=F32)
    cnt1 = jnp.sum(hot1, axis=0, keepdims=True)
    cnt2 = jnp.sum(hot2, axis=0, keepdims=True)
    base = carry[...]
    rank1 = jnp.sum(jnp.where(oh1, before[:, 0:LANES] + base, 0.0), axis=-1, keepdims=True)
    rank2 = jnp.sum(jnp.where(oh2, before[:, LANES:2 * LANES] + (base + cnt1), 0.0), axis=-1, keepdims=True)
    carry[...] = base + cnt1 + cnt2
    counts_ref[...] = carry[...]

    route = jnp.zeros_like(logits)
    for col, val in enumerate((e1, e2, w1, w2, rank1, rank2)):
        route = jnp.where(lane == col, val, route)
    route_ref[...] = route


def _mix_and_route(x2d, attn, ssm, proj, wa, wb, wo, norm_w, w_router, b_router, count0):
    t = x2d.shape[0]
    tile = lambda w, col: pl.BlockSpec((ROW_TILE, w), lambda i: (i, col))
    slt = jnp.asarray(np.tril(np.ones((ROW_TILE, ROW_TILE), np.float32), k=-1), BF16)
    return pl.pallas_call(
        _mix_kernel,
        grid=(t // ROW_TILE,),
        in_specs=[tile(D_MODEL, 0), tile(ATTN_WIDTH, 0), tile(D_INNER, 0),
                  tile(D_MODEL, GA0 // D_MODEL), tile(D_MODEL, GB0 // D_MODEL),
                  _const_spec((ATTN_WIDTH, D_MODEL)), _const_spec((D_INNER, D_MODEL)),
                  _const_spec((D_MODEL, D_MODEL)), _const_spec((1, D_MODEL)),
                  _const_spec((D_MODEL, 2 * LANES)), _const_spec((1, LANES)),
                  _const_spec((ROW_TILE, ROW_TILE)), _const_spec((1, LANES))],
        out_specs=[tile(D_MODEL, 0), pl.BlockSpec((ROW_TILE * ROW_SUBLANES, LANES), lambda i: (i, 0)),
                   tile(LANES, 0), _const_spec((1, LANES))],
        out_shape=[jax.ShapeDtypeStruct((t, D_MODEL), F32),
                   jax.ShapeDtypeStruct((t * ROW_SUBLANES, LANES), F32),
                   jax.ShapeDtypeStruct((t, LANES), F32),
                   jax.ShapeDtypeStruct((1, LANES), F32)],
        scratch_shapes=[pltpu.VMEM((1, LANES), F32)],
        name="mix_and_route",
        compiler_params=_params("arbitrary"),
    )(x2d, attn, ssm, proj, proj, wa, wb, wo, norm_w, w_router, b_router, slt, count0)


TILE_ROW_COPIES = ROW_TILE * TOP_K
INDEX_BATCH = 8
DISPATCH_BUFFERS = 3
DMA_THREADS = 2


def _tile_wait(ref_a, ref_b, sem):
    n = TILE_ROW_COPIES * ROW_SUBLANES
    pltpu.make_async_copy(ref_a.at[pl.ds(0, n)], ref_b.at[pl.ds(0, n)], sem).wait()


def _dispatch_kernel(fill_start_ref, fill_n_ref, dest_ref, *refs, tiles_per_source):
    n_src = len(tiles_per_source)
    sources = refs[:n_src]
    xs_hbm, hbuf, zbuf, in_sem, out_sem, fill_sem = refs[n_src:]
    i = pl.program_id(0)
    n = pl.num_programs(0)
    tile_rows = ROW_TILE * ROW_SUBLANES

    def tile_copy(src, j, b):
        return pltpu.make_async_copy(src.at[pl.ds(pl.multiple_of(j * tile_rows, tile_rows), tile_rows)],
                                     hbuf.at[b], in_sem.at[b])

    def tile_in_start(j):
        b = j % DISPATCH_BUFFERS
        first = 0
        for src, count in zip(sources, tiles_per_source):
            pl.when((j >= first) & (j < first + count))(lambda src=src, first=first: tile_copy(src, j - first, b).start())
            first += count

    @pl.when(i == 0)
    def _():
        zbuf[...] = jnp.zeros((ROW_SUBLANES, LANES), F32)
        tile_in_start(i)
        pl.when(n > 1)(lambda: tile_in_start(i + 1))

    buf = i % DISPATCH_BUFFERS
    tile_copy(sources[0], 0, buf).wait()
    for r0 in range(0, ROW_TILE, INDEX_BATCH):
        dests = [dest_ref[0, j] for j in range(TOP_K * r0, TOP_K * (r0 + INDEX_BATCH))]
        for r in range(r0, r0 + INDEX_BATCH):
            src = hbuf.at[buf, pl.ds(r * ROW_SUBLANES, ROW_SUBLANES)]
            for k in range(TOP_K):
                dst = _row_tile(xs_hbm, dests[TOP_K * (r - r0) + k])
                pltpu.make_async_copy(src, dst, out_sem.at[buf]).start(priority=k % DMA_THREADS)

    @pl.when(i > 0)
    def _():
        _tile_wait(xs_hbm, xs_hbm, out_sem.at[(i - 1) % DISPATCH_BUFFERS])

    pl.when(i + 2 < n)(lambda: tile_in_start(i + 2))

    @pl.when(i == n - 1)
    def _():
        _tile_wait(xs_hbm, xs_hbm, out_sem.at[buf])

        def per_range(e, carry):
            first = fill_start_ref[e]
            count = fill_n_ref[e]

            def start(j, c):
                pltpu.make_async_copy(zbuf, _row_tile(xs_hbm, (first + j) * ROW_SUBLANES), fill_sem.at[0]).start()
                return c

            def wait(j, c):
                pltpu.make_async_copy(zbuf, _row_tile(xs_hbm, 0), fill_sem.at[0]).wait()
                return c

            lax.fori_loop(0, count, start, 0)
            lax.fori_loop(0, count, wait, 0)
            return carry

        lax.fori_loop(0, N_EXPERTS + 1, per_range, 0)


def _dispatch(dest8, fill_start, fill_n, hn_list, n_slots):
    n_tiles = dest8.shape[0]
    tiles_per_source = tuple(h.shape[0] // (ROW_TILE * ROW_SUBLANES) for h in hn_list)
    assert sum(tiles_per_source) == n_tiles
    grid_spec = pltpu.PrefetchScalarGridSpec(
        num_scalar_prefetch=2,
        grid=(n_tiles,),
        in_specs=[pl.BlockSpec((None, 1, TILE_ROW_COPIES), lambda i, fs, fn: (i, 0, 0), memory_space=pltpu.SMEM)]
        + [pl.BlockSpec(memory_space=pl.ANY)] * len(hn_list),
        out_specs=pl.BlockSpec(memory_space=pl.ANY),
        scratch_shapes=[pltpu.VMEM((DISPATCH_BUFFERS, ROW_TILE * ROW_SUBLANES, LANES), F32),
                        pltpu.VMEM((ROW_SUBLANES, LANES), F32),
                        pltpu.SemaphoreType.DMA((DISPATCH_BUFFERS,)), pltpu.SemaphoreType.DMA((DISPATCH_BUFFERS,)),
                        pltpu.SemaphoreType.DMA((1,))],
    )
    return pl.pallas_call(
        functools.partial(_dispatch_kernel, tiles_per_source=tiles_per_source),
        grid_spec=grid_spec,
        out_shape=jax.ShapeDtypeStruct((n_slots * ROW_SUBLANES, LANES), F32),
        name="dispatch_rows",
        compiler_params=_params("arbitrary", disable_bounds_checks=True),
    )(fill_start, fill_n, dest8, *hn_list)


EXPERT_ROW_BUFFERS = 3


def _expert_kernel(bexp_ref, bval_ref, xs_hbm, wg_ref, wu_ref, wd_ref, y_hbm, wg_b, wu_b, wd_b,
                   xbuf, ybuf, in_sem, out_sem):
    i = pl.program_id(0)
    n = pl.num_programs(0)
    block_rows = MOE_BLOCK * ROW_SUBLANES
    buf = i % EXPERT_ROW_BUFFERS

    def block_of(ref, j):
        return ref.at[pl.ds(pl.multiple_of(j * block_rows, block_rows), block_rows)]

    def copy_in(j):
        b = j % EXPERT_ROW_BUFFERS
        return pltpu.make_async_copy(block_of(xs_hbm, j), xbuf.at[b], in_sem.at[b])

    def copy_out(j):
        b = j % EXPERT_ROW_BUFFERS
        return pltpu.make_async_copy(ybuf.at[b], block_of(y_hbm, j), out_sem.at[b])

    @pl.when(i == 0)
    def _():
        for j in range(EXPERT_ROW_BUFFERS - 1):
            pl.when(j < n)(lambda j=j: copy_in(j).start())

    ahead = i + EXPERT_ROW_BUFFERS - 1
    pl.when(ahead < n)(lambda: copy_in(ahead).start())

    @pl.when((i == 0) | (bexp_ref[i] != bexp_ref[jnp.maximum(i - 1, 0)]))
    def _():
        wg_b[...] = wg_ref[...].astype(BF16)
        wu_b[...] = wu_ref[...].astype(BF16)
        wd_b[...] = wd_ref[...].astype(BF16)

    copy_in(i).wait()
    behind = i - (EXPERT_ROW_BUFFERS - 1)
    pl.when(behind >= 0)(lambda: copy_out(behind).wait())

    @pl.when(bval_ref[i] > 0)
    def _():
        xb = _load_rows(xbuf.at[buf], MOE_BLOCK).astype(BF16)
        gate = jnp.dot(xb, wg_b[...], preferred_element_type=F32)
        up = jnp.dot(xb, wu_b[...], preferred_element_type=F32)
        hid = (gate * jax.nn.sigmoid(gate) * up).astype(BF16)
        _store_rows(ybuf.at[buf], jnp.dot(hid, wd_b[...], preferred_element_type=F32), MOE_BLOCK)

    @pl.when(bval_ref[i] == 0)
    def _():
        ybuf[buf] = jnp.zeros((block_rows, LANES), F32)

    copy_out(i).start()

    @pl.when(i == n - 1)
    def _():
        for back in range(EXPERT_ROW_BUFFERS - 1):
            pl.when(i - back >= 0)(lambda back=back: copy_out(i - back).wait())


def _routed_experts(xs, block_expert, block_valid, wg, wu, wd):
    n_blocks = block_expert.shape[0]
    shape = (EXPERT_ROW_BUFFERS, MOE_BLOCK * ROW_SUBLANES, LANES)
    grid_spec = pltpu.PrefetchScalarGridSpec(
        num_scalar_prefetch=2,
        grid=(n_blocks,),
        in_specs=[pl.BlockSpec(memory_space=pl.ANY),
                  pl.BlockSpec((None, D_MODEL, D_EXPERT), lambda i, be, bv: (be[i], 0, 0)),
                  pl.BlockSpec((None, D_MODEL, D_EXPERT), lambda i, be, bv: (be[i], 0, 0)),
                  pl.BlockSpec((None, D_EXPERT, D_MODEL), lambda i, be, bv: (be[i], 0, 0))],
        out_specs=pl.BlockSpec(memory_space=pl.ANY),
        scratch_shapes=[pltpu.VMEM((D_MODEL, D_EXPERT), BF16), pltpu.VMEM((D_MODEL, D_EXPERT), BF16),
                        pltpu.VMEM((D_EXPERT, D_MODEL), BF16),
                        pltpu.VMEM(shape, F32), pltpu.VMEM(shape, F32),
                        pltpu.SemaphoreType.DMA((EXPERT_ROW_BUFFERS,)),
                        pltpu.SemaphoreType.DMA((EXPERT_ROW_BUFFERS,))],
    )
    return pl.pallas_call(
        _expert_kernel,
        grid_spec=grid_spec,
        out_shape=jax.ShapeDtypeStruct(xs.shape, F32),
        name="routed_experts",
        compiler_params=_params("arbitrary"),
    )(block_expert, block_valid, xs, wg, wu, wd)


def _combine_kernel(dest_ref, dest_next_ref, x1_ref, route_ref, g_ref, yb_hbm, o_ref, ybuf, sem, *, n):
    i = pl.program_id(0)
    slot = i % 2

    def row_copy(idx8, r, k, s):
        dst = ybuf.at[s, pl.ds((k * ROW_TILE + r) * ROW_SUBLANES, ROW_SUBLANES)]
        return pltpu.make_async_copy(_row_tile(yb_hbm, idx8), dst, sem.at[s])

    @pl.when(i == 0)
    def _():
        def body(r, c):
            idx = [dest_ref[0, TOP_K * r + k] for k in range(TOP_K)]
            for k in range(TOP_K):
                row_copy(idx[k], r, k, 0).start()
            return c
        lax.fori_loop(0, ROW_TILE, body, 0)

    @pl.when(i + 1 < n)
    def _():
        for r0 in range(0, ROW_TILE, INDEX_BATCH):
            idx = [dest_next_ref[0, j] for j in range(TOP_K * r0, TOP_K * (r0 + INDEX_BATCH))]
            for r in range(r0, r0 + INDEX_BATCH):
                for k in range(TOP_K):
                    row_copy(idx[TOP_K * (r - r0) + k], r, k, 1 - slot).start(priority=k % DMA_THREADS)

    _tile_wait(yb_hbm, ybuf.at[slot], sem.at[slot])
    route = route_ref[...]
    buf = ybuf.at[slot]
    y = (_load_rows(buf, ROW_TILE) * route[:, 2:3]
         + _load_rows(buf, ROW_TILE, offset=ROW_TILE * ROW_SUBLANES) * route[:, 3:4])
    x2 = x1_ref[...] + y
    ms = jnp.mean(x2 * x2, axis=-1, keepdims=True)
    o_ref[...] = x2 * lax.rsqrt(ms + EPS) * g_ref[...]


def _combine_and_norm(dest8, x1, route, norm_w, yb):
    t = x1.shape[0]
    n_tiles = t // ROW_TILE
    dest_spec = lambda f: pl.BlockSpec((None, 1, TILE_ROW_COPIES), lambda i: (f(i), 0, 0), memory_space=pltpu.SMEM)
    return pl.pallas_call(
        functools.partial(_combine_kernel, n=n_tiles),
        grid=(n_tiles,),
        in_specs=[dest_spec(lambda i: i), dest_spec(lambda i: jnp.minimum(i + 1, n_tiles - 1)),
                  pl.BlockSpec((ROW_TILE, D_MODEL), lambda i: (i, 0)),
                  pl.BlockSpec((ROW_TILE, LANES), lambda i: (i, 0)),
                  _const_spec((1, D_MODEL)),
                  pl.BlockSpec(memory_space=pl.ANY)],
        out_specs=pl.BlockSpec((ROW_TILE, D_MODEL), lambda i: (i, 0)),
        out_shape=jax.ShapeDtypeStruct((t, D_MODEL), F32),
        scratch_shapes=[pltpu.VMEM((2, TILE_ROW_COPIES * ROW_SUBLANES, LANES), F32),
                        pltpu.SemaphoreType.DMA((2,))],
        name="combine_and_norm",
        compiler_params=_params("arbitrary", disable_bounds_checks=True),
    )(dest8, dest8, x1, route, norm_w, yb)


def _dispatch_plan(route, counts):
    t = route.shape[0]
    counts = counts[0, 0:N_EXPERTS].astype(jnp.int32)
    padded = (counts + MOE_BLOCK - 1) // MOE_BLOCK * MOE_BLOCK
    pad_end = jnp.cumsum(padded)
    pad_start = pad_end - padded
    expert = route[:, 0:TOP_K].astype(jnp.int32)
    rank = route[:, 4:4 + TOP_K].astype(jnp.int32)
    ids = jnp.arange(N_EXPERTS, dtype=jnp.int32)
    start_of = jnp.sum(jnp.where(expert[:, :, None] == ids, pad_start, 0), axis=-1)
    dest8 = ((start_of + rank) * ROW_SUBLANES).reshape(t // ROW_TILE, 1, TILE_ROW_COPIES)
    n_blocks = -(-(t * TOP_K + N_EXPERTS * (MOE_BLOCK - 1)) // MOE_BLOCK)
    block_start = jnp.arange(n_blocks, dtype=jnp.int32) * MOE_BLOCK
    block_expert = jnp.minimum(jnp.sum((pad_end[None, :] <= block_start[:, None]).astype(jnp.int32), axis=1),
                               N_EXPERTS - 1)
    block_valid = (block_start < pad_end[-1]).astype(jnp.int32)
    n_slots = n_blocks * MOE_BLOCK
    fill_start = jnp.concatenate([pad_start + counts, pad_end[-1:]])
    fill_n = jnp.concatenate([padded - counts, n_slots - pad_end[-1:]])
    return dest8, fill_start, fill_n, block_expert, block_valid, n_slots


def _rel_bias_band(rel_table, qc):
    rows = qc * CHUNK
    window = ATTN_REACH + rows
    n = window + rows - 1
    diag = rel_table[:, np.clip(ATTN_REACH + rows - 1 - np.arange(n), -MAX_REL, MAX_REL) + MAX_REL].astype(F32)
    shifted = jnp.tile(diag, (1, rows + 1))[:, :rows * (n + 1)].reshape(N_HEADS_A, rows, n + 1)[:, :, :window]
    bias = shifted[:, ::-1, :] * LOG2E
    i = np.arange(rows)[:, None]
    j = np.arange(window)[None, :]
    first = (i // CHUNK) * CHUNK
    return jnp.where((j >= first) & (j < first + BAND), bias, NEG_INF)


def _pad_lanes(v, width=LANES):
    return jnp.pad(v.astype(F32), (0, width - v.shape[0])).reshape(1, width)


def _mixers(x, kv_cache, conv_state, ssm_state, w):
    batch, l, _ = x.shape
    nc = l // CHUNK
    t = batch * l
    x2d = x.reshape(t, D_MODEL)
    proj, dt_raw = _in_projection(x2d, w["norm_mix"], w["w_proj"])
    proj3 = proj.reshape(batch, l, PROJ_WIDTH)
    k_new = lambda rows: proj3[:, rows, K0:K0 + ATTN_WIDTH].astype(F32)
    v_new = lambda rows: proj3[:, rows, V0:V0 + ATTN_WIDTH].astype(F32)
    if kv_cache is None:
        qc = ATTN_CHUNKS_PER_STEP if nc % ATTN_CHUNKS_PER_STEP == 0 else 1
        kv_spec = [pl.BlockSpec((l, ATTN_WIDTH), lambda b, c: (b, K0 // ATTN_WIDTH)),
                   pl.BlockSpec((l, ATTN_WIDTH), lambda b, c: (b, V0 // ATTN_WIDTH))]
        attn = _band_attention(proj, Q0 // ATTN_WIDTH, proj, proj, kv_spec, _rel_bias_band(w["rel_table"], qc),
                               batch=batch, nc=nc, lk=l, pad_rows=ATTN_REACH, qc=qc)
        keep = slice(l - min(ATTN_REACH, l), l)
        k_keep, v_keep = k_new(keep), v_new(keep)
    else:
        assert nc == 1
        cache_k, cache_v = kv_cache
        wlen = cache_k.shape[1]
        kk = jnp.concatenate([cache_k.reshape(batch, wlen, ATTN_WIDTH), k_new(slice(None))], axis=1)
        vv = jnp.concatenate([cache_v.reshape(batch, wlen, ATTN_WIDTH), v_new(slice(None))], axis=1)
        spec = pl.BlockSpec((None, wlen + l, ATTN_WIDTH), lambda b, c: (b, 0, 0))
        attn = _band_attention(proj, Q0 // ATTN_WIDTH, kk, vv, [spec, spec], _rel_bias_band(w["rel_table"], 1),
                               batch=batch, nc=nc, lk=wlen + l, pad_rows=BAND - (wlen + l), qc=1)
        k_keep, v_keep = kk[:, l:], vv[:, l:]
    tail = proj3[:, l - (CONV_WIDTH - 1):, XBC0:XBC0 + CONV_DIM].astype(F32)
    if conv_state is None:
        conv0 = st0 = None
        conv_new = tail
    else:
        conv0 = jnp.pad(conv_state, ((0, 0), (CARRY_ROWS - (CONV_WIDTH - 1), 0), (0, 0)))
        st0 = jnp.transpose(ssm_state.reshape(batch, D_INNER, D_STATE), (0, 2, 1))
        conv_new = jnp.concatenate([conv_state, tail], axis=1)[:, -(CONV_WIDTH - 1):]
    ssm, st_t = _ssd_mixer(proj, dt_raw, conv0, st0, w["conv_w"], w["conv_b"], w["dt_bias"], w["a_log"], w["d_exp"],
                           w["ssm_norm"], batch=batch, nc=nc)
    ssm_new = jnp.transpose(st_t, (0, 2, 1)).reshape(batch, SSM_HEADS, SSM_HEAD_DIM, D_STATE)
    heads = lambda u: u.reshape(batch, u.shape[1], N_HEADS_A, HEAD_DIM_A)
    return (x2d, attn, ssm, proj), (heads(k_keep), heads(v_keep), conv_new, ssm_new)


def kernel(x_prompt, x_sample, cache_attn_k, cache_attn_v, state_conv, state_ssm, norm_mix, w_in, rel_bias, conv_w, conv_b, dt_bias, a_log, d_skip, ssm_norm, w_branch_a, w_branch_b, w_out, norm_ffn, w_router_group, b_router_group, w_router_expert, b_router_expert, w_gate, w_up, w_down, norm_final):
    assert norm_mix.shape[0] == 1, "single-layer trunk"
    q, k, v, z, xbc, dtc, ga, gb = jnp.split(w_in[0], np.cumsum(
        (ATTN_WIDTH, ATTN_WIDTH, ATTN_WIDTH, D_INNER, CONV_DIM, SSM_HEADS, D_MODEL, D_MODEL))[:-1].tolist(), axis=1)
    w_proj = jnp.concatenate([z, ga, gb, xbc, q, k, v, jnp.pad(dtc, ((0, 0), (0, LANES - SSM_HEADS)))],
                             axis=1).astype(BF16)
    w_router = jnp.pad(jnp.concatenate([w_router_group[0], w_router_expert[0]], axis=1),
                       ((0, 0), (0, LANES - N_EXPERT_GROUPS - N_EXPERTS)))
    w_router_hi = w_router.astype(BF16)
    w_router_lo = (w_router - w_router_hi.astype(F32)).astype(BF16)
    b_router = _pad_lanes(jnp.concatenate([b_router_group[0], b_router_expert[0]]))
    w = dict(
        norm_mix=norm_mix[0].reshape(1, D_MODEL), w_proj=w_proj, rel_table=rel_bias[0],
        conv_w=conv_w[0], conv_b=conv_b[0].reshape(1, CONV_DIM), dt_bias=_pad_lanes(dt_bias[0]),
        a_log=_pad_lanes(a_log[0]), d_exp=jnp.repeat(d_skip[0], SSM_HEAD_DIM).reshape(1, D_INNER),
        ssm_norm=ssm_norm[0].reshape(1, D_INNER),
    )
    mix_w = (w_branch_a[0].astype(BF16), w_branch_b[0].astype(BF16), w_out[0].astype(BF16),
             norm_ffn[0].reshape(1, D_MODEL), jnp.concatenate([w_router_hi, w_router_lo], axis=1), b_router)
    norm_out = norm_final.reshape(1, D_MODEL)

    groups = [_mixers(x_prompt, None, None, None, w),
              _mixers(x_sample, (cache_attn_k[0], cache_attn_v[0]), state_conv[0], state_ssm[0], w)]
    counts = jnp.zeros((1, LANES), F32)
    mixed = []
    for acts, _ in groups:
        x1, hn, route, counts = _mix_and_route(*acts, *mix_w, counts)
        mixed.append((x1, hn, route))
    route_all = jnp.concatenate([m[2] for m in mixed], axis=0)
    dest8, fill_start, fill_n, block_expert, block_valid, n_slots = _dispatch_plan(route_all, counts)
    xs = _dispatch(dest8, fill_start, fill_n, [m[1] for m in mixed], n_slots)
    yb = _routed_experts(xs, block_expert, block_valid, w_gate[0], w_up[0], w_down[0])
    outs, first = [], 0
    for (x1, _, route), (acts, _) in zip(mixed, groups):
        n_tiles = x1.shape[0] // ROW_TILE
        y = _combine_and_norm(dest8[first:first + n_tiles], x1, route, norm_out, yb)
        first += n_tiles
        outs.append(y)
    (yp, ys), ((kp, vp, cp, sp), (ks, vs, cs, ss)) = outs, [g[1] for g in groups]
    yp = yp.reshape(x_prompt.shape)
    ys = ys.reshape(x_sample.shape)
    return (yp, ys, kp[None], vp[None], cp[None], sp[None], ks[None], vs[None], cs[None], ss[None])
```

```python
import functools
import math

import numpy as np
import jax
import jax.numpy as jnp
from jax import lax
from jax.experimental import pallas as pl
from jax.experimental.pallas import tpu as pltpu

F32 = jnp.float32
BF16 = jnp.bfloat16

D_MODEL = 1024
CHUNK = 64
LEFT_CHUNKS = 8
ATTN_REACH = LEFT_CHUNKS * CHUNK
BAND = ATTN_REACH + CHUNK
N_HEADS_A = 8
HEAD_DIM_A = 64
ATTN_WIDTH = N_HEADS_A * HEAD_DIM_A
MAX_REL = 256
ATTN_SCALE = 1.0 / math.sqrt(HEAD_DIM_A)
NEG_INF = -1e30
SSM_HEADS = 16
SSM_HEAD_DIM = 64
D_INNER = SSM_HEADS * SSM_HEAD_DIM
SSM_GROUPS = 2
GROUP_WIDTH = D_INNER // SSM_GROUPS
D_STATE = 128
CONV_WIDTH = 4
CONV_DIM = D_INNER + 2 * SSM_GROUPS * D_STATE
N_EXPERT_GROUPS = 4
EXPERTS_PER_GROUP = 8
N_EXPERTS = N_EXPERT_GROUPS * EXPERTS_PER_GROUP
TOP_K = 2
D_EXPERT = 512
MOE_BLOCK = 256
EPS = 1e-6

LANES = 128
SUBLANES = 8
Z0, GA0, GB0, XBC0, Q0, K0, V0 = 0, 1024, 2048, 3072, 4608, 5120, 5632
PROJ_WIDTH = V0 + ATTN_WIDTH
PROJ_TILE = 1024
ROW_TILE = 256
ATTN_CHUNKS_PER_STEP = 2
SSD_STREAMS = 4
VMEM_LIMIT = 56 * 1024 * 1024

_NT = (((1,), (1,)), ((), ()))


def _params(*sem, **kw):
    return pltpu.CompilerParams(dimension_semantics=sem, vmem_limit_bytes=VMEM_LIMIT, **kw)


def _const_spec(shape):
    nd = len(shape)
    return pl.BlockSpec(shape, lambda *_: (0,) * nd)


def _inproj_kernel(x_ref, g_ref, w_ref, o_ref, dt_ref):
    x = x_ref[...]
    ms = jnp.mean(x * x, axis=-1, keepdims=True)
    h = (x * lax.rsqrt(ms + EPS) * g_ref[...]).astype(BF16)
    for j in range(0, PROJ_WIDTH, PROJ_TILE):
        o_ref[:, j:j + PROJ_TILE] = jnp.dot(h, w_ref[:, j:j + PROJ_TILE], preferred_element_type=F32).astype(BF16)
    dt_ref[...] = jnp.dot(h, w_ref[:, PROJ_WIDTH:PROJ_WIDTH + LANES], preferred_element_type=F32)


def _in_projection(x2d, norm_w, w_proj):
    t = x2d.shape[0]
    return pl.pallas_call(
        _inproj_kernel,
        grid=(t // ROW_TILE,),
        in_specs=[pl.BlockSpec((ROW_TILE, D_MODEL), lambda i: (i, 0)),
                  _const_spec((1, D_MODEL)),
                  _const_spec((D_MODEL, PROJ_WIDTH + LANES))],
        out_specs=[pl.BlockSpec((ROW_TILE, PROJ_WIDTH), lambda i: (i, 0)),
                   pl.BlockSpec((ROW_TILE, LANES), lambda i: (i, 0))],
        out_shape=[jax.ShapeDtypeStruct((t, PROJ_WIDTH), BF16), jax.ShapeDtypeStruct((t, LANES), F32)],
        name="in_projection",
        compiler_params=_params("parallel"),
    )(x2d, norm_w, w_proj)


LOG2E = math.log2(math.e)
ATTN_STAGES = 3


def _attn_kernel(q_ref, k_ref, v_ref, bias_ref, o_ref, kpad, vpad, *, pad_rows, lk, qc):
    rows = qc * CHUNK
    window = ATTN_REACH + rows
    c = pl.program_id(1)

    @pl.when(c == 0)
    def _():
        if pad_rows:
            kpad[0:pad_rows, :] = jnp.zeros((pad_rows, ATTN_WIDTH), BF16)
            vpad[0:pad_rows, :] = jnp.zeros((pad_rows, ATTN_WIDTH), BF16)
        kpad[pad_rows:pad_rows + lk, :] = k_ref[...].astype(BF16)
        vpad[pad_rows:pad_rows + lk, :] = v_ref[...].astype(BF16)

    start = pl.multiple_of(c * rows, rows)

    def body(mask_start):
        kw = kpad[pl.ds(start, window), :]
        vw = vpad[pl.ds(start, window), :]
        q = (q_ref[...].astype(F32) * (ATTN_SCALE * LOG2E)).astype(BF16)
        low = lax.broadcasted_iota(jnp.int32, (rows, LANES), 1) < HEAD_DIM_A
        if mask_start:
            valid = lax.broadcasted_iota(jnp.int32, (rows, window), 1) + start >= pad_rows
        outs = {}

        def head(h):
            hp, sub = divmod(h, 2)
            sl = slice(LANES * hp, LANES * (hp + 1))
            q2 = q[:, sl]
            qm = jnp.where(low if sub == 0 else jnp.logical_not(low), q2, jnp.zeros_like(q2))
            s = lax.dot_general(qm, kw[:, sl], _NT, preferred_element_type=F32) + bias_ref[h]
            if mask_start:
                s = jnp.where(valid, s, NEG_INF)
            yield
            m = jnp.max(s, axis=-1, keepdims=True)
            p = jnp.exp2(s - m)
            l = jnp.sum(p, axis=-1, keepdims=True)
            pb = p.astype(BF16)
            yield
            outs[h] = jnp.dot(pb, vw[:, sl], preferred_element_type=F32) / l
            if sub == 1:
                o_ref[:, sl] = jnp.where(low, outs[h - 1], outs[h]).astype(o_ref.dtype)

        heads = [head(h) for h in range(N_HEADS_A)]
        for t in range(N_HEADS_A + ATTN_STAGES - 1):
            for h in range(min(t, N_HEADS_A - 1), max(t - ATTN_STAGES, -1), -1):
                next(heads[h], None)

    if pad_rows:
        pl.when(start < pad_rows)(lambda: body(True))
        pl.when(start >= pad_rows)(lambda: body(False))
    else:
        body(False)


def _band_attention(q_arr, q_col, k_arr, v_arr, kv_spec, bias, *, batch, nc, lk, pad_rows, qc):
    steps = nc // qc
    rows = qc * CHUNK
    kern = functools.partial(_attn_kernel, pad_rows=pad_rows, lk=lk, qc=qc)
    return pl.pallas_call(
        kern,
        grid=(batch, steps),
        in_specs=[pl.BlockSpec((rows, ATTN_WIDTH), lambda b, c: (b * steps + c, q_col)),
                  kv_spec[0], kv_spec[1],
                  _const_spec((N_HEADS_A, rows, ATTN_REACH + rows))],
        out_specs=pl.BlockSpec((rows, ATTN_WIDTH), lambda b, c: (b * steps + c, 0)),
        out_shape=jax.ShapeDtypeStruct((batch * nc * CHUNK, ATTN_WIDTH), BF16),
        scratch_shapes=[pltpu.VMEM((pad_rows + lk, ATTN_WIDTH), BF16),
                        pltpu.VMEM((pad_rows + lk, ATTN_WIDTH), BF16)],
        name="band_attention",
        compiler_params=_params("parallel", "arbitrary"),
    )(q_arr, k_arr, v_arr, bias)


CARRY_ROWS = SUBLANES
SHIFT_TAPS = CONV_WIDTH - 1
SHIFT_K = 2 * CARRY_ROWS + CHUNK


def _split_bf16(x, parts):
    out = []
    for _ in range(parts - 1):
        h = x.astype(BF16)
        out.append(h)
        x = x - h.astype(F32)
    out.append(x.astype(BF16))
    return out


def _ssd_kernel(xbc_ref, z_ref, dt_ref, conv0_ref, st0_ref, cw_ref, cb_ref, dtb_ref, alog_ref,
                dexp_ref, nw_ref, tri_ref, exp_ref, sel_ref, eye_ref, shift_ref,
                y_ref, stout_ref, carry, st, *, has_init, streams, nc):
    c = pl.program_id(1)

    @pl.when(c == 0)
    def _():
        if has_init:
            carry[...] = conv0_ref[...]
            st[...] = st0_ref[...]
        else:
            carry[...] = jnp.zeros((streams, CARRY_ROWS, CONV_DIM), F32)
            st[...] = jnp.zeros((streams, D_STATE, D_INNER), F32)

    chunks = [_ssd_chunk(xbc_ref.at[s], z_ref.at[s], dt_ref.at[s], cw_ref, cb_ref, dtb_ref, alog_ref, dexp_ref,
                         nw_ref, tri_ref, exp_ref, sel_ref, eye_ref, shift_ref, y_ref.at[s], carry.at[s], st.at[s])
              for s in range(streams)]
    while chunks:
        chunks = [g for g in chunks if next(g, True) is None]

    @pl.when(c == nc - 1)
    def _():
        stout_ref[...] = st[...]


def _ssd_chunk(xbc_ref, z_ref, dt_ref, cw_ref, cb_ref, dtb_ref, alog_ref, dexp_ref, nw_ref,
               tri_ref, exp_ref, sel_ref, eye_ref, shift_ref, y_ref, carry, st):
    x_f = xbc_ref[...].astype(F32)
    prev = carry[...]
    prev_hi = prev.astype(BF16).astype(F32)
    stacked = jnp.concatenate([prev_hi, prev - prev_hi, x_f], axis=0).astype(BF16)
    taps = jnp.dot(shift_ref[...], stacked, preferred_element_type=F32)

    dt_in = dt_ref[...] + dtb_ref[...]
    dt = jnp.maximum(dt_in, 0.0) + jnp.log1p(jnp.exp(-jnp.abs(dt_in)))
    da = dt * (-jnp.exp(alog_ref[...]))
    tri = tri_ref[...]
    cs3 = jnp.dot(tri, jnp.concatenate(_split_bf16(da, 3), axis=1), preferred_element_type=F32)
    yield

    conv = cb_ref[...] + x_f * cw_ref[CONV_WIDTH - 1:CONV_WIDTH, :]
    for j in range(SHIFT_TAPS):
        conv = conv + taps[j * CHUNK:(j + 1) * CHUNK, :] * cw_ref[j:j + 1, :]
    u = conv * jax.nn.sigmoid(conv)
    xs = u[:, 0:D_INNER]
    cs = cs3[:, 0:LANES] + cs3[:, LANES:2 * LANES] + cs3[:, 2 * LANES:3 * LANES]
    cs_last = cs[CHUNK - 1:CHUNK, :]
    ecs = jnp.exp(cs)
    to_end = jnp.exp(cs_last - cs)

    stack = jnp.concatenate(_split_bf16(dt, 2) + _split_bf16(ecs, 2) + _split_bf16(to_end, 2)
                            + _split_bf16(cs, 3), axis=0)
    wide = jnp.dot(stack, exp_ref[...], preferred_element_type=F32)

    lane = lax.broadcasted_iota(jnp.int32, (CHUNK, LANES), 1)
    even = (lane % 2) == 0
    csm = jnp.concatenate([jnp.where(even, cs, 0.0), jnp.where(even, 0.0, cs)], axis=0)
    srow3 = lax.dot_general(sel_ref[...], jnp.concatenate(_split_bf16(csm, 3), axis=0), _NT,
                            preferred_element_type=F32)
    yield

    rows = [wide[i * CHUNK:(i + 1) * CHUNK, :] for i in range(9)]
    dt_w = rows[0] + rows[1]
    ecs_w = rows[2] + rows[3]
    to_end_w = rows[4] + rows[5]
    cs_w = rows[6] + rows[7] + rows[8]
    srow = srow3[:, 0:LANES] + srow3[:, LANES:2 * LANES] + srow3[:, 2 * LANES:3 * LANES]

    xdt = xs * dt_w
    xdt_b = xdt.astype(BF16)
    xw_b = (xdt * to_end_w).astype(BF16)
    low = lane < SSM_HEAD_DIM
    t_idx = lax.broadcasted_iota(jnp.int32, (CHUNK, LANES), 0)
    causal = t_idx >= (lane % SSM_HEAD_DIM)
    eye = eye_ref[...]

    y_parts = []
    for g in range(SSM_GROUPS):
        b_g = u[:, D_INNER + g * D_STATE:D_INNER + (g + 1) * D_STATE].astype(BF16)
        c_g = u[:, D_INNER + (SSM_GROUPS + g) * D_STATE:D_INNER + (SSM_GROUPS + g + 1) * D_STATE].astype(BF16)
        gsl = slice(g * GROUP_WIDTH, (g + 1) * GROUP_WIDTH)
        st_g = st[:, gsl]
        y_off = jnp.dot(c_g, st_g.astype(BF16), preferred_element_type=F32) * ecs_w[:, gsl]
        b_t = lax.dot_general(eye, b_g, _NT, preferred_element_type=F32).astype(BF16)
        st[:, gsl] = st_g * ecs_w[CHUNK - 1:CHUNK, gsl] + jnp.dot(b_t, xw_b[:, gsl], preferred_element_type=F32)
        cb2 = lax.dot_general(c_g, jnp.concatenate([b_g, b_g], axis=0), _NT, preferred_element_type=F32)
        yield
        for kk in range(SSM_HEADS // SSM_GROUPS // 2):
            k = g * (SSM_HEADS // SSM_GROUPS // 2) + kk
            psl = slice(k * LANES, (k + 1) * LANES)
            decay = jnp.where(causal, jnp.exp(cs_w[:, psl] - srow[k:k + 1, :]), 0.0)
            gmat = (cb2 * decay).astype(BF16)
            xp = xdt_b[:, psl]
            xblk = jnp.concatenate([jnp.where(low, xp, jnp.zeros_like(xp)),
                                    jnp.where(low, jnp.zeros_like(xp), xp)], axis=0)
            y_parts.append(jnp.dot(gmat, xblk, preferred_element_type=F32)
                           + y_off[:, kk * LANES:(kk + 1) * LANES])
        yield
    carry[...] = x_f[CHUNK - CARRY_ROWS:CHUNK, :]
    y = jnp.concatenate(y_parts, axis=1) + xs * dexp_ref[...]
    zz = z_ref[...].astype(F32)
    y = y * (zz * jax.nn.sigmoid(zz))
    outs = []
    for g in range(SSM_GROUPS):
        yg = y[:, g * GROUP_WIDTH:(g + 1) * GROUP_WIDTH]
        outs.append(yg * lax.rsqrt(jnp.mean(yg * yg, axis=-1, keepdims=True) + EPS))
    y_ref[...] = (jnp.concatenate(outs, axis=1) * nw_ref[...]).astype(y_ref.dtype)


def _ssd_constants():
    tri = np.tril(np.ones((CHUNK, CHUNK), np.float32))
    expand = np.zeros((LANES, D_INNER), np.float32)
    for e in range(SSM_HEADS):
        expand[e, e * SSM_HEAD_DIM:(e + 1) * SSM_HEAD_DIM] = 1.0
    sel = np.zeros((16, LANES), np.float32)
    for e in range(SSM_HEADS):
        sel[e // 2, e] = 1.0
    eye = np.eye(D_STATE, dtype=np.float32)
    shift = np.zeros((SHIFT_TAPS * CHUNK, SHIFT_K), np.float32)
    for j in range(SHIFT_TAPS):
        for t in range(CHUNK):
            m = t - (CONV_WIDTH - 1) + j
            if m >= 0:
                shift[j * CHUNK + t, 2 * CARRY_ROWS + m] = 1.0
            else:
                shift[j * CHUNK + t, CARRY_ROWS + m] = 1.0
                shift[j * CHUNK + t, 2 * CARRY_ROWS + m] = 1.0
    return tuple(jnp.asarray(a, BF16) for a in (tri, expand, sel, eye, shift))


def _ssd_mixer(proj, dt_raw, conv0, st0, conv_w, conv_b, dt_bias_p, a_log_p, d_exp, norm_w, *, batch, nc):
    streams = SSD_STREAMS if batch % SSD_STREAMS == 0 else 1
    groups = batch // streams
    l = nc * CHUNK
    has_init = conv0 is not None
    if not has_init:
        conv0 = jnp.zeros((1, streams, CARRY_ROWS, CONV_DIM), F32)
        st0 = jnp.zeros((1, streams, D_STATE, D_INNER), F32)
        init_map = lambda b, c: (0, 0, 0, 0)
    else:
        conv0 = conv0.reshape(groups, streams, CARRY_ROWS, CONV_DIM)
        st0 = st0.reshape(groups, streams, D_STATE, D_INNER)
        init_map = lambda b, c: (b, 0, 0, 0)
    tri, expand, sel, eye, shift = _ssd_constants()
    proj4 = proj.reshape(groups, streams, l, PROJ_WIDTH)
    dt4 = dt_raw.reshape(groups, streams, l, LANES)
    chunk_of = lambda w, col: pl.BlockSpec((None, streams, CHUNK, w), lambda b, c: (b, 0, c, col))
    state_spec = lambda m: pl.BlockSpec((None, streams, D_STATE, D_INNER), m)
    kern = functools.partial(_ssd_kernel, has_init=has_init, streams=streams, nc=nc)
    y, st_out = pl.pallas_call(
        kern,
        grid=(groups, nc),
        in_specs=[chunk_of(CONV_DIM, XBC0 // CONV_DIM), chunk_of(D_INNER, Z0 // D_INNER), chunk_of(LANES, 0),
                  pl.BlockSpec((None, streams, CARRY_ROWS, CONV_DIM), init_map), state_spec(init_map),
                  _const_spec((CONV_WIDTH, CONV_DIM)), _const_spec((1, CONV_DIM)),
                  _const_spec((1, LANES)), _const_spec((1, LANES)),
                  _const_spec((1, D_INNER)), _const_spec((1, D_INNER)),
                  _const_spec((CHUNK, CHUNK)), _const_spec((LANES, D_INNER)),
                  _const_spec((16, LANES)), _const_spec((D_STATE, D_STATE)),
                  _const_spec((SHIFT_TAPS * CHUNK, SHIFT_K))],
        out_specs=[chunk_of(D_INNER, 0), state_spec(lambda b, c: (b, 0, 0, 0))],
        out_shape=[jax.ShapeDtypeStruct((groups, streams, l, D_INNER), BF16),
                   jax.ShapeDtypeStruct((groups, streams, D_STATE, D_INNER), F32)],
        scratch_shapes=[pltpu.VMEM((streams, CARRY_ROWS, CONV_DIM), F32),
                        pltpu.VMEM((streams, D_STATE, D_INNER), F32)],
        name="ssd_mixer",
        compiler_params=_params("parallel", "arbitrary"),
    )(proj4, proj4, dt4, conv0, st0, conv_w, conv_b, dt_bias_p, a_log_p, d_exp, norm_w, tri, expand, sel, eye, shift)
    return y.reshape(batch * l, D_INNER), st_out.reshape(batch, D_STATE, D_INNER)


ROW_SUBLANES = D_MODEL // LANES
assert ROW_SUBLANES == SUBLANES


def _store_rows(ref, x, rows, offset=0):
    for s in range(ROW_SUBLANES):
        ref[pl.ds(offset + s, rows, stride=ROW_SUBLANES), :] = x[:, s * LANES:(s + 1) * LANES]


def _load_rows(ref, rows, offset=0):
    return jnp.concatenate([ref[pl.ds(offset + s, rows, stride=ROW_SUBLANES), :] for s in range(ROW_SUBLANES)],
                           axis=1)


def _row_tile(ref, idx8):
    return ref.at[pl.ds(pl.multiple_of(idx8, ROW_SUBLANES), ROW_SUBLANES)]


def _mix_kernel(x_ref, attn_ref, ssm_ref, ga_ref, gb_ref, wa_ref, wb_ref, wo_ref, g_ref, wr_ref, br_ref, slt_ref,
                count0_ref, x1_ref, hn_ref, route_ref, route_t_ref, counts_ref, carry, *, subtiles):
    @pl.when(pl.program_id(0) == 0)
    def _():
        carry[...] = count0_ref[...]

    running = {"counts": carry[...]}
    tiles = [_mix_tile(j * ROW_TILE, running, x_ref, attn_ref, ssm_ref, ga_ref, gb_ref, wa_ref, wb_ref, wo_ref, g_ref,
                       wr_ref, br_ref, slt_ref, x1_ref, hn_ref, route_ref, route_t_ref) for j in range(subtiles)]
    while tiles:
        tiles = [g for g in tiles if next(g, True) is None]
    carry[...] = running["counts"]
    counts_ref[...] = running["counts"]


def _mix_tile(r0, running, x_ref, attn_ref, ssm_ref, ga_ref, gb_ref, wa_ref, wb_ref, wo_ref, g_ref, wr_ref, br_ref,
              slt_ref, x1_ref, hn_ref, route_ref, route_t_ref):
    rows = slice(r0, r0 + ROW_TILE)
    a = jnp.dot(attn_ref[rows, :], wa_ref[...], preferred_element_type=F32)
    s = jnp.dot(ssm_ref[rows, :], wb_ref[...], preferred_element_type=F32)
    yield
    mixed = jax.nn.sigmoid(ga_ref[rows, :].astype(F32)) * a + jax.nn.sigmoid(gb_ref[rows, :].astype(F32)) * s
    x1 = x_ref[rows, :] + jnp.dot(mixed.astype(BF16), wo_ref[...], preferred_element_type=F32)
    yield
    x1_ref[rows, :] = x1
    ms = jnp.mean(x1 * x1, axis=-1, keepdims=True)
    hn = x1 * lax.rsqrt(ms + EPS) * g_ref[...]
    _store_rows(hn_ref, hn, ROW_TILE, offset=r0 * ROW_SUBLANES)

    hi = hn.astype(BF16)
    lo = (hn - hi.astype(F32)).astype(BF16)
    both = jnp.dot(hi, wr_ref[...], preferred_element_type=F32)
    lo_part = jnp.dot(lo, wr_ref[:, 0:LANES], preferred_element_type=F32)
    yield
    logits = both[:, 0:LANES] + both[:, LANES:2 * LANES] + lo_part + br_ref[...]
    lane = lax.broadcasted_iota(jnp.int32, logits.shape, 1)
    lane_f = lane.astype(F32)
    big = float(LANES)
    is_g = lane < N_EXPERT_GROUPS
    gl = jnp.where(is_g, logits, -jnp.inf)
    gmax = jnp.max(gl, axis=-1, keepdims=True)
    gidx = jnp.min(jnp.where(gl == gmax, lane_f, big), axis=-1, keepdims=True)
    gsum = jnp.sum(jnp.where(is_g, jnp.exp(gl - gmax), 0.0), axis=-1, keepdims=True)
    g_top = 1.0 / gsum
    first = N_EXPERT_GROUPS + gidx * EXPERTS_PER_GROUP
    in_group = (lane_f >= first) & (lane_f < first + EXPERTS_PER_GROUP)
    el = jnp.where(in_group, logits, -jnp.inf)
    m1 = jnp.max(el, axis=-1, keepdims=True)
    i1 = jnp.min(jnp.where(el == m1, lane_f, big), axis=-1, keepdims=True)
    el2 = jnp.where(lane_f == i1, -jnp.inf, el)
    m2 = jnp.max(el2, axis=-1, keepdims=True)
    i2 = jnp.min(jnp.where(el2 == m2, lane_f, big), axis=-1, keepdims=True)
    r = jnp.exp(m2 - m1)
    w1 = g_top / (1.0 + r)
    w2 = g_top * r / (1.0 + r)
    e1 = i1 - N_EXPERT_GROUPS
    e2 = i2 - N_EXPERT_GROUPS

    oh1 = lane_f == e1
    oh2 = lane_f == e2
    hot1 = jnp.where(oh1, 1.0, 0.0)
    hot2 = jnp.where(oh2, 1.0, 0.0)
    onehots = jnp.concatenate([hot1, hot2], axis=1).astype(BF16)
    before = jnp.dot(slt_ref[...], onehots, preferred_element_type=F32)
    cnt1 = jnp.sum(hot1, axis=0, keepdims=True)
    cnt2 = jnp.sum(hot2, axis=0, keepdims=True)
    yield
    base = running["counts"]
    rank1 = jnp.sum(jnp.where(oh1, before[:, 0:LANES] + base, 0.0), axis=-1, keepdims=True)
    rank2 = jnp.sum(jnp.where(oh2, before[:, LANES:2 * LANES] + (base + cnt1), 0.0), axis=-1, keepdims=True)
    running["counts"] = base + cnt1 + cnt2

    route = jnp.zeros_like(logits)
    for col, val in enumerate((e1, e2, w1, w2, rank1, rank2)):
        route = jnp.where(lane == col, val, route)
    route_ref[rows, :] = route
    route_t_ref[:, rows] = jnp.transpose(route)[0:SUBLANES, :]


MIX_SUBTILES = 2


def _mix_and_route(x2d, attn, ssm, proj, wa, wb, wo, norm_w, w_router, b_router, count0):
    t = x2d.shape[0]
    subtiles = MIX_SUBTILES if t % (MIX_SUBTILES * ROW_TILE) == 0 else 1
    step_rows = subtiles * ROW_TILE
    tile = lambda w, col: pl.BlockSpec((step_rows, w), lambda i: (i, col))
    slt = jnp.asarray(np.tril(np.ones((ROW_TILE, ROW_TILE), np.float32), k=-1), BF16)
    return pl.pallas_call(
        functools.partial(_mix_kernel, subtiles=subtiles),
        grid=(t // step_rows,),
        in_specs=[tile(D_MODEL, 0), tile(ATTN_WIDTH, 0), tile(D_INNER, 0),
                  tile(D_MODEL, GA0 // D_MODEL), tile(D_MODEL, GB0 // D_MODEL),
                  _const_spec((ATTN_WIDTH, D_MODEL)), _const_spec((D_INNER, D_MODEL)),
                  _const_spec((D_MODEL, D_MODEL)), _const_spec((1, D_MODEL)),
                  _const_spec((D_MODEL, 2 * LANES)), _const_spec((1, LANES)),
                  _const_spec((ROW_TILE, ROW_TILE)), _const_spec((1, LANES))],
        out_specs=[tile(D_MODEL, 0), pl.BlockSpec((step_rows * ROW_SUBLANES, LANES), lambda i: (i, 0)),
                   tile(LANES, 0), pl.BlockSpec((SUBLANES, step_rows), lambda i: (0, i)), _const_spec((1, LANES))],
        out_shape=[jax.ShapeDtypeStruct((t, D_MODEL), F32),
                   jax.ShapeDtypeStruct((t * ROW_SUBLANES, LANES), F32),
                   jax.ShapeDtypeStruct((t, LANES), F32),
                   jax.ShapeDtypeStruct((SUBLANES, t), F32),
                   jax.ShapeDtypeStruct((1, LANES), F32)],
        scratch_shapes=[pltpu.VMEM((1, LANES), F32)],
        name="mix_and_route",
        compiler_params=_params("arbitrary"),
    )(x2d, attn, ssm, proj, proj, wa, wb, wo, norm_w, w_router, b_router, slt, count0)


TILE_ROW_COPIES = ROW_TILE * TOP_K
INDEX_BATCH = 8
DISPATCH_BUFFERS = 3
DMA_THREADS = 2


def _tile_wait(ref_a, ref_b, sem):
    n = TILE_ROW_COPIES * ROW_SUBLANES
    pltpu.make_async_copy(ref_a.at[pl.ds(0, n)], ref_b.at[pl.ds(0, n)], sem).wait()


def _dispatch_kernel(fill_start_ref, fill_n_ref, dest_ref, *refs, tiles_per_source):
    n_src = len(tiles_per_source)
    sources = refs[:n_src]
    xs_hbm, hbuf, zbuf, in_sem, out_sem, fill_sem = refs[n_src:]
    i = pl.program_id(0)
    n = pl.num_programs(0)
    tile_rows = ROW_TILE * ROW_SUBLANES

    def tile_copy(src, j, b):
        return pltpu.make_async_copy(src.at[pl.ds(pl.multiple_of(j * tile_rows, tile_rows), tile_rows)],
                                     hbuf.at[b], in_sem.at[b])

    def tile_in_start(j):
        b = j % DISPATCH_BUFFERS
        first = 0
        for src, count in zip(sources, tiles_per_source):
            pl.when((j >= first) & (j < first + count))(lambda src=src, first=first: tile_copy(src, j - first, b).start())
            first += count

    @pl.when(i == 0)
    def _():
        zbuf[...] = jnp.zeros((ROW_SUBLANES, LANES), F32)
        tile_in_start(i)
        pl.when(n > 1)(lambda: tile_in_start(i + 1))

    buf = i % DISPATCH_BUFFERS
    tile_copy(sources[0], 0, buf).wait()
    for r0 in range(0, ROW_TILE, INDEX_BATCH):
        dests = [[dest_ref[0, k * ROW_TILE + r] for r in range(r0, r0 + INDEX_BATCH)] for k in range(TOP_K)]
        for r in range(r0, r0 + INDEX_BATCH):
            src = hbuf.at[buf, pl.ds(r * ROW_SUBLANES, ROW_SUBLANES)]
            for k in range(TOP_K):
                dst = _row_tile(xs_hbm, dests[k][r - r0])
                pltpu.make_async_copy(src, dst, out_sem.at[buf]).start(priority=k % DMA_THREADS)

    @pl.when(i > 0)
    def _():
        _tile_wait(xs_hbm, xs_hbm, out_sem.at[(i - 1) % DISPATCH_BUFFERS])

    pl.when(i + 2 < n)(lambda: tile_in_start(i + 2))

    @pl.when(i == n - 1)
    def _():
        _tile_wait(xs_hbm, xs_hbm, out_sem.at[buf])

        def per_range(e, carry):
            first = fill_start_ref[e]
            count = fill_n_ref[e]

            def start(j, c):
                pltpu.make_async_copy(zbuf, _row_tile(xs_hbm, (first + j) * ROW_SUBLANES), fill_sem.at[0]).start()
                return c

            def wait(j, c):
                pltpu.make_async_copy(zbuf, _row_tile(xs_hbm, 0), fill_sem.at[0]).wait()
                return c

            lax.fori_loop(0, count, start, 0)
            lax.fori_loop(0, count, wait, 0)
            return carry

        lax.fori_loop(0, N_EXPERTS + 1, per_range, 0)


def _dispatch(dest8, fill_start, fill_n, hn_list, n_slots):
    n_tiles = dest8.shape[0]
    tiles_per_source = tuple(h.shape[0] // (ROW_TILE * ROW_SUBLANES) for h in hn_list)
    assert sum(tiles_per_source) == n_tiles
    grid_spec = pltpu.PrefetchScalarGridSpec(
        num_scalar_prefetch=2,
        grid=(n_tiles,),
        in_specs=[pl.BlockSpec((None, 1, TILE_ROW_COPIES), lambda i, fs, fn: (i, 0, 0), memory_space=pltpu.SMEM)]
        + [pl.BlockSpec(memory_space=pl.ANY)] * len(hn_list),
        out_specs=pl.BlockSpec(memory_space=pl.ANY),
        scratch_shapes=[pltpu.VMEM((DISPATCH_BUFFERS, ROW_TILE * ROW_SUBLANES, LANES), F32),
                        pltpu.VMEM((ROW_SUBLANES, LANES), F32),
                        pltpu.SemaphoreType.DMA((DISPATCH_BUFFERS,)), pltpu.SemaphoreType.DMA((DISPATCH_BUFFERS,)),
                        pltpu.SemaphoreType.DMA((1,))],
    )
    return pl.pallas_call(
        functools.partial(_dispatch_kernel, tiles_per_source=tiles_per_source),
        grid_spec=grid_spec,
        out_shape=jax.ShapeDtypeStruct((n_slots * ROW_SUBLANES, LANES), F32),
        name="dispatch_rows",
        compiler_params=_params("arbitrary", disable_bounds_checks=True),
    )(fill_start, fill_n, dest8, *hn_list)


EXPERT_ROW_BUFFERS = 3


def _expert_kernel(bexp_ref, bval_ref, xs_hbm, wg_ref, wu_ref, wd_ref, y_hbm, wg_b, wu_b, wd_b,
                   xbuf, ybuf, in_sem, out_sem):
    i = pl.program_id(0)
    n = pl.num_programs(0)
    block_rows = MOE_BLOCK * ROW_SUBLANES
    buf = i % EXPERT_ROW_BUFFERS

    def block_of(ref, j):
        return ref.at[pl.ds(pl.multiple_of(j * block_rows, block_rows), block_rows)]

    def copy_in(j):
        b = j % EXPERT_ROW_BUFFERS
        return pltpu.make_async_copy(block_of(xs_hbm, j), xbuf.at[b], in_sem.at[b])

    def copy_out(j):
        b = j % EXPERT_ROW_BUFFERS
        return pltpu.make_async_copy(ybuf.at[b], block_of(y_hbm, j), out_sem.at[b])

    @pl.when(i == 0)
    def _():
        for j in range(EXPERT_ROW_BUFFERS - 1):
            pl.when(j < n)(lambda j=j: copy_in(j).start())

    ahead = i + EXPERT_ROW_BUFFERS - 1
    pl.when(ahead < n)(lambda: copy_in(ahead).start())

    @pl.when((i == 0) | (bexp_ref[i] != bexp_ref[jnp.maximum(i - 1, 0)]))
    def _():
        wg_b[...] = wg_ref[...].astype(BF16)
        wu_b[...] = wu_ref[...].astype(BF16)
        wd_b[...] = wd_ref[...].astype(BF16)

    copy_in(i).wait()
    behind = i - (EXPERT_ROW_BUFFERS - 1)
    pl.when(behind >= 0)(lambda: copy_out(behind).wait())

    @pl.when(bval_ref[i] > 0)
    def _():
        xb = _load_rows(xbuf.at[buf], MOE_BLOCK).astype(BF16)
        gate = jnp.dot(xb, wg_b[...], preferred_element_type=F32)
        up = jnp.dot(xb, wu_b[...], preferred_element_type=F32)
        hid = (gate * jax.nn.sigmoid(gate) * up).astype(BF16)
        _store_rows(ybuf.at[buf], jnp.dot(hid, wd_b[...], preferred_element_type=F32), MOE_BLOCK)

    @pl.when(bval_ref[i] == 0)
    def _():
        ybuf[buf] = jnp.zeros((block_rows, LANES), F32)

    copy_out(i).start()

    @pl.when(i == n - 1)
    def _():
        for back in range(EXPERT_ROW_BUFFERS - 1):
            pl.when(i - back >= 0)(lambda back=back: copy_out(i - back).wait())


def _routed_experts(xs, block_expert, block_valid, wg, wu, wd):
    n_blocks = block_expert.shape[0]
    shape = (EXPERT_ROW_BUFFERS, MOE_BLOCK * ROW_SUBLANES, LANES)
    grid_spec = pltpu.PrefetchScalarGridSpec(
        num_scalar_prefetch=2,
        grid=(n_blocks,),
        in_specs=[pl.BlockSpec(memory_space=pl.ANY),
                  pl.BlockSpec((None, D_MODEL, D_EXPERT), lambda i, be, bv: (be[i], 0, 0)),
                  pl.BlockSpec((None, D_MODEL, D_EXPERT), lambda i, be, bv: (be[i], 0, 0)),
                  pl.BlockSpec((None, D_EXPERT, D_MODEL), lambda i, be, bv: (be[i], 0, 0))],
        out_specs=pl.BlockSpec(memory_space=pl.ANY),
        scratch_shapes=[pltpu.VMEM((D_MODEL, D_EXPERT), BF16), pltpu.VMEM((D_MODEL, D_EXPERT), BF16),
                        pltpu.VMEM((D_EXPERT, D_MODEL), BF16),
                        pltpu.VMEM(shape, F32), pltpu.VMEM(shape, F32),
                        pltpu.SemaphoreType.DMA((EXPERT_ROW_BUFFERS,)),
                        pltpu.SemaphoreType.DMA((EXPERT_ROW_BUFFERS,))],
    )
    return pl.pallas_call(
        _expert_kernel,
        grid_spec=grid_spec,
        out_shape=jax.ShapeDtypeStruct(xs.shape, F32),
        name="routed_experts",
        compiler_params=_params("arbitrary"),
    )(block_expert, block_valid, xs, wg, wu, wd)


def _combine_kernel(dest_ref, dest_next_ref, x1_ref, route_ref, g_ref, yb_hbm, o_ref, ybuf, sem, *, n):
    i = pl.program_id(0)
    slot = i % 2

    def row_copy(idx8, r, k, s):
        dst = ybuf.at[s, pl.ds((k * ROW_TILE + r) * ROW_SUBLANES, ROW_SUBLANES)]
        return pltpu.make_async_copy(_row_tile(yb_hbm, idx8), dst, sem.at[s])

    @pl.when(i == 0)
    def _():
        def body(r, c):
            idx = [dest_ref[0, k * ROW_TILE + r] for k in range(TOP_K)]
            for k in range(TOP_K):
                row_copy(idx[k], r, k, 0).start()
            return c
        lax.fori_loop(0, ROW_TILE, body, 0)

    @pl.when(i + 1 < n)
    def _():
        for r0 in range(0, ROW_TILE, INDEX_BATCH):
            idx = [[dest_next_ref[0, k * ROW_TILE + r] for r in range(r0, r0 + INDEX_BATCH)] for k in range(TOP_K)]
            for r in range(r0, r0 + INDEX_BATCH):
                for k in range(TOP_K):
                    row_copy(idx[k][r - r0], r, k, 1 - slot).start(priority=k % DMA_THREADS)

    _tile_wait(yb_hbm, ybuf.at[slot], sem.at[slot])
    route = route_ref[...]
    buf = ybuf.at[slot]
    y = (_load_rows(buf, ROW_TILE) * route[:, 2:3]
         + _load_rows(buf, ROW_TILE, offset=ROW_TILE * ROW_SUBLANES) * route[:, 3:4])
    x2 = x1_ref[...] + y
    ms = jnp.mean(x2 * x2, axis=-1, keepdims=True)
    o_ref[...] = x2 * lax.rsqrt(ms + EPS) * g_ref[...]


def _combine_and_norm(dest8, x1, route, norm_w, yb):
    t = x1.shape[0]
    n_tiles = t // ROW_TILE
    dest_spec = lambda f: pl.BlockSpec((None, 1, TILE_ROW_COPIES), lambda i: (f(i), 0, 0), memory_space=pltpu.SMEM)
    return pl.pallas_call(
        functools.partial(_combine_kernel, n=n_tiles),
        grid=(n_tiles,),
        in_specs=[dest_spec(lambda i: i), dest_spec(lambda i: jnp.minimum(i + 1, n_tiles - 1)),
                  pl.BlockSpec((ROW_TILE, D_MODEL), lambda i: (i, 0)),
                  pl.BlockSpec((ROW_TILE, LANES), lambda i: (i, 0)),
                  _const_spec((1, D_MODEL)),
                  pl.BlockSpec(memory_space=pl.ANY)],
        out_specs=pl.BlockSpec((ROW_TILE, D_MODEL), lambda i: (i, 0)),
        out_shape=jax.ShapeDtypeStruct((t, D_MODEL), F32),
        scratch_shapes=[pltpu.VMEM((2, TILE_ROW_COPIES * ROW_SUBLANES, LANES), F32),
                        pltpu.SemaphoreType.DMA((2,))],
        name="combine_and_norm",
        compiler_params=_params("arbitrary", disable_bounds_checks=True),
    )(dest8, dest8, x1, route, norm_w, yb)


def _dispatch_plan(route_t, counts):
    t = route_t.shape[1]
    counts = counts[0, 0:N_EXPERTS].astype(jnp.int32)
    padded = (counts + MOE_BLOCK - 1) // MOE_BLOCK * MOE_BLOCK
    pad_end = jnp.cumsum(padded)
    pad_start = pad_end - padded
    expert = route_t[0:TOP_K].astype(jnp.int32)
    rank = route_t[4:4 + TOP_K].astype(jnp.int32)
    start_of = sum(jnp.where(expert == e, pad_start[e], 0) for e in range(N_EXPERTS))
    dest8 = (start_of + rank) * ROW_SUBLANES
    dest8 = dest8.reshape(TOP_K, t // ROW_TILE, ROW_TILE).transpose(1, 0, 2).reshape(t // ROW_TILE, 1, TILE_ROW_COPIES)
    n_blocks = -(-(t * TOP_K + N_EXPERTS * (MOE_BLOCK - 1)) // MOE_BLOCK)
    block_start = jnp.arange(n_blocks, dtype=jnp.int32) * MOE_BLOCK
    block_expert = jnp.minimum(jnp.sum((pad_end[None, :] <= block_start[:, None]).astype(jnp.int32), axis=1),
                               N_EXPERTS - 1)
    block_valid = (block_start < pad_end[-1]).astype(jnp.int32)
    n_slots = n_blocks * MOE_BLOCK
    fill_start = jnp.concatenate([pad_start + counts, pad_end[-1:]])
    fill_n = jnp.concatenate([padded - counts, n_slots - pad_end[-1:]])
    return dest8, fill_start, fill_n, block_expert, block_valid, n_slots


def _rel_bias_band(rel_table, qc):
    rows = qc * CHUNK
    window = ATTN_REACH + rows
    n = window + rows - 1
    diag = rel_table[:, np.clip(ATTN_REACH + rows - 1 - np.arange(n), -MAX_REL, MAX_REL) + MAX_REL].astype(F32)
    shifted = jnp.tile(diag, (1, rows + 1))[:, :rows * (n + 1)].reshape(N_HEADS_A, rows, n + 1)[:, :, :window]
    bias = shifted[:, ::-1, :] * LOG2E
    i = np.arange(rows)[:, None]
    j = np.arange(window)[None, :]
    first = (i // CHUNK) * CHUNK
    return jnp.where((j >= first) & (j < first + BAND), bias, NEG_INF)


def _pad_lanes(v, width=LANES):
    return jnp.pad(v.astype(F32), (0, width - v.shape[0])).reshape(1, width)


def _mixers(x, kv_cache, conv_state, ssm_state, w):
    batch, l, _ = x.shape
    nc = l // CHUNK
    t = batch * l
    x2d = x.reshape(t, D_MODEL)
    proj, dt_raw = _in_projection(x2d, w["norm_mix"], w["w_proj"])
    proj3 = proj.reshape(batch, l, PROJ_WIDTH)
    k_new = lambda rows: proj3[:, rows, K0:K0 + ATTN_WIDTH].astype(F32)
    v_new = lambda rows: proj3[:, rows, V0:V0 + ATTN_WIDTH].astype(F32)
    if kv_cache is None:
        qc = ATTN_CHUNKS_PER_STEP if nc % ATTN_CHUNKS_PER_STEP == 0 else 1
        kv_spec = [pl.BlockSpec((l, ATTN_WIDTH), lambda b, c: (b, K0 // ATTN_WIDTH)),
                   pl.BlockSpec((l, ATTN_WIDTH), lambda b, c: (b, V0 // ATTN_WIDTH))]
        attn = _band_attention(proj, Q0 // ATTN_WIDTH, proj, proj, kv_spec, _rel_bias_band(w["rel_table"], qc),
                               batch=batch, nc=nc, lk=l, pad_rows=ATTN_REACH, qc=qc)
        keep = slice(l - min(ATTN_REACH, l), l)
        k_keep, v_keep = k_new(keep), v_new(keep)
    else:
        assert nc == 1
        cache_k, cache_v = kv_cache
        wlen = cache_k.shape[1]
        kk = jnp.concatenate([cache_k.reshape(batch, wlen, ATTN_WIDTH), k_new(slice(None))], axis=1)
        vv = jnp.concatenate([cache_v.reshape(batch, wlen, ATTN_WIDTH), v_new(slice(None))], axis=1)
        spec = pl.BlockSpec((None, wlen + l, ATTN_WIDTH), lambda b, c: (b, 0, 0))
        attn = _band_attention(proj, Q0 // ATTN_WIDTH, kk, vv, [spec, spec], _rel_bias_band(w["rel_table"], 1),
                               batch=batch, nc=nc, lk=wlen + l, pad_rows=BAND - (wlen + l), qc=1)
        k_keep, v_keep = kk[:, l:], vv[:, l:]
    tail = proj3[:, l - (CONV_WIDTH - 1):, XBC0:XBC0 + CONV_DIM].astype(F32)
    if conv_state is None:
        conv0 = st0 = None
        conv_new = tail
    else:
        conv0 = jnp.pad(conv_state, ((0, 0), (CARRY_ROWS - (CONV_WIDTH - 1), 0), (0, 0)))
        st0 = jnp.transpose(ssm_state.reshape(batch, D_INNER, D_STATE), (0, 2, 1))
        conv_new = jnp.concatenate([conv_state, tail], axis=1)[:, -(CONV_WIDTH - 1):]
    ssm, st_t = _ssd_mixer(proj, dt_raw, conv0, st0, w["conv_w"], w["conv_b"], w["dt_bias"], w["a_log"], w["d_exp"],
                           w["ssm_norm"], batch=batch, nc=nc)
    ssm_new = jnp.transpose(st_t, (0, 2, 1)).reshape(batch, SSM_HEADS, SSM_HEAD_DIM, D_STATE)
    heads = lambda u: u.reshape(batch, u.shape[1], N_HEADS_A, HEAD_DIM_A)
    return (x2d, attn, ssm, proj), (heads(k_keep), heads(v_keep), conv_new, ssm_new)


def kernel(x_prompt, x_sample, cache_attn_k, cache_attn_v, state_conv, state_ssm, norm_mix, w_in, rel_bias, conv_w, conv_b, dt_bias, a_log, d_skip, ssm_norm, w_branch_a, w_branch_b, w_out, norm_ffn, w_router_group, b_router_group, w_router_expert, b_router_expert, w_gate, w_up, w_down, norm_final):
    assert norm_mix.shape[0] == 1, "single-layer trunk"
    q, k, v, z, xbc, dtc, ga, gb = jnp.split(w_in[0], np.cumsum(
        (ATTN_WIDTH, ATTN_WIDTH, ATTN_WIDTH, D_INNER, CONV_DIM, SSM_HEADS, D_MODEL, D_MODEL))[:-1].tolist(), axis=1)
    w_proj = jnp.concatenate([z, ga, gb, xbc, q, k, v, jnp.pad(dtc, ((0, 0), (0, LANES - SSM_HEADS)))],
                             axis=1).astype(BF16)
    w_router = jnp.pad(jnp.concatenate([w_router_group[0], w_router_expert[0]], axis=1),
                       ((0, 0), (0, LANES - N_EXPERT_GROUPS - N_EXPERTS)))
    w_router_hi = w_router.astype(BF16)
    w_router_lo = (w_router - w_router_hi.astype(F32)).astype(BF16)
    b_router = _pad_lanes(jnp.concatenate([b_router_group[0], b_router_expert[0]]))
    w = dict(
        norm_mix=norm_mix[0].reshape(1, D_MODEL), w_proj=w_proj, rel_table=rel_bias[0],
        conv_w=conv_w[0], conv_b=conv_b[0].reshape(1, CONV_DIM), dt_bias=_pad_lanes(dt_bias[0]),
        a_log=_pad_lanes(a_log[0]), d_exp=jnp.repeat(d_skip[0], SSM_HEAD_DIM).reshape(1, D_INNER),
        ssm_norm=ssm_norm[0].reshape(1, D_INNER),
    )
    mix_w = (w_branch_a[0].astype(BF16), w_branch_b[0].astype(BF16), w_out[0].astype(BF16),
             norm_ffn[0].reshape(1, D_MODEL), jnp.concatenate([w_router_hi, w_router_lo], axis=1), b_router)
    norm_out = norm_final.reshape(1, D_MODEL)

    groups = [_mixers(x_prompt, None, None, None, w),
              _mixers(x_sample, (cache_attn_k[0], cache_attn_v[0]), state_conv[0], state_ssm[0], w)]
    counts = jnp.zeros((1, LANES), F32)
    mixed = []
    routes_t = []
    for acts, _ in groups:
        x1, hn, route, route_t, counts = _mix_and_route(*acts, *mix_w, counts)
        mixed.append((x1, hn, route))
        routes_t.append(route_t)
    dest8, fill_start, fill_n, block_expert, block_valid, n_slots = _dispatch_plan(
        jnp.concatenate(routes_t, axis=1), counts)
    xs = _dispatch(dest8, fill_start, fill_n, [m[1] for m in mixed], n_slots)
    yb = _routed_experts(xs, block_expert, block_valid, w_gate[0], w_up[0], w_down[0])
    outs, first = [], 0
    for (x1, _, route), (acts, _) in zip(mixed, groups):
        n_tiles = x1.shape[0] // ROW_TILE
        y = _combine_and_norm(dest8[first:first + n_tiles], x1, route, norm_out, yb)
        first += n_tiles
        outs.append(y)
    (yp, ys), ((kp, vp, cp, sp), (ks, vs, cs, ss)) = outs, [g[1] for g in groups]
    yp = yp.reshape(x_prompt.shape)
    ys = ys.reshape(x_sample.shape)
    return (yp, ys, kp[None], vp[None], cp[None], sp[None], ks[None], vs[None], cs[None], ss[None])
```

```python
import functools
import math

import numpy as np
import jax
import jax.numpy as jnp
from jax import lax
from jax.experimental import pallas as pl
from jax.experimental.pallas import tpu as pltpu

F32 = jnp.float32
BF16 = jnp.bfloat16

D_MODEL = 1024
CHUNK = 64
LEFT_CHUNKS = 8
ATTN_REACH = LEFT_CHUNKS * CHUNK
BAND = ATTN_REACH + CHUNK
N_HEADS_A = 8
HEAD_DIM_A = 64
ATTN_WIDTH = N_HEADS_A * HEAD_DIM_A
MAX_REL = 256
ATTN_SCALE = 1.0 / math.sqrt(HEAD_DIM_A)
NEG_INF = -1e30
SSM_HEADS = 16
SSM_HEAD_DIM = 64
D_INNER = SSM_HEADS * SSM_HEAD_DIM
SSM_GROUPS = 2
GROUP_WIDTH = D_INNER // SSM_GROUPS
D_STATE = 128
CONV_WIDTH = 4
CONV_DIM = D_INNER + 2 * SSM_GROUPS * D_STATE
N_EXPERT_GROUPS = 4
EXPERTS_PER_GROUP = 8
N_EXPERTS = N_EXPERT_GROUPS * EXPERTS_PER_GROUP
TOP_K = 2
D_EXPERT = 512
MOE_BLOCK = 256
EPS = 1e-6

LANES = 128
SUBLANES = 8
Z0, GA0, GB0, XBC0, Q0, K0, V0 = 0, 1024, 2048, 3072, 4608, 5120, 5632
PROJ_WIDTH = V0 + ATTN_WIDTH
PROJ_TILE = 1024
ROW_TILE = 256
ATTN_CHUNKS_PER_STEP = 2
SSD_STREAMS = 4
VMEM_LIMIT = 56 * 1024 * 1024

_NT = (((1,), (1,)), ((), ()))


def _params(*sem, **kw):
    return pltpu.CompilerParams(dimension_semantics=sem, vmem_limit_bytes=VMEM_LIMIT, **kw)


def _const_spec(shape):
    nd = len(shape)
    return pl.BlockSpec(shape, lambda *_: (0,) * nd)


def _inproj_kernel(x_ref, g_ref, w_ref, o_ref, dt_ref):
    x = x_ref[...]
    ms = jnp.mean(x * x, axis=-1, keepdims=True)
    h = (x * lax.rsqrt(ms + EPS) * g_ref[...]).astype(BF16)
    for j in range(0, PROJ_WIDTH, PROJ_TILE):
        o_ref[:, j:j + PROJ_TILE] = jnp.dot(h, w_ref[:, j:j + PROJ_TILE], preferred_element_type=F32).astype(BF16)
    dt_ref[...] = jnp.dot(h, w_ref[:, PROJ_WIDTH:PROJ_WIDTH + LANES], preferred_element_type=F32)


def _in_projection(x2d, norm_w, w_proj):
    t = x2d.shape[0]
    return pl.pallas_call(
        _inproj_kernel,
        grid=(t // ROW_TILE,),
        in_specs=[pl.BlockSpec((ROW_TILE, D_MODEL), lambda i: (i, 0)),
                  _const_spec((1, D_MODEL)),
                  _const_spec((D_MODEL, PROJ_WIDTH + LANES))],
        out_specs=[pl.BlockSpec((ROW_TILE, PROJ_WIDTH), lambda i: (i, 0)),
                   pl.BlockSpec((ROW_TILE, LANES), lambda i: (i, 0))],
        out_shape=[jax.ShapeDtypeStruct((t, PROJ_WIDTH), BF16), jax.ShapeDtypeStruct((t, LANES), F32)],
        name="in_projection",
        compiler_params=_params("parallel"),
    )(x2d, norm_w, w_proj)


LOG2E = math.log2(math.e)
ATTN_STAGES = 3


def _attn_kernel(q_ref, *refs, pad_rows, part_rows, qc):
    n_parts = len(part_rows)
    k_parts, v_parts = refs[:n_parts], refs[n_parts:2 * n_parts]
    bias_ref, o_ref, kpad, vpad = refs[2 * n_parts:]
    rows = qc * CHUNK
    window = ATTN_REACH + rows
    c = pl.program_id(1)

    @pl.when(c == 0)
    def _():
        if pad_rows:
            kpad[0:pad_rows, :] = jnp.zeros((pad_rows, ATTN_WIDTH), BF16)
            vpad[0:pad_rows, :] = jnp.zeros((pad_rows, ATTN_WIDTH), BF16)
        first = pad_rows
        for k_ref, v_ref, n in zip(k_parts, v_parts, part_rows):
            kpad[first:first + n, :] = k_ref[...].astype(BF16)
            vpad[first:first + n, :] = v_ref[...].astype(BF16)
            first += n

    start = pl.multiple_of(c * rows, rows)

    def body(mask_start):
        kw = kpad[pl.ds(start, window), :]
        vw = vpad[pl.ds(start, window), :]
        q = (q_ref[...].astype(F32) * (ATTN_SCALE * LOG2E)).astype(BF16)
        low = lax.broadcasted_iota(jnp.int32, (rows, LANES), 1) < HEAD_DIM_A
        if mask_start:
            valid = lax.broadcasted_iota(jnp.int32, (rows, window), 1) + start >= pad_rows
        outs = {}

        def head(h):
            hp, sub = divmod(h, 2)
            sl = slice(LANES * hp, LANES * (hp + 1))
            q2 = q[:, sl]
            qm = jnp.where(low if sub == 0 else jnp.logical_not(low), q2, jnp.zeros_like(q2))
            s = lax.dot_general(qm, kw[:, sl], _NT, preferred_element_type=F32) + bias_ref[h]
            if mask_start:
                s = jnp.where(valid, s, NEG_INF)
            yield
            m = jnp.max(s, axis=-1, keepdims=True)
            p = jnp.exp2(s - m)
            l = jnp.sum(p, axis=-1, keepdims=True)
            pb = p.astype(BF16)
            yield
            outs[h] = jnp.dot(pb, vw[:, sl], preferred_element_type=F32) / l
            if sub == 1:
                o_ref[:, sl] = jnp.where(low, outs[h - 1], outs[h]).astype(o_ref.dtype)

        heads = [head(h) for h in range(N_HEADS_A)]
        for t in range(N_HEADS_A + ATTN_STAGES - 1):
            for h in range(min(t, N_HEADS_A - 1), max(t - ATTN_STAGES, -1), -1):
                next(heads[h], None)

    if pad_rows:
        pl.when(start < pad_rows)(lambda: body(True))
        pl.when(start >= pad_rows)(lambda: body(False))
    else:
        body(False)


def _band_attention(q_arr, q_col, k_parts, v_parts, bias, *, batch, nc, pad_rows, qc):
    steps = nc // qc
    rows = qc * CHUNK
    part_rows = tuple(p[2] for p in k_parts)
    lk = sum(part_rows)
    kern = functools.partial(_attn_kernel, pad_rows=pad_rows, part_rows=part_rows, qc=qc)
    return pl.pallas_call(
        kern,
        grid=(batch, steps),
        in_specs=[pl.BlockSpec((rows, ATTN_WIDTH), lambda b, c: (b * steps + c, q_col))]
        + [p[1] for p in k_parts] + [p[1] for p in v_parts]
        + [_const_spec((N_HEADS_A, rows, ATTN_REACH + rows))],
        out_specs=pl.BlockSpec((rows, ATTN_WIDTH), lambda b, c: (b * steps + c, 0)),
        out_shape=jax.ShapeDtypeStruct((batch * nc * CHUNK, ATTN_WIDTH), BF16),
        scratch_shapes=[pltpu.VMEM((pad_rows + lk, ATTN_WIDTH), BF16),
                        pltpu.VMEM((pad_rows + lk, ATTN_WIDTH), BF16)],
        name="band_attention",
        compiler_params=_params("parallel", "arbitrary"),
    )(q_arr, *[p[0] for p in k_parts], *[p[0] for p in v_parts], bias)


CARRY_ROWS = SUBLANES
SHIFT_TAPS = CONV_WIDTH - 1
SHIFT_K = 2 * CARRY_ROWS + CHUNK


def _split_bf16(x, parts):
    out = []
    for _ in range(parts - 1):
        h = x.astype(BF16)
        out.append(h)
        x = x - h.astype(F32)
    out.append(x.astype(BF16))
    return out


def _ssd_kernel(xbc_ref, z_ref, dt_ref, conv0_ref, st0_ref, cw_ref, cb_ref, dtb_ref, alog_ref,
                dexp_ref, nw_ref, tri_ref, exp_ref, sel_ref, eye_ref, shift_ref,
                y_ref, stout_ref, carry, st, *, has_init, streams, nc):
    c = pl.program_id(1)

    @pl.when(c == 0)
    def _():
        if has_init:
            carry[...] = conv0_ref[...]
            st[...] = st0_ref[...]
        else:
            carry[...] = jnp.zeros((streams, CARRY_ROWS, CONV_DIM), F32)
            st[...] = jnp.zeros((streams, D_STATE, D_INNER), F32)

    chunks = [_ssd_chunk(xbc_ref.at[s], z_ref.at[s], dt_ref.at[s], cw_ref, cb_ref, dtb_ref, alog_ref, dexp_ref,
                         nw_ref, tri_ref, exp_ref, sel_ref, eye_ref, shift_ref, y_ref.at[s], carry.at[s], st.at[s])
              for s in range(streams)]
    while chunks:
        chunks = [g for g in chunks if next(g, True) is None]

    @pl.when(c == nc - 1)
    def _():
        stout_ref[...] = st[...]


def _ssd_chunk(xbc_ref, z_ref, dt_ref, cw_ref, cb_ref, dtb_ref, alog_ref, dexp_ref, nw_ref,
               tri_ref, exp_ref, sel_ref, eye_ref, shift_ref, y_ref, carry, st):
    x_f = xbc_ref[...].astype(F32)
    prev = carry[...]
    prev_hi = prev.astype(BF16).astype(F32)
    stacked = jnp.concatenate([prev_hi, prev - prev_hi, x_f], axis=0).astype(BF16)
    taps = jnp.dot(shift_ref[...], stacked, preferred_element_type=F32)

    dt_in = dt_ref[...] + dtb_ref[...]
    dt = jnp.maximum(dt_in, 0.0) + jnp.log1p(jnp.exp(-jnp.abs(dt_in)))
    da = dt * (-jnp.exp(alog_ref[...]))
    tri = tri_ref[...]
    cs3 = jnp.dot(tri, jnp.concatenate(_split_bf16(da, 3), axis=1), preferred_element_type=F32)
    yield

    conv = cb_ref[...] + x_f * cw_ref[CONV_WIDTH - 1:CONV_WIDTH, :]
    for j in range(SHIFT_TAPS):
        conv = conv + taps[j * CHUNK:(j + 1) * CHUNK, :] * cw_ref[j:j + 1, :]
    u = conv * jax.nn.sigmoid(conv)
    xs = u[:, 0:D_INNER]
    cs = cs3[:, 0:LANES] + cs3[:, LANES:2 * LANES] + cs3[:, 2 * LANES:3 * LANES]
    cs_last = cs[CHUNK - 1:CHUNK, :]
    ecs = jnp.exp(cs)
    to_end = jnp.exp(cs_last - cs)

    stack = jnp.concatenate(_split_bf16(dt, 1) + _split_bf16(to_end, 1) + _split_bf16(ecs, 2)
                            + _split_bf16(cs, 3), axis=0)
    wide = jnp.dot(stack, exp_ref[...], preferred_element_type=F32)

    lane = lax.broadcasted_iota(jnp.int32, (CHUNK, LANES), 1)
    even = (lane % 2) == 0
    csm = jnp.concatenate([jnp.where(even, cs, 0.0), jnp.where(even, 0.0, cs)], axis=0)
    srow3 = lax.dot_general(sel_ref[...], jnp.concatenate(_split_bf16(csm, 3), axis=0), _NT,
                            preferred_element_type=F32)
    yield

    rows = [wide[i * CHUNK:(i + 1) * CHUNK, :] for i in range(7)]
    dt_w = rows[0]
    to_end_w = rows[1]
    ecs_w = rows[2] + rows[3]
    cs_w = rows[4] + rows[5] + rows[6]
    srow = srow3[:, 0:LANES] + srow3[:, LANES:2 * LANES] + srow3[:, 2 * LANES:3 * LANES]

    xdt = xs * dt_w
    xdt_b = xdt.astype(BF16)
    xw_b = (xdt * to_end_w).astype(BF16)
    low = lane < SSM_HEAD_DIM
    t_idx = lax.broadcasted_iota(jnp.int32, (CHUNK, LANES), 0)
    causal = t_idx >= (lane % SSM_HEAD_DIM)
    eye = eye_ref[...]

    y_parts = []
    for g in range(SSM_GROUPS):
        b_g = u[:, D_INNER + g * D_STATE:D_INNER + (g + 1) * D_STATE].astype(BF16)
        c_g = u[:, D_INNER + (SSM_GROUPS + g) * D_STATE:D_INNER + (SSM_GROUPS + g + 1) * D_STATE].astype(BF16)
        gsl = slice(g * GROUP_WIDTH, (g + 1) * GROUP_WIDTH)
        st_g = st[:, gsl]
        y_off = jnp.dot(c_g, st_g.astype(BF16), preferred_element_type=F32) * ecs_w[:, gsl]
        b_t = lax.dot_general(eye, b_g, _NT, preferred_element_type=F32).astype(BF16)
        st[:, gsl] = st_g * ecs_w[CHUNK - 1:CHUNK, gsl] + jnp.dot(b_t, xw_b[:, gsl], preferred_element_type=F32)
        cb2 = lax.dot_general(c_g, jnp.concatenate([b_g, b_g], axis=0), _NT, preferred_element_type=F32)
        yield
        for kk in range(SSM_HEADS // SSM_GROUPS // 2):
            k = g * (SSM_HEADS // SSM_GROUPS // 2) + kk
            psl = slice(k * LANES, (k + 1) * LANES)
            decay = jnp.where(causal, jnp.exp(cs_w[:, psl] - srow[k:k + 1, :]), 0.0)
            gmat = (cb2 * decay).astype(BF16)
            xp = xdt_b[:, psl]
            xblk = jnp.concatenate([jnp.where(low, xp, jnp.zeros_like(xp)),
                                    jnp.where(low, jnp.zeros_like(xp), xp)], axis=0)
            y_parts.append(jnp.dot(gmat, xblk, preferred_element_type=F32)
                           + y_off[:, kk * LANES:(kk + 1) * LANES])
        yield
    carry[...] = x_f[CHUNK - CARRY_ROWS:CHUNK, :]
    y = jnp.concatenate(y_parts, axis=1) + xs * dexp_ref[...]
    zz = z_ref[...].astype(F32)
    y = y * (zz * jax.nn.sigmoid(zz))
    outs = []
    for g in range(SSM_GROUPS):
        yg = y[:, g * GROUP_WIDTH:(g + 1) * GROUP_WIDTH]
        outs.append(yg * lax.rsqrt(jnp.mean(yg * yg, axis=-1, keepdims=True) + EPS))
    y_ref[...] = (jnp.concatenate(outs, axis=1) * nw_ref[...]).astype(y_ref.dtype)


def _ssd_constants():
    tri = np.tril(np.ones((CHUNK, CHUNK), np.float32))
    expand = np.zeros((LANES, D_INNER), np.float32)
    for e in range(SSM_HEADS):
        expand[e, e * SSM_HEAD_DIM:(e + 1) * SSM_HEAD_DIM] = 1.0
    sel = np.zeros((16, LANES), np.float32)
    for e in range(SSM_HEADS):
        sel[e // 2, e] = 1.0
    eye = np.eye(D_STATE, dtype=np.float32)
    shift = np.zeros((SHIFT_TAPS * CHUNK, SHIFT_K), np.float32)
    for j in range(SHIFT_TAPS):
        for t in range(CHUNK):
            m = t - (CONV_WIDTH - 1) + j
            if m >= 0:
                shift[j * CHUNK + t, 2 * CARRY_ROWS + m] = 1.0
            else:
                shift[j * CHUNK + t, CARRY_ROWS + m] = 1.0
                shift[j * CHUNK + t, 2 * CARRY_ROWS + m] = 1.0
    return tuple(jnp.asarray(a, BF16) for a in (tri, expand, sel, eye, shift))


def _ssd_mixer(proj, dt_raw, conv0, st0, conv_w, conv_b, dt_bias_p, a_log_p, d_exp, norm_w, *, batch, nc):
    streams = SSD_STREAMS if batch % SSD_STREAMS == 0 else 1
    groups = batch // streams
    l = nc * CHUNK
    has_init = conv0 is not None
    if not has_init:
        conv0 = jnp.zeros((1, streams, CARRY_ROWS, CONV_DIM), F32)
        st0 = jnp.zeros((1, streams, D_STATE, D_INNER), F32)
        init_map = lambda b, c: (0, 0, 0, 0)
    else:
        conv0 = conv0.reshape(groups, streams, CARRY_ROWS, CONV_DIM)
        st0 = st0.reshape(groups, streams, D_STATE, D_INNER)
        init_map = lambda b, c: (b, 0, 0, 0)
    tri, expand, sel, eye, shift = _ssd_constants()
    proj4 = proj.reshape(groups, streams, l, PROJ_WIDTH)
    dt4 = dt_raw.reshape(groups, streams, l, LANES)
    chunk_of = lambda w, col: pl.BlockSpec((None, streams, CHUNK, w), lambda b, c: (b, 0, c, col))
    state_spec = lambda m: pl.BlockSpec((None, streams, D_STATE, D_INNER), m)
    kern = functools.partial(_ssd_kernel, has_init=has_init, streams=streams, nc=nc)
    y, st_out = pl.pallas_call(
        kern,
        grid=(groups, nc),
        in_specs=[chunk_of(CONV_DIM, XBC0 // CONV_DIM), chunk_of(D_INNER, Z0 // D_INNER), chunk_of(LANES, 0),
                  pl.BlockSpec((None, streams, CARRY_ROWS, CONV_DIM), init_map), state_spec(init_map),
                  _const_spec((CONV_WIDTH, CONV_DIM)), _const_spec((1, CONV_DIM)),
                  _const_spec((1, LANES)), _const_spec((1, LANES)),
                  _const_spec((1, D_INNER)), _const_spec((1, D_INNER)),
                  _const_spec((CHUNK, CHUNK)), _const_spec((LANES, D_INNER)),
                  _const_spec((16, LANES)), _const_spec((D_STATE, D_STATE)),
                  _const_spec((SHIFT_TAPS * CHUNK, SHIFT_K))],
        out_specs=[chunk_of(D_INNER, 0), state_spec(lambda b, c: (b, 0, 0, 0))],
        out_shape=[jax.ShapeDtypeStruct((groups, streams, l, D_INNER), BF16),
                   jax.ShapeDtypeStruct((groups, streams, D_STATE, D_INNER), F32)],
        scratch_shapes=[pltpu.VMEM((streams, CARRY_ROWS, CONV_DIM), F32),
                        pltpu.VMEM((streams, D_STATE, D_INNER), F32)],
        name="ssd_mixer",
        compiler_params=_params("parallel", "arbitrary"),
    )(proj4, proj4, dt4, conv0, st0, conv_w, conv_b, dt_bias_p, a_log_p, d_exp, norm_w, tri, expand, sel, eye, shift)
    return y.reshape(batch * l, D_INNER), st_out.reshape(batch, D_STATE, D_INNER)


ROW_SUBLANES = D_MODEL // LANES
assert ROW_SUBLANES == SUBLANES


def _store_rows(ref, x, rows, offset=0):
    for s in range(ROW_SUBLANES):
        ref[pl.ds(offset + s, rows, stride=ROW_SUBLANES), :] = x[:, s * LANES:(s + 1) * LANES]


def _load_rows(ref, rows, offset=0):
    return jnp.concatenate([ref[pl.ds(offset + s, rows, stride=ROW_SUBLANES), :] for s in range(ROW_SUBLANES)],
                           axis=1)


def _row_tile(ref, idx8):
    return ref.at[pl.ds(pl.multiple_of(idx8, ROW_SUBLANES), ROW_SUBLANES)]


def _mix_kernel(x_ref, attn_ref, ssm_ref, ga_ref, gb_ref, wa_ref, wb_ref, wo_ref, g_ref, wr_ref, br_ref, slt_ref,
                count0_ref, x1_ref, hn_ref, route_ref, route_t_ref, counts_ref, carry, *, subtiles):
    @pl.when(pl.program_id(0) == 0)
    def _():
        carry[...] = count0_ref[...]

    running = {"counts": carry[...]}
    tiles = [_mix_tile(j * ROW_TILE, running, x_ref, attn_ref, ssm_ref, ga_ref, gb_ref, wa_ref, wb_ref, wo_ref, g_ref,
                       wr_ref, br_ref, slt_ref, x1_ref, hn_ref, route_ref, route_t_ref) for j in range(subtiles)]
    while tiles:
        tiles = [g for g in tiles if next(g, True) is None]
    carry[...] = running["counts"]
    counts_ref[...] = running["counts"]


def _mix_tile(r0, running, x_ref, attn_ref, ssm_ref, ga_ref, gb_ref, wa_ref, wb_ref, wo_ref, g_ref, wr_ref, br_ref,
              slt_ref, x1_ref, hn_ref, route_ref, route_t_ref):
    rows = slice(r0, r0 + ROW_TILE)
    a = jnp.dot(attn_ref[rows, :], wa_ref[...], preferred_element_type=F32)
    s = jnp.dot(ssm_ref[rows, :], wb_ref[...], preferred_element_type=F32)
    yield
    mixed = jax.nn.sigmoid(ga_ref[rows, :].astype(F32)) * a + jax.nn.sigmoid(gb_ref[rows, :].astype(F32)) * s
    x1 = x_ref[rows, :] + jnp.dot(mixed.astype(BF16), wo_ref[...], preferred_element_type=F32)
    yield
    x1_ref[rows, :] = x1
    ms = jnp.mean(x1 * x1, axis=-1, keepdims=True)
    hn = x1 * lax.rsqrt(ms + EPS) * g_ref[...]
    _store_rows(hn_ref, hn, ROW_TILE, offset=r0 * ROW_SUBLANES)

    hi = hn.astype(BF16)
    lo = (hn - hi.astype(F32)).astype(BF16)
    both = jnp.dot(hi, wr_ref[...], preferred_element_type=F32)
    lo_part = jnp.dot(lo, wr_ref[:, 0:LANES], preferred_element_type=F32)
    yield
    logits = both[:, 0:LANES] + both[:, LANES:2 * LANES] + lo_part + br_ref[...]
    lane = lax.broadcasted_iota(jnp.int32, logits.shape, 1)
    lane_f = lane.astype(F32)
    big = float(LANES)
    is_g = lane < N_EXPERT_GROUPS
    gl = jnp.where(is_g, logits, -jnp.inf)
    gmax = jnp.max(gl, axis=-1, keepdims=True)
    gidx = jnp.min(jnp.where(gl == gmax, lane_f, big), axis=-1, keepdims=True)
    gsum = jnp.sum(jnp.where(is_g, jnp.exp(gl - gmax), 0.0), axis=-1, keepdims=True)
    g_top = 1.0 / gsum
    first = N_EXPERT_GROUPS + gidx * EXPERTS_PER_GROUP
    in_group = (lane_f >= first) & (lane_f < first + EXPERTS_PER_GROUP)
    el = jnp.where(in_group, logits, -jnp.inf)
    m1 = jnp.max(el, axis=-1, keepdims=True)
    i1 = jnp.min(jnp.where(el == m1, lane_f, big), axis=-1, keepdims=True)
    el2 = jnp.where(lane_f == i1, -jnp.inf, el)
    m2 = jnp.max(el2, axis=-1, keepdims=True)
    i2 = jnp.min(jnp.where(el2 == m2, lane_f, big), axis=-1, keepdims=True)
    r = jnp.exp(m2 - m1)
    w1 = g_top / (1.0 + r)
    w2 = g_top * r / (1.0 + r)
    e1 = i1 - N_EXPERT_GROUPS
    e2 = i2 - N_EXPERT_GROUPS

    oh1 = lane_f == e1
    oh2 = lane_f == e2
    hot1 = jnp.where(oh1, 1.0, 0.0)
    hot2 = jnp.where(oh2, 1.0, 0.0)
    onehots = jnp.concatenate([hot1, hot2], axis=1).astype(BF16)
    before = jnp.dot(slt_ref[...], onehots, preferred_element_type=F32)
    cnt1 = jnp.sum(hot1, axis=0, keepdims=True)
    cnt2 = jnp.sum(hot2, axis=0, keepdims=True)
    yield
    base = running["counts"]
    rank1 = jnp.sum(jnp.where(oh1, before[:, 0:LANES] + base, 0.0), axis=-1, keepdims=True)
    rank2 = jnp.sum(jnp.where(oh2, before[:, LANES:2 * LANES] + (base + cnt1), 0.0), axis=-1, keepdims=True)
    running["counts"] = base + cnt1 + cnt2

    route = jnp.zeros_like(logits)
    for col, val in enumerate((e1, e2, w1, w2, rank1, rank2)):
        route = jnp.where(lane == col, val, route)
    route_ref[rows, :] = route
    route_t_ref[:, rows] = jnp.transpose(route)[0:SUBLANES, :]


MIX_SUBTILES = 2


def _mix_and_route(x2d, attn, ssm, proj, wa, wb, wo, norm_w, w_router, b_router, count0):
    t = x2d.shape[0]
    subtiles = MIX_SUBTILES if t % (MIX_SUBTILES * ROW_TILE) == 0 else 1
    step_rows = subtiles * ROW_TILE
    tile = lambda w, col: pl.BlockSpec((step_rows, w), lambda i: (i, col))
    slt = jnp.asarray(np.tril(np.ones((ROW_TILE, ROW_TILE), np.float32), k=-1), BF16)
    return pl.pallas_call(
        functools.partial(_mix_kernel, subtiles=subtiles),
        grid=(t // step_rows,),
        in_specs=[tile(D_MODEL, 0), tile(ATTN_WIDTH, 0), tile(D_INNER, 0),
                  tile(D_MODEL, GA0 // D_MODEL), tile(D_MODEL, GB0 // D_MODEL),
                  _const_spec((ATTN_WIDTH, D_MODEL)), _const_spec((D_INNER, D_MODEL)),
                  _const_spec((D_MODEL, D_MODEL)), _const_spec((1, D_MODEL)),
                  _const_spec((D_MODEL, 2 * LANES)), _const_spec((1, LANES)),
                  _const_spec((ROW_TILE, ROW_TILE)), _const_spec((1, LANES))],
        out_specs=[tile(D_MODEL, 0), pl.BlockSpec((step_rows * ROW_SUBLANES, LANES), lambda i: (i, 0)),
                   tile(LANES, 0), pl.BlockSpec((SUBLANES, step_rows), lambda i: (0, i)), _const_spec((1, LANES))],
        out_shape=[jax.ShapeDtypeStruct((t, D_MODEL), F32),
                   jax.ShapeDtypeStruct((t * ROW_SUBLANES, LANES), F32),
                   jax.ShapeDtypeStruct((t, LANES), F32),
                   jax.ShapeDtypeStruct((SUBLANES, t), F32),
                   jax.ShapeDtypeStruct((1, LANES), F32)],
        scratch_shapes=[pltpu.VMEM((1, LANES), F32)],
        name="mix_and_route",
        compiler_params=_params("arbitrary"),
    )(x2d, attn, ssm, proj, proj, wa, wb, wo, norm_w, w_router, b_router, slt, count0)


TILE_ROW_COPIES = ROW_TILE * TOP_K
INDEX_BATCH = 8
DISPATCH_BUFFERS = 3
DMA_THREADS = 2


def _tile_wait(ref_a, ref_b, sem):
    n = TILE_ROW_COPIES * ROW_SUBLANES
    pltpu.make_async_copy(ref_a.at[pl.ds(0, n)], ref_b.at[pl.ds(0, n)], sem).wait()


def _dispatch_kernel(fill_start_ref, fill_n_ref, dest_ref, *refs, tiles_per_source, fill_rows):
    n_src = len(tiles_per_source)
    sources = refs[:n_src]
    xs_hbm, hbuf, zbuf, in_sem, out_sem, fill_sem = refs[n_src:]
    i = pl.program_id(0)
    n = pl.num_programs(0)
    tile_rows = ROW_TILE * ROW_SUBLANES

    def tile_copy(src, j, b):
        return pltpu.make_async_copy(src.at[pl.ds(pl.multiple_of(j * tile_rows, tile_rows), tile_rows)],
                                     hbuf.at[b], in_sem.at[b])

    def tile_in_start(j):
        b = j % DISPATCH_BUFFERS
        first = 0
        for src, count in zip(sources, tiles_per_source):
            pl.when((j >= first) & (j < first + count))(lambda src=src, first=first: tile_copy(src, j - first, b).start())
            first += count

    @pl.when(i == 0)
    def _():
        zbuf[...] = jnp.zeros((ROW_SUBLANES, LANES), F32)
        tile_in_start(i)
        pl.when(n > 1)(lambda: tile_in_start(i + 1))

    buf = i % DISPATCH_BUFFERS
    tile_copy(sources[0], 0, buf).wait()
    for r0 in range(0, ROW_TILE, INDEX_BATCH):
        dests = [[dest_ref[0, k * ROW_TILE + r] for r in range(r0, r0 + INDEX_BATCH)] for k in range(TOP_K)]
        for r in range(r0, r0 + INDEX_BATCH):
            src = hbuf.at[buf, pl.ds(r * ROW_SUBLANES, ROW_SUBLANES)]
            for k in range(TOP_K):
                dst = _row_tile(xs_hbm, dests[k][r - r0])
                pltpu.make_async_copy(src, dst, out_sem.at[buf]).start(priority=k % DMA_THREADS)

    @pl.when(i > 0)
    def _():
        _tile_wait(xs_hbm, xs_hbm, out_sem.at[(i - 1) % DISPATCH_BUFFERS])

    pl.when(i + 2 < n)(lambda: tile_in_start(i + 2))

    @pl.when(i == n - 1)
    def _():
        _tile_wait(xs_hbm, xs_hbm, out_sem.at[buf])

        def per_range(e, carry):
            first = fill_start_ref[e]

            def start(j, c):
                pltpu.make_async_copy(zbuf, _row_tile(xs_hbm, (first + j) * ROW_SUBLANES), fill_sem.at[0]).start()
                return c

            lax.fori_loop(0, fill_n_ref[e], start, 0)
            return carry

        lax.fori_loop(0, N_EXPERTS + 1, per_range, 0)
        filled = pl.ds(0, fill_rows * ROW_SUBLANES)
        pltpu.make_async_copy(xs_hbm.at[filled], xs_hbm.at[filled], fill_sem.at[0]).wait()


def _dispatch(dest8, fill_start, fill_n, hn_list, n_slots):
    n_tiles = dest8.shape[0]
    tiles_per_source = tuple(h.shape[0] // (ROW_TILE * ROW_SUBLANES) for h in hn_list)
    assert sum(tiles_per_source) == n_tiles
    grid_spec = pltpu.PrefetchScalarGridSpec(
        num_scalar_prefetch=2,
        grid=(n_tiles,),
        in_specs=[pl.BlockSpec((None, 1, TILE_ROW_COPIES), lambda i, fs, fn: (i, 0, 0), memory_space=pltpu.SMEM)]
        + [pl.BlockSpec(memory_space=pl.ANY)] * len(hn_list),
        out_specs=pl.BlockSpec(memory_space=pl.ANY),
        scratch_shapes=[pltpu.VMEM((DISPATCH_BUFFERS, ROW_TILE * ROW_SUBLANES, LANES), F32),
                        pltpu.VMEM((ROW_SUBLANES, LANES), F32),
                        pltpu.SemaphoreType.DMA((DISPATCH_BUFFERS,)), pltpu.SemaphoreType.DMA((DISPATCH_BUFFERS,)),
                        pltpu.SemaphoreType.DMA((1,))],
    )
    return pl.pallas_call(
        functools.partial(_dispatch_kernel, tiles_per_source=tiles_per_source,
                          fill_rows=n_slots - n_tiles * TILE_ROW_COPIES),
        grid_spec=grid_spec,
        out_shape=jax.ShapeDtypeStruct((n_slots * ROW_SUBLANES, LANES), F32),
        name="dispatch_rows",
        compiler_params=_params("arbitrary", disable_bounds_checks=True),
    )(fill_start, fill_n, dest8, *hn_list)


EXPERT_ROW_BUFFERS = 3


def _expert_kernel(bexp_ref, bval_ref, fetch_ref, cast_ref, cset_ref, set_ref, next_ref, xs_hbm, wg_hbm, wu_hbm, wd_hbm, y_hbm,
                   wg_f, wu_f, wd_f, wg_b, wu_b, wd_b, xbuf, ybuf, in_sem, out_sem, w_sem, *, n):
    i = pl.program_id(0)
    block_rows = MOE_BLOCK * ROW_SUBLANES
    buf = i % EXPERT_ROW_BUFFERS

    def weight_copies(e):
        return [pltpu.make_async_copy(src.at[e], dst, w_sem.at[j])
                for j, (src, dst) in enumerate(((wg_hbm, wg_f), (wu_hbm, wu_f), (wd_hbm, wd_f)))]

    def cast_staged(which):
        for c in weight_copies(0):
            c.wait()
        wg_b[which] = wg_f[...].astype(BF16)
        wu_b[which] = wu_f[...].astype(BF16)
        wd_b[which] = wd_f[...].astype(BF16)

    def block_of(ref, j):
        return ref.at[pl.ds(pl.multiple_of(j * block_rows, block_rows), block_rows)]

    def copy_in(j):
        b = j % EXPERT_ROW_BUFFERS
        return pltpu.make_async_copy(block_of(xs_hbm, j), xbuf.at[b], in_sem.at[b])

    def copy_out(j):
        b = j % EXPERT_ROW_BUFFERS
        return pltpu.make_async_copy(ybuf.at[b], block_of(y_hbm, j), out_sem.at[b])

    @pl.when(i == 0)
    def _():
        for j in range(min(EXPERT_ROW_BUFFERS - 1, n)):
            copy_in(j).start()

    ahead = i + EXPERT_ROW_BUFFERS - 1
    pl.when(ahead < n)(lambda: copy_in(ahead).start())

    @pl.when(i == 0)
    def _():
        for c in weight_copies(bexp_ref[0]):
            c.start()
        cast_staged(0)

    pl.when(cast_ref[i] == 1)(lambda: cast_staged(cset_ref[i]))

    @pl.when(fetch_ref[i] == 1)
    def _():
        for c in weight_copies(next_ref[i]):
            c.start()

    which = set_ref[i]
    copy_in(i).wait()
    behind = i - (EXPERT_ROW_BUFFERS - 1)
    pl.when(behind >= 0)(lambda: copy_out(behind).wait())

    @pl.when(bval_ref[i] > 0)
    def _():
        xb = _load_rows(xbuf.at[buf], MOE_BLOCK).astype(BF16)
        gate = jnp.dot(xb, wg_b[which], preferred_element_type=F32)
        up = jnp.dot(xb, wu_b[which], preferred_element_type=F32)
        hid = (gate * jax.nn.sigmoid(gate) * up).astype(BF16)
        _store_rows(ybuf.at[buf], jnp.dot(hid, wd_b[which], preferred_element_type=F32), MOE_BLOCK)

    @pl.when(bval_ref[i] == 0)
    def _():
        ybuf[buf] = jnp.zeros((block_rows, LANES), F32)

    copy_out(i).start()

    @pl.when(i == n - 1)
    def _():
        for back in range(EXPERT_ROW_BUFFERS - 1):
            pl.when(i - back >= 0)(lambda back=back: copy_out(i - back).wait())


WEIGHT_CAST_LAG = 3


def _weight_schedule(block_expert):
    n = block_expert.shape[0]
    idx = jnp.arange(n, dtype=jnp.int32)
    run_start = jnp.concatenate([jnp.ones((1,), bool), block_expert[1:] != block_expert[:-1]])
    run_id = jnp.cumsum(run_start.astype(jnp.int32)) - 1
    run_first = jnp.min(jnp.where(run_id[None, :] == idx[:, None], idx[None, :], n), axis=1)
    next_first = jnp.concatenate([run_first[1:], jnp.full((1,), n, jnp.int32)])
    has_next = next_first < n
    cast_at = jnp.where(has_next, jnp.minimum(run_first + WEIGHT_CAST_LAG, next_first), -1)
    hit = idx[:, None] == cast_at[None, :]
    cast = jnp.any(hit, axis=1).astype(jnp.int32)
    cast_set = jnp.sum(jnp.where(hit, (idx[None, :] + 1) % 2, 0), axis=1).astype(jnp.int32)
    fetch = (run_start & has_next[run_id]).astype(jnp.int32)
    next_expert = block_expert[jnp.minimum(next_first[run_id], n - 1)]
    return fetch, cast, cast_set, run_id % 2, next_expert


def _routed_experts(xs, block_expert, block_valid, wg, wu, wd):
    n_blocks = block_expert.shape[0]
    shape = (EXPERT_ROW_BUFFERS, MOE_BLOCK * ROW_SUBLANES, LANES)
    any_spec = pl.BlockSpec(memory_space=pl.ANY)
    grid_spec = pltpu.PrefetchScalarGridSpec(
        num_scalar_prefetch=7,
        grid=(n_blocks,),
        in_specs=[any_spec, any_spec, any_spec, any_spec],
        out_specs=any_spec,
        scratch_shapes=[pltpu.VMEM((D_MODEL, D_EXPERT), F32), pltpu.VMEM((D_MODEL, D_EXPERT), F32),
                        pltpu.VMEM((D_EXPERT, D_MODEL), F32),
                        pltpu.VMEM((2, D_MODEL, D_EXPERT), BF16), pltpu.VMEM((2, D_MODEL, D_EXPERT), BF16),
                        pltpu.VMEM((2, D_EXPERT, D_MODEL), BF16),
                        pltpu.VMEM(shape, F32), pltpu.VMEM(shape, F32),
                        pltpu.SemaphoreType.DMA((EXPERT_ROW_BUFFERS,)),
                        pltpu.SemaphoreType.DMA((EXPERT_ROW_BUFFERS,)),
                        pltpu.SemaphoreType.DMA((3,))],
    )
    return pl.pallas_call(
        functools.partial(_expert_kernel, n=n_blocks),
        grid_spec=grid_spec,
        out_shape=jax.ShapeDtypeStruct(xs.shape, F32),
        name="routed_experts",
        compiler_params=_params("arbitrary"),
    )(block_expert, block_valid, *_weight_schedule(block_expert), xs, wg, wu, wd)


def _combine_kernel(dest_ref, dest_next_ref, x1_ref, route_ref, g_ref, yb_hbm, o_ref, ybuf, sem, *, n):
    i = pl.program_id(0)
    slot = i % 2

    def row_copy(idx8, r, k, s):
        dst = ybuf.at[s, pl.ds((k * ROW_TILE + r) * ROW_SUBLANES, ROW_SUBLANES)]
        return pltpu.make_async_copy(_row_tile(yb_hbm, idx8), dst, sem.at[s])

    @pl.when(i == 0)
    def _():
        def body(r, c):
            idx = [dest_ref[0, k * ROW_TILE + r] for k in range(TOP_K)]
            for k in range(TOP_K):
                row_copy(idx[k], r, k, 0).start()
            return c
        lax.fori_loop(0, ROW_TILE, body, 0)

    @pl.when(i + 1 < n)
    def _():
        for r0 in range(0, ROW_TILE, INDEX_BATCH):
            idx = [[dest_next_ref[0, k * ROW_TILE + r] for r in range(r0, r0 + INDEX_BATCH)] for k in range(TOP_K)]
            for r in range(r0, r0 + INDEX_BATCH):
                for k in range(TOP_K):
                    row_copy(idx[k][r - r0], r, k, 1 - slot).start(priority=k % DMA_THREADS)

    _tile_wait(yb_hbm, ybuf.at[slot], sem.at[slot])
    route = route_ref[...]
    buf = ybuf.at[slot]
    y = (_load_rows(buf, ROW_TILE) * route[:, 2:3]
         + _load_rows(buf, ROW_TILE, offset=ROW_TILE * ROW_SUBLANES) * route[:, 3:4])
    x2 = x1_ref[...] + y
    ms = jnp.mean(x2 * x2, axis=-1, keepdims=True)
    o_ref[...] = x2 * lax.rsqrt(ms + EPS) * g_ref[...]


def _combine_and_norm(dest8, x1, route, norm_w, yb):
    t = x1.shape[0]
    n_tiles = t // ROW_TILE
    dest_spec = lambda f: pl.BlockSpec((None, 1, TILE_ROW_COPIES), lambda i: (f(i), 0, 0), memory_space=pltpu.SMEM)
    return pl.pallas_call(
        functools.partial(_combine_kernel, n=n_tiles),
        grid=(n_tiles,),
        in_specs=[dest_spec(lambda i: i), dest_spec(lambda i: jnp.minimum(i + 1, n_tiles - 1)),
                  pl.BlockSpec((ROW_TILE, D_MODEL), lambda i: (i, 0)),
                  pl.BlockSpec((ROW_TILE, LANES), lambda i: (i, 0)),
                  _const_spec((1, D_MODEL)),
                  pl.BlockSpec(memory_space=pl.ANY)],
        out_specs=pl.BlockSpec((ROW_TILE, D_MODEL), lambda i: (i, 0)),
        out_shape=jax.ShapeDtypeStruct((t, D_MODEL), F32),
        scratch_shapes=[pltpu.VMEM((2, TILE_ROW_COPIES * ROW_SUBLANES, LANES), F32),
                        pltpu.SemaphoreType.DMA((2,))],
        name="combine_and_norm",
        compiler_params=_params("arbitrary", disable_bounds_checks=True),
    )(dest8, dest8, x1, route, norm_w, yb)


def _dispatch_plan(route_t, counts):
    t = route_t.shape[1]
    counts = counts[0, 0:N_EXPERTS].astype(jnp.int32)
    padded = (counts + MOE_BLOCK - 1) // MOE_BLOCK * MOE_BLOCK
    pad_end = jnp.cumsum(padded)
    pad_start = pad_end - padded
    expert = route_t[0:TOP_K].astype(jnp.int32)
    rank = route_t[4:4 + TOP_K].astype(jnp.int32)
    start_of = sum(jnp.where(expert == e, pad_start[e], 0) for e in range(N_EXPERTS))
    dest8 = (start_of + rank) * ROW_SUBLANES
    dest8 = dest8.reshape(TOP_K, t // ROW_TILE, ROW_TILE).transpose(1, 0, 2).reshape(t // ROW_TILE, 1, TILE_ROW_COPIES)
    n_blocks = -(-(t * TOP_K + N_EXPERTS * (MOE_BLOCK - 1)) // MOE_BLOCK)
    block_start = jnp.arange(n_blocks, dtype=jnp.int32) * MOE_BLOCK
    block_expert = jnp.minimum(jnp.sum((pad_end[None, :] <= block_start[:, None]).astype(jnp.int32), axis=1),
                               N_EXPERTS - 1)
    block_valid = (block_start < pad_end[-1]).astype(jnp.int32)
    n_slots = n_blocks * MOE_BLOCK
    fill_start = jnp.concatenate([pad_start + counts, pad_end[-1:]])
    fill_n = jnp.concatenate([padded - counts, n_slots - pad_end[-1:]])
    return dest8, fill_start, fill_n, block_expert, block_valid, n_slots


def _rel_bias_band(rel_table, qc):
    rows = qc * CHUNK
    window = ATTN_REACH + rows
    n = window + rows - 1
    diag = rel_table[:, np.clip(ATTN_REACH + rows - 1 - np.arange(n), -MAX_REL, MAX_REL) + MAX_REL].astype(F32)
    shifted = jnp.tile(diag, (1, rows + 1))[:, :rows * (n + 1)].reshape(N_HEADS_A, rows, n + 1)[:, :, :window]
    bias = shifted[:, ::-1, :] * LOG2E
    i = np.arange(rows)[:, None]
    j = np.arange(window)[None, :]
    first = (i // CHUNK) * CHUNK
    return jnp.where((j >= first) & (j < first + BAND), bias, NEG_INF)


def _pad_lanes(v, width=LANES):
    return jnp.pad(v.astype(F32), (0, width - v.shape[0])).reshape(1, width)


def _mixers(x, kv_cache, conv_state, ssm_state, w):
    batch, l, _ = x.shape
    nc = l // CHUNK
    t = batch * l
    x2d = x.reshape(t, D_MODEL)
    proj, dt_raw = _in_projection(x2d, w["norm_mix"], w["w_proj"])
    proj3 = proj.reshape(batch, l, PROJ_WIDTH)
    k_new = lambda rows: proj3[:, rows, K0:K0 + ATTN_WIDTH].astype(F32)
    v_new = lambda rows: proj3[:, rows, V0:V0 + ATTN_WIDTH].astype(F32)
    new_part = lambda col: (proj, pl.BlockSpec((l, ATTN_WIDTH), lambda b, c: (b, col // ATTN_WIDTH)), l)
    if kv_cache is None:
        qc = ATTN_CHUNKS_PER_STEP if nc % ATTN_CHUNKS_PER_STEP == 0 else 1
        attn = _band_attention(proj, Q0 // ATTN_WIDTH, [new_part(K0)], [new_part(V0)],
                               _rel_bias_band(w["rel_table"], qc), batch=batch, nc=nc, pad_rows=ATTN_REACH, qc=qc)
        keep = slice(l - min(ATTN_REACH, l), l)
        k_keep, v_keep = k_new(keep), v_new(keep)
    else:
        assert nc == 1
        wlen = kv_cache[0].shape[1]
        cache_k, cache_v = (u.reshape(batch, wlen, ATTN_WIDTH) for u in kv_cache)
        cache_part = lambda u: (u, pl.BlockSpec((None, wlen, ATTN_WIDTH), lambda b, c: (b, 0, 0)), wlen)
        attn = _band_attention(proj, Q0 // ATTN_WIDTH, [cache_part(cache_k), new_part(K0)],
                               [cache_part(cache_v), new_part(V0)], _rel_bias_band(w["rel_table"], 1),
                               batch=batch, nc=nc, pad_rows=BAND - (wlen + l), qc=1)
        k_keep = jnp.concatenate([cache_k[:, l:], k_new(slice(None))], axis=1)
        v_keep = jnp.concatenate([cache_v[:, l:], v_new(slice(None))], axis=1)
    tail = proj3[:, l - (CONV_WIDTH - 1):, XBC0:XBC0 + CONV_DIM].astype(F32)
    if conv_state is None:
        conv0 = st0 = None
        conv_new = tail
    else:
        conv0 = jnp.pad(conv_state, ((0, 0), (CARRY_ROWS - (CONV_WIDTH - 1), 0), (0, 0)))
        st0 = jnp.transpose(ssm_state.reshape(batch, D_INNER, D_STATE), (0, 2, 1))
        conv_new = jnp.concatenate([conv_state, tail], axis=1)[:, -(CONV_WIDTH - 1):]
    ssm, st_t = _ssd_mixer(proj, dt_raw, conv0, st0, w["conv_w"], w["conv_b"], w["dt_bias"], w["a_log"], w["d_exp"],
                           w["ssm_norm"], batch=batch, nc=nc)
    ssm_new = jnp.transpose(st_t, (0, 2, 1)).reshape(batch, SSM_HEADS, SSM_HEAD_DIM, D_STATE)
    heads = lambda u: u.reshape(batch, u.shape[1], N_HEADS_A, HEAD_DIM_A)
    return (x2d, attn, ssm, proj), (heads(k_keep), heads(v_keep), conv_new, ssm_new)


def kernel(x_prompt, x_sample, cache_attn_k, cache_attn_v, state_conv, state_ssm, norm_mix, w_in, rel_bias, conv_w, conv_b, dt_bias, a_log, d_skip, ssm_norm, w_branch_a, w_branch_b, w_out, norm_ffn, w_router_group, b_router_group, w_router_expert, b_router_expert, w_gate, w_up, w_down, norm_final):
    assert norm_mix.shape[0] == 1, "single-layer trunk"
    q, k, v, z, xbc, dtc, ga, gb = jnp.split(w_in[0], np.cumsum(
        (ATTN_WIDTH, ATTN_WIDTH, ATTN_WIDTH, D_INNER, CONV_DIM, SSM_HEADS, D_MODEL, D_MODEL))[:-1].tolist(), axis=1)
    w_proj = jnp.concatenate([z, ga, gb, xbc, q, k, v, jnp.pad(dtc, ((0, 0), (0, LANES - SSM_HEADS)))],
                             axis=1).astype(BF16)
    w_router = jnp.pad(jnp.concatenate([w_router_group[0], w_router_expert[0]], axis=1),
                       ((0, 0), (0, LANES - N_EXPERT_GROUPS - N_EXPERTS)))
    w_router_hi = w_router.astype(BF16)
    w_router_lo = (w_router - w_router_hi.astype(F32)).astype(BF16)
    b_router = _pad_lanes(jnp.concatenate([b_router_group[0], b_router_expert[0]]))
    w = dict(
        norm_mix=norm_mix[0].reshape(1, D_MODEL), w_proj=w_proj, rel_table=rel_bias[0],
        conv_w=conv_w[0], conv_b=conv_b[0].reshape(1, CONV_DIM), dt_bias=_pad_lanes(dt_bias[0]),
        a_log=_pad_lanes(a_log[0]), d_exp=jnp.repeat(d_skip[0], SSM_HEAD_DIM).reshape(1, D_INNER),
        ssm_norm=ssm_norm[0].reshape(1, D_INNER),
    )
    mix_w = (w_branch_a[0].astype(BF16), w_branch_b[0].astype(BF16), w_out[0].astype(BF16),
             norm_ffn[0].reshape(1, D_MODEL), jnp.concatenate([w_router_hi, w_router_lo], axis=1), b_router)
    norm_out = norm_final.reshape(1, D_MODEL)

    groups = [_mixers(x_prompt, None, None, None, w),
              _mixers(x_sample, (cache_attn_k[0], cache_attn_v[0]), state_conv[0], state_ssm[0], w)]
    counts = jnp.zeros((1, LANES), F32)
    mixed = []
    routes_t = []
    for acts, _ in groups:
        x1, hn, route, route_t, counts = _mix_and_route(*acts, *mix_w, counts)
        mixed.append((x1, hn, route))
        routes_t.append(route_t)
    dest8, fill_start, fill_n, block_expert, block_valid, n_slots = _dispatch_plan(
        jnp.concatenate(routes_t, axis=1), counts)
    xs = _dispatch(dest8, fill_start, fill_n, [m[1] for m in mixed], n_slots)
    yb = _routed_experts(xs, block_expert, block_valid, w_gate[0], w_up[0], w_down[0])
    outs, first = [], 0
    for (x1, _, route), (acts, _) in zip(mixed, groups):
        n_tiles = x1.shape[0] // ROW_TILE
        y = _combine_and_norm(dest8[first:first + n_tiles], x1, route, norm_out, yb)
        first += n_tiles
        outs.append(y)
    (yp, ys), ((kp, vp, cp, sp), (ks, vs, cs, ss)) = outs, [g[1] for g in groups]
    yp = yp.reshape(x_prompt.shape)
    ys = ys.reshape(x_sample.shape)
    return (yp, ys, kp[None], vp[None], cp[None], sp[None], ks[None], vs[None], cs[None], ss[None])
```

```python
import functools
import math

import numpy as np
import jax
import jax.numpy as jnp
from jax import lax
from jax.experimental import pallas as pl
from jax.experimental.pallas import tpu as pltpu

F32 = jnp.float32
BF16 = jnp.bfloat16

D_MODEL = 1024
CHUNK = 64
LEFT_CHUNKS = 8
ATTN_REACH = LEFT_CHUNKS * CHUNK
BAND = ATTN_REACH + CHUNK
N_HEADS_A = 8
HEAD_DIM_A = 64
ATTN_WIDTH = N_HEADS_A * HEAD_DIM_A
MAX_REL = 256
ATTN_SCALE = 1.0 / math.sqrt(HEAD_DIM_A)
NEG_INF = -1e30
SSM_HEADS = 16
SSM_HEAD_DIM = 64
D_INNER = SSM_HEADS * SSM_HEAD_DIM
SSM_GROUPS = 2
GROUP_WIDTH = D_INNER // SSM_GROUPS
D_STATE = 128
CONV_WIDTH = 4
CONV_DIM = D_INNER + 2 * SSM_GROUPS * D_STATE
N_EXPERT_GROUPS = 4
EXPERTS_PER_GROUP = 8
N_EXPERTS = N_EXPERT_GROUPS * EXPERTS_PER_GROUP
TOP_K = 2
D_EXPERT = 512
MOE_BLOCK = 256
EPS = 1e-6

LANES = 128
SUBLANES = 8
Z0, GA0, GB0, XBC0, Q0, K0, V0 = 0, 1024, 2048, 3072, 4608, 5120, 5632
PROJ_WIDTH = V0 + ATTN_WIDTH
PROJ_TILE = 1024
ROW_TILE = 256
ATTN_CHUNKS_PER_STEP = 2
SSD_STREAMS = 4
VMEM_LIMIT = 56 * 1024 * 1024

_NT = (((1,), (1,)), ((), ()))


def _params(*sem, **kw):
    return pltpu.CompilerParams(dimension_semantics=sem, vmem_limit_bytes=VMEM_LIMIT, **kw)


def _const_spec(shape):
    nd = len(shape)
    return pl.BlockSpec(shape, lambda *_: (0,) * nd)


def _inproj_kernel(x_ref, g_ref, w_ref, o_ref, dt_ref):
    x = x_ref[...]
    ms = jnp.mean(x * x, axis=-1, keepdims=True)
    h = (x * lax.rsqrt(ms + EPS) * g_ref[...]).astype(BF16)
    for j in range(0, PROJ_WIDTH, PROJ_TILE):
        o_ref[:, j:j + PROJ_TILE] = jnp.dot(h, w_ref[:, j:j + PROJ_TILE], preferred_element_type=F32).astype(BF16)
    dt_ref[...] = jnp.dot(h, w_ref[:, PROJ_WIDTH:PROJ_WIDTH + LANES], preferred_element_type=F32)


def _in_projection(x2d, norm_w, w_proj):
    t = x2d.shape[0]
    return pl.pallas_call(
        _inproj_kernel,
        grid=(t // ROW_TILE,),
        in_specs=[pl.BlockSpec((ROW_TILE, D_MODEL), lambda i: (i, 0)),
                  _const_spec((1, D_MODEL)),
                  _const_spec((D_MODEL, PROJ_WIDTH + LANES))],
        out_specs=[pl.BlockSpec((ROW_TILE, PROJ_WIDTH), lambda i: (i, 0)),
                   pl.BlockSpec((ROW_TILE, LANES), lambda i: (i, 0))],
        out_shape=[jax.ShapeDtypeStruct((t, PROJ_WIDTH), BF16), jax.ShapeDtypeStruct((t, LANES), F32)],
        name="in_projection",
        compiler_params=_params("parallel"),
    )(x2d, norm_w, w_proj)


LOG2E = math.log2(math.e)
ATTN_STAGES = 3


def _attn_kernel(q_ref, *refs, pad_rows, part_rows, qc, roll_out):
    n_parts = len(part_rows)
    k_parts, v_parts = refs[:n_parts], refs[n_parts:2 * n_parts]
    bias_ref, o_ref = refs[2 * n_parts:2 * n_parts + 2]
    rolled = refs[2 * n_parts + 2:-2]
    kpad, vpad = refs[-2:]
    rows = qc * CHUNK
    window = ATTN_REACH + rows
    c = pl.program_id(1)

    @pl.when(c == 0)
    def _():
        if pad_rows:
            kpad[0:pad_rows, :] = jnp.zeros((pad_rows, ATTN_WIDTH), BF16)
            vpad[0:pad_rows, :] = jnp.zeros((pad_rows, ATTN_WIDTH), BF16)
        first = pad_rows
        for k_ref, v_ref, n in zip(k_parts, v_parts, part_rows):
            kpad[first:first + n, :] = k_ref[...].astype(BF16)
            vpad[first:first + n, :] = v_ref[...].astype(BF16)
            first += n
        if roll_out:
            kept, new = part_rows[0] - part_rows[1], part_rows[1]
            for out_ref, parts in zip(rolled, (k_parts, v_parts)):
                out_ref[0:kept, :] = parts[0][new:new + kept, :]
                out_ref[kept:kept + new, :] = parts[1][...].astype(F32)

    start = pl.multiple_of(c * rows, rows)

    def body(mask_start):
        kw = kpad[pl.ds(start, window), :]
        vw = vpad[pl.ds(start, window), :]
        q = (q_ref[...].astype(F32) * (ATTN_SCALE * LOG2E)).astype(BF16)
        low = lax.broadcasted_iota(jnp.int32, (rows, LANES), 1) < HEAD_DIM_A
        if mask_start:
            valid = lax.broadcasted_iota(jnp.int32, (rows, window), 1) + start >= pad_rows
        outs = {}

        def head(h):
            hp, sub = divmod(h, 2)
            sl = slice(LANES * hp, LANES * (hp + 1))
            q2 = q[:, sl]
            qm = jnp.where(low if sub == 0 else jnp.logical_not(low), q2, jnp.zeros_like(q2))
            s = lax.dot_general(qm, kw[:, sl], _NT, preferred_element_type=F32) + bias_ref[h]
            if mask_start:
                s = jnp.where(valid, s, NEG_INF)
            yield
            m = jnp.max(s, axis=-1, keepdims=True)
            p = jnp.exp2(s - m)
            l = jnp.sum(p, axis=-1, keepdims=True)
            pb = p.astype(BF16)
            yield
            outs[h] = jnp.dot(pb, vw[:, sl], preferred_element_type=F32) / l
            if sub == 1:
                o_ref[:, sl] = jnp.where(low, outs[h - 1], outs[h]).astype(o_ref.dtype)

        heads = [head(h) for h in range(N_HEADS_A)]
        for t in range(N_HEADS_A + ATTN_STAGES - 1):
            for h in range(min(t, N_HEADS_A - 1), max(t - ATTN_STAGES, -1), -1):
                next(heads[h], None)

    if pad_rows:
        pl.when(start < pad_rows)(lambda: body(True))
        pl.when(start >= pad_rows)(lambda: body(False))
    else:
        body(False)


def _band_attention(q_arr, q_col, k_parts, v_parts, bias, *, batch, nc, pad_rows, qc, roll_out=False):
    steps = nc // qc
    rows = qc * CHUNK
    part_rows = tuple(p[2] for p in k_parts)
    lk = sum(part_rows)
    kern = functools.partial(_attn_kernel, pad_rows=pad_rows, part_rows=part_rows, qc=qc, roll_out=roll_out)
    out_specs = [pl.BlockSpec((rows, ATTN_WIDTH), lambda b, c: (b * steps + c, 0))]
    out_shape = [jax.ShapeDtypeStruct((batch * nc * CHUNK, ATTN_WIDTH), BF16)]
    if roll_out:
        assert len(part_rows) == 2 and nc == 1
        out_specs += [pl.BlockSpec((None, part_rows[0], ATTN_WIDTH), lambda b, c: (b, 0, 0))] * 2
        out_shape += [jax.ShapeDtypeStruct((batch, part_rows[0], ATTN_WIDTH), F32)] * 2
    return pl.pallas_call(
        kern,
        grid=(batch, steps),
        in_specs=[pl.BlockSpec((rows, ATTN_WIDTH), lambda b, c: (b * steps + c, q_col))]
        + [p[1] for p in k_parts] + [p[1] for p in v_parts]
        + [_const_spec((N_HEADS_A, rows, ATTN_REACH + rows))],
        out_specs=out_specs,
        out_shape=out_shape,
        scratch_shapes=[pltpu.VMEM((pad_rows + lk, ATTN_WIDTH), BF16),
                        pltpu.VMEM((pad_rows + lk, ATTN_WIDTH), BF16)],
        name="band_attention",
        compiler_params=_params("parallel", "arbitrary"),
    )(q_arr, *[p[0] for p in k_parts], *[p[0] for p in v_parts], bias)


CARRY_ROWS = SUBLANES
SHIFT_TAPS = CONV_WIDTH - 1
SHIFT_K = 2 * CARRY_ROWS + CHUNK


def _split_bf16(x, parts):
    out = []
    for _ in range(parts - 1):
        h = x.astype(BF16)
        out.append(h)
        x = x - h.astype(F32)
    out.append(x.astype(BF16))
    return out


def _ssd_kernel(xbc_ref, z_ref, dt_ref, conv0_ref, st0_ref, cw_ref, cb_ref, dtb_ref, alog_ref,
                dexp_ref, nw_ref, tri_ref, exp_ref, sel_ref, eye_ref, shift_ref,
                y_ref, stout_ref, carry, st, *, has_init, streams, nc):
    c = pl.program_id(1)

    @pl.when(c == 0)
    def _():
        if has_init:
            carry[...] = conv0_ref[...]
            st[...] = st0_ref[...]
        else:
            carry[...] = jnp.zeros((streams, CARRY_ROWS, CONV_DIM), F32)
            st[...] = jnp.zeros((streams, D_STATE, D_INNER), F32)

    chunks = [_ssd_chunk(xbc_ref.at[s], z_ref.at[s], dt_ref.at[s], cw_ref, cb_ref, dtb_ref, alog_ref, dexp_ref,
                         nw_ref, tri_ref, exp_ref, sel_ref, eye_ref, shift_ref, y_ref.at[s], carry.at[s], st.at[s])
              for s in range(streams)]
    while chunks:
        chunks = [g for g in chunks if next(g, True) is None]

    @pl.when(c == nc - 1)
    def _():
        stout_ref[...] = st[...]


def _ssd_chunk(xbc_ref, z_ref, dt_ref, cw_ref, cb_ref, dtb_ref, alog_ref, dexp_ref, nw_ref,
               tri_ref, exp_ref, sel_ref, eye_ref, shift_ref, y_ref, carry, st):
    prev = carry[...]
    prev_hi = prev.astype(BF16).astype(F32)
    prev_parts = jnp.concatenate([prev_hi, prev - prev_hi], axis=0).astype(BF16)
    taps = jnp.dot(shift_ref[...], jnp.concatenate([prev_parts, xbc_ref[...]], axis=0), preferred_element_type=F32)

    dt_in = dt_ref[...] + dtb_ref[...]
    dt = jnp.maximum(dt_in, 0.0) + jnp.log1p(jnp.exp(-jnp.abs(dt_in)))
    da = dt * (-jnp.exp(alog_ref[...]))
    tri = tri_ref[...]
    cs3 = jnp.dot(tri, jnp.concatenate(_split_bf16(da, 3), axis=1), preferred_element_type=F32)
    yield

    cs = cs3[:, 0:LANES] + cs3[:, LANES:2 * LANES] + cs3[:, 2 * LANES:3 * LANES]
    cs_last = cs[CHUNK - 1:CHUNK, :]
    ecs = jnp.exp(cs)
    to_end = jnp.exp(cs_last - cs)

    stack = jnp.concatenate(_split_bf16(dt, 1) + _split_bf16(to_end, 1) + _split_bf16(ecs, 2)
                            + _split_bf16(cs, 3), axis=0)
    wide = jnp.dot(stack, exp_ref[...], preferred_element_type=F32)

    lane = lax.broadcasted_iota(jnp.int32, (CHUNK, LANES), 1)
    even = (lane % 2) == 0
    csm = jnp.concatenate([jnp.where(even, cs, 0.0), jnp.where(even, 0.0, cs)], axis=0)
    srow3 = lax.dot_general(sel_ref[...], jnp.concatenate(_split_bf16(csm, 3), axis=0), _NT,
                            preferred_element_type=F32)
    yield

    rows = [wide[i * CHUNK:(i + 1) * CHUNK, :] for i in range(7)]
    dt_w = rows[0]
    to_end_w = rows[1]
    ecs_w = rows[2] + rows[3]
    cs_w = rows[4] + rows[5] + rows[6]
    srow = srow3[:, 0:LANES] + srow3[:, LANES:2 * LANES] + srow3[:, 2 * LANES:3 * LANES]

    xdt_b, xw_b, skip, bc = [], [], [], []
    for j in range(CONV_DIM // LANES):
        sl = slice(j * LANES, (j + 1) * LANES)
        conv = cb_ref[:, sl] + xbc_ref[:, sl].astype(F32) * cw_ref[CONV_WIDTH - 1:CONV_WIDTH, sl]
        for tap in range(SHIFT_TAPS):
            conv = conv + taps[tap * CHUNK:(tap + 1) * CHUNK, sl] * cw_ref[tap:tap + 1, sl]
        u = conv * jax.nn.sigmoid(conv)
        if j < D_INNER // LANES:
            xdt = u * dt_w[:, sl]
            xdt_b.append(xdt.astype(BF16))
            xw_b.append((xdt * to_end_w[:, sl]).astype(BF16))
            skip.append(u * dexp_ref[:, sl])
        else:
            bc.append(u.astype(BF16))
    low = lane < SSM_HEAD_DIM
    t_idx = lax.broadcasted_iota(jnp.int32, (CHUNK, LANES), 0)
    causal = t_idx >= (lane % SSM_HEAD_DIM)
    eye = eye_ref[...]
    pairs = SSM_HEADS // SSM_GROUPS // 2
    yield

    for g in range(SSM_GROUPS):
        b_g, c_g = bc[g], bc[SSM_GROUPS + g]
        gsl = slice(g * GROUP_WIDTH, (g + 1) * GROUP_WIDTH)
        st_g = st[:, gsl]
        y_off = jnp.dot(c_g, st_g.astype(BF16), preferred_element_type=F32)
        b_t = lax.dot_general(eye, b_g, _NT, preferred_element_type=F32).astype(BF16)
        xw_g = jnp.concatenate(xw_b[g * pairs:(g + 1) * pairs], axis=1)
        st[:, gsl] = st_g * ecs_w[CHUNK - 1:CHUNK, gsl] + jnp.dot(b_t, xw_g, preferred_element_type=F32)
        cb2 = lax.dot_general(c_g, jnp.concatenate([b_g, b_g], axis=0), _NT, preferred_element_type=F32)
        yield
        gated, sumsq = [], 0.0
        for kk in range(pairs):
            k = g * pairs + kk
            psl = slice(k * LANES, (k + 1) * LANES)
            decay = jnp.where(causal, jnp.exp(cs_w[:, psl] - srow[k:k + 1, :]), 0.0)
            gmat = (cb2 * decay).astype(BF16)
            xp = xdt_b[k]
            xblk = jnp.concatenate([jnp.where(low, xp, jnp.zeros_like(xp)),
                                    jnp.where(low, jnp.zeros_like(xp), xp)], axis=0)
            y = (jnp.dot(gmat, xblk, preferred_element_type=F32)
                 + y_off[:, kk * LANES:(kk + 1) * LANES] * ecs_w[:, psl] + skip[k])
            zz = z_ref[:, psl].astype(F32)
            y = y * (zz * jax.nn.sigmoid(zz))
            gated.append(y)
            sumsq = sumsq + jnp.sum(y * y, axis=-1, keepdims=True)
        scale = lax.rsqrt(sumsq * (1.0 / GROUP_WIDTH) + EPS)
        for kk in range(pairs):
            psl = slice((g * pairs + kk) * LANES, (g * pairs + kk + 1) * LANES)
            y_ref[:, psl] = (gated[kk] * scale * nw_ref[:, psl]).astype(y_ref.dtype)
        yield
    carry[...] = xbc_ref[CHUNK - 2 * CARRY_ROWS:CHUNK, :].astype(F32)[CARRY_ROWS:, :]


def _ssd_constants():
    tri = np.tril(np.ones((CHUNK, CHUNK), np.float32))
    expand = np.zeros((LANES, D_INNER), np.float32)
    for e in range(SSM_HEADS):
        expand[e, e * SSM_HEAD_DIM:(e + 1) * SSM_HEAD_DIM] = 1.0
    sel = np.zeros((16, LANES), np.float32)
    for e in range(SSM_HEADS):
        sel[e // 2, e] = 1.0
    eye = np.eye(D_STATE, dtype=np.float32)
    shift = np.zeros((SHIFT_TAPS * CHUNK, SHIFT_K), np.float32)
    for j in range(SHIFT_TAPS):
        for t in range(CHUNK):
            m = t - (CONV_WIDTH - 1) + j
            if m >= 0:
                shift[j * CHUNK + t, 2 * CARRY_ROWS + m] = 1.0
            else:
                shift[j * CHUNK + t, CARRY_ROWS + m] = 1.0
                shift[j * CHUNK + t, 2 * CARRY_ROWS + m] = 1.0
    return tuple(jnp.asarray(a, BF16) for a in (tri, expand, sel, eye, shift))


def _ssd_mixer(proj, dt_raw, conv0, st0, conv_w, conv_b, dt_bias_p, a_log_p, d_exp, norm_w, *, batch, nc):
    streams = SSD_STREAMS if batch % SSD_STREAMS == 0 else 1
    groups = batch // streams
    l = nc * CHUNK
    has_init = conv0 is not None
    if not has_init:
        conv0 = jnp.zeros((1, streams, CARRY_ROWS, CONV_DIM), F32)
        st0 = jnp.zeros((1, streams, D_STATE, D_INNER), F32)
        init_map = lambda b, c: (0, 0, 0, 0)
    else:
        conv0 = conv0.reshape(groups, streams, CARRY_ROWS, CONV_DIM)
        st0 = st0.reshape(groups, streams, D_STATE, D_INNER)
        init_map = lambda b, c: (b, 0, 0, 0)
    tri, expand, sel, eye, shift = _ssd_constants()
    proj4 = proj.reshape(groups, streams, l, PROJ_WIDTH)
    dt4 = dt_raw.reshape(groups, streams, l, LANES)
    chunk_of = lambda w, col: pl.BlockSpec((None, streams, CHUNK, w), lambda b, c: (b, 0, c, col))
    state_spec = lambda m: pl.BlockSpec((None, streams, D_STATE, D_INNER), m)
    kern = functools.partial(_ssd_kernel, has_init=has_init, streams=streams, nc=nc)
    y, st_out = pl.pallas_call(
        kern,
        grid=(groups, nc),
        in_specs=[chunk_of(CONV_DIM, XBC0 // CONV_DIM), chunk_of(D_INNER, Z0 // D_INNER), chunk_of(LANES, 0),
                  pl.BlockSpec((None, streams, CARRY_ROWS, CONV_DIM), init_map), state_spec(init_map),
                  _const_spec((CONV_WIDTH, CONV_DIM)), _const_spec((1, CONV_DIM)),
                  _const_spec((1, LANES)), _const_spec((1, LANES)),
                  _const_spec((1, D_INNER)), _const_spec((1, D_INNER)),
                  _const_spec((CHUNK, CHUNK)), _const_spec((LANES, D_INNER)),
                  _const_spec((16, LANES)), _const_spec((D_STATE, D_STATE)),
                  _const_spec((SHIFT_TAPS * CHUNK, SHIFT_K))],
        out_specs=[chunk_of(D_INNER, 0), state_spec(lambda b, c: (b, 0, 0, 0))],
        out_shape=[jax.ShapeDtypeStruct((groups, streams, l, D_INNER), BF16),
                   jax.ShapeDtypeStruct((groups, streams, D_STATE, D_INNER), F32)],
        scratch_shapes=[pltpu.VMEM((streams, CARRY_ROWS, CONV_DIM), F32),
                        pltpu.VMEM((streams, D_STATE, D_INNER), F32)],
        name="ssd_mixer",
        compiler_params=_params("parallel", "arbitrary"),
    )(proj4, proj4, dt4, conv0, st0, conv_w, conv_b, dt_bias_p, a_log_p, d_exp, norm_w, tri, expand, sel, eye, shift)
    return y.reshape(batch * l, D_INNER), st_out.reshape(batch, D_STATE, D_INNER)


ROW_SUBLANES = D_MODEL // LANES
assert ROW_SUBLANES == SUBLANES


def _store_rows(ref, x, rows, offset=0):
    for s in range(ROW_SUBLANES):
        ref[pl.ds(offset + s, rows, stride=ROW_SUBLANES), :] = x[:, s * LANES:(s + 1) * LANES]


def _load_rows(ref, rows, offset=0):
    return jnp.concatenate([ref[pl.ds(offset + s, rows, stride=ROW_SUBLANES), :] for s in range(ROW_SUBLANES)],
                           axis=1)


def _row_tile(ref, idx8):
    return ref.at[pl.ds(pl.multiple_of(idx8, ROW_SUBLANES), ROW_SUBLANES)]


def _mix_kernel(x_ref, attn_ref, ssm_ref, ga_ref, gb_ref, wa_ref, wb_ref, wo_ref, g_ref, wr_ref, br_ref, slt_ref,
                count0_ref, x1_ref, hn_ref, route_ref, route_t_ref, counts_ref, carry, *, subtiles):
    @pl.when(pl.program_id(0) == 0)
    def _():
        carry[...] = count0_ref[...]

    running = {"counts": carry[...]}
    tiles = [_mix_tile(j * ROW_TILE, running, x_ref, attn_ref, ssm_ref, ga_ref, gb_ref, wa_ref, wb_ref, wo_ref, g_ref,
                       wr_ref, br_ref, slt_ref, x1_ref, hn_ref, route_ref, route_t_ref) for j in range(subtiles)]
    while tiles:
        tiles = [g for g in tiles if next(g, True) is None]
    carry[...] = running["counts"]
    counts_ref[...] = running["counts"]


def _mix_tile(r0, running, x_ref, attn_ref, ssm_ref, ga_ref, gb_ref, wa_ref, wb_ref, wo_ref, g_ref, wr_ref, br_ref,
              slt_ref, x1_ref, hn_ref, route_ref, route_t_ref):
    rows = slice(r0, r0 + ROW_TILE)
    a = jnp.dot(attn_ref[rows, :], wa_ref[...], preferred_element_type=F32)
    s = jnp.dot(ssm_ref[rows, :], wb_ref[...], preferred_element_type=F32)
    yield
    mixed = jax.nn.sigmoid(ga_ref[rows, :].astype(F32)) * a + jax.nn.sigmoid(gb_ref[rows, :].astype(F32)) * s
    x1 = x_ref[rows, :] + jnp.dot(mixed.astype(BF16), wo_ref[...], preferred_element_type=F32)
    yield
    x1_ref[rows, :] = x1
    ms = jnp.mean(x1 * x1, axis=-1, keepdims=True)
    hn = x1 * lax.rsqrt(ms + EPS) * g_ref[...]
    _store_rows(hn_ref, hn, ROW_TILE, offset=r0 * ROW_SUBLANES)

    hi = hn.astype(BF16)
    lo = (hn - hi.astype(F32)).astype(BF16)
    both = jnp.dot(hi, wr_ref[...], preferred_element_type=F32)
    lo_part = jnp.dot(lo, wr_ref[:, 0:LANES], preferred_element_type=F32)
    yield
    logits = both[:, 0:LANES] + both[:, LANES:2 * LANES] + lo_part + br_ref[...]
    lane = lax.broadcasted_iota(jnp.int32, logits.shape, 1)
    lane_f = lane.astype(F32)
    big = float(LANES)
    is_g = lane < N_EXPERT_GROUPS
    gl = jnp.where(is_g, logits, -jnp.inf)
    gmax = jnp.max(gl, axis=-1, keepdims=True)
    gidx = jnp.min(jnp.where(gl == gmax, lane_f, big), axis=-1, keepdims=True)
    gsum = jnp.sum(jnp.where(is_g, jnp.exp(gl - gmax), 0.0), axis=-1, keepdims=True)
    g_top = 1.0 / gsum
    first = N_EXPERT_GROUPS + gidx * EXPERTS_PER_GROUP
    in_group = (lane_f >= first) & (lane_f < first + EXPERTS_PER_GROUP)
    el = jnp.where(in_group, logits, -jnp.inf)
    m1 = jnp.max(el, axis=-1, keepdims=True)
    i1 = jnp.min(jnp.where(el == m1, lane_f, big), axis=-1, keepdims=True)
    el2 = jnp.where(lane_f == i1, -jnp.inf, el)
    m2 = jnp.max(el2, axis=-1, keepdims=True)
    i2 = jnp.min(jnp.where(el2 == m2, lane_f, big), axis=-1, keepdims=True)
    r = jnp.exp(m2 - m1)
    w1 = g_top / (1.0 + r)
    w2 = g_top * r / (1.0 + r)
    e1 = i1 - N_EXPERT_GROUPS
    e2 = i2 - N_EXPERT_GROUPS

    oh1 = lane_f == e1
    oh2 = lane_f == e2
    hot1 = jnp.where(oh1, 1.0, 0.0)
    hot2 = jnp.where(oh2, 1.0, 0.0)
    onehots = jnp.concatenate([hot1, hot2], axis=1).astype(BF16)
    before = jnp.dot(slt_ref[...], onehots, preferred_element_type=F32)
    cnt1 = jnp.sum(hot1, axis=0, keepdims=True)
    cnt2 = jnp.sum(hot2, axis=0, keepdims=True)
    yield
    base = running["counts"]
    rank1 = jnp.sum(jnp.where(oh1, before[:, 0:LANES] + base, 0.0), axis=-1, keepdims=True)
    rank2 = jnp.sum(jnp.where(oh2, before[:, LANES:2 * LANES] + (base + cnt1), 0.0), axis=-1, keepdims=True)
    running["counts"] = base + cnt1 + cnt2

    route = jnp.zeros_like(logits)
    for col, val in enumerate((e1, e2, w1, w2, rank1, rank2)):
        route = jnp.where(lane == col, val, route)
    route_ref[rows, :] = route
    route_t_ref[:, rows] = jnp.transpose(route)[0:SUBLANES, :]


MIX_SUBTILES = 2


def _mix_and_route(x2d, attn, ssm, proj, wa, wb, wo, norm_w, w_router, b_router, count0):
    t = x2d.shape[0]
    subtiles = MIX_SUBTILES if t % (MIX_SUBTILES * ROW_TILE) == 0 else 1
    step_rows = subtiles * ROW_TILE
    tile = lambda w, col: pl.BlockSpec((step_rows, w), lambda i: (i, col))
    slt = jnp.asarray(np.tril(np.ones((ROW_TILE, ROW_TILE), np.float32), k=-1), BF16)
    return pl.pallas_call(
        functools.partial(_mix_kernel, subtiles=subtiles),
        grid=(t // step_rows,),
        in_specs=[tile(D_MODEL, 0), tile(ATTN_WIDTH, 0), tile(D_INNER, 0),
                  tile(D_MODEL, GA0 // D_MODEL), tile(D_MODEL, GB0 // D_MODEL),
                  _const_spec((ATTN_WIDTH, D_MODEL)), _const_spec((D_INNER, D_MODEL)),
                  _const_spec((D_MODEL, D_MODEL)), _const_spec((1, D_MODEL)),
                  _const_spec((D_MODEL, 2 * LANES)), _const_spec((1, LANES)),
                  _const_spec((ROW_TILE, ROW_TILE)), _const_spec((1, LANES))],
        out_specs=[tile(D_MODEL, 0), pl.BlockSpec((step_rows * ROW_SUBLANES, LANES), lambda i: (i, 0)),
                   tile(LANES, 0), pl.BlockSpec((SUBLANES, step_rows), lambda i: (0, i)), _const_spec((1, LANES))],
        out_shape=[jax.ShapeDtypeStruct((t, D_MODEL), F32),
                   jax.ShapeDtypeStruct((t * ROW_SUBLANES, LANES), F32),
                   jax.ShapeDtypeStruct((t, LANES), F32),
                   jax.ShapeDtypeStruct((SUBLANES, t), F32),
                   jax.ShapeDtypeStruct((1, LANES), F32)],
        scratch_shapes=[pltpu.VMEM((1, LANES), F32)],
        name="mix_and_route",
        compiler_params=_params("arbitrary"),
    )(x2d, attn, ssm, proj, proj, wa, wb, wo, norm_w, w_router, b_router, slt, count0)


TILE_ROW_COPIES = ROW_TILE * TOP_K
INDEX_BATCH = 8
DISPATCH_BUFFERS = 3
DMA_THREADS = 2


def _tile_wait(ref_a, ref_b, sem):
    n = TILE_ROW_COPIES * ROW_SUBLANES
    pltpu.make_async_copy(ref_a.at[pl.ds(0, n)], ref_b.at[pl.ds(0, n)], sem).wait()


def _dispatch_kernel(fill_start_ref, fill_n_ref, dest_ref, *refs, tiles_per_source, fill_rows):
    n_src = len(tiles_per_source)
    sources = refs[:n_src]
    xs_hbm, hbuf, zbuf, in_sem, out_sem, fill_sem = refs[n_src:]
    i = pl.program_id(0)
    n = pl.num_programs(0)
    tile_rows = ROW_TILE * ROW_SUBLANES

    def tile_copy(src, j, b):
        return pltpu.make_async_copy(src.at[pl.ds(pl.multiple_of(j * tile_rows, tile_rows), tile_rows)],
                                     hbuf.at[b], in_sem.at[b])

    def tile_in_start(j):
        b = j % DISPATCH_BUFFERS
        first = 0
        for src, count in zip(sources, tiles_per_source):
            pl.when((j >= first) & (j < first + count))(lambda src=src, first=first: tile_copy(src, j - first, b).start())
            first += count

    @pl.when(i == 0)
    def _():
        zbuf[...] = jnp.zeros((ROW_SUBLANES, LANES), F32)
        tile_in_start(i)
        pl.when(n > 1)(lambda: tile_in_start(i + 1))

    buf = i % DISPATCH_BUFFERS
    tile_copy(sources[0], 0, buf).wait()
    for r0 in range(0, ROW_TILE, INDEX_BATCH):
        dests = [[dest_ref[0, k * ROW_TILE + r] for r in range(r0, r0 + INDEX_BATCH)] for k in range(TOP_K)]
        for r in range(r0, r0 + INDEX_BATCH):
            src = hbuf.at[buf, pl.ds(r * ROW_SUBLANES, ROW_SUBLANES)]
            for k in range(TOP_K):
                dst = _row_tile(xs_hbm, dests[k][r - r0])
                pltpu.make_async_copy(src, dst, out_sem.at[buf]).start(priority=k % DMA_THREADS)

    @pl.when(i > 0)
    def _():
        _tile_wait(xs_hbm, xs_hbm, out_sem.at[(i - 1) % DISPATCH_BUFFERS])

    pl.when(i + 2 < n)(lambda: tile_in_start(i + 2))

    @pl.when(i == n - 1)
    def _():
        _tile_wait(xs_hbm, xs_hbm, out_sem.at[buf])

        def per_range(e, carry):
            first = fill_start_ref[e]

            def start(j, c):
                pltpu.make_async_copy(zbuf, _row_tile(xs_hbm, (first + j) * ROW_SUBLANES), fill_sem.at[0]).start()
                return c

            lax.fori_loop(0, fill_n_ref[e], start, 0)
            return carry

        lax.fori_loop(0, N_EXPERTS + 1, per_range, 0)
        filled = pl.ds(0, fill_rows * ROW_SUBLANES)
        pltpu.make_async_copy(xs_hbm.at[filled], xs_hbm.at[filled], fill_sem.at[0]).wait()


def _dispatch(dest8, fill_start, fill_n, hn_list, n_slots):
    n_tiles = dest8.shape[0]
    tiles_per_source = tuple(h.shape[0] // (ROW_TILE * ROW_SUBLANES) for h in hn_list)
    assert sum(tiles_per_source) == n_tiles
    grid_spec = pltpu.PrefetchScalarGridSpec(
        num_scalar_prefetch=2,
        grid=(n_tiles,),
        in_specs=[pl.BlockSpec((None, 1, TILE_ROW_COPIES), lambda i, fs, fn: (i, 0, 0), memory_space=pltpu.SMEM)]
        + [pl.BlockSpec(memory_space=pl.ANY)] * len(hn_list),
        out_specs=pl.BlockSpec(memory_space=pl.ANY),
        scratch_shapes=[pltpu.VMEM((DISPATCH_BUFFERS, ROW_TILE * ROW_SUBLANES, LANES), F32),
                        pltpu.VMEM((ROW_SUBLANES, LANES), F32),
                        pltpu.SemaphoreType.DMA((DISPATCH_BUFFERS,)), pltpu.SemaphoreType.DMA((DISPATCH_BUFFERS,)),
                        pltpu.SemaphoreType.DMA((1,))],
    )
    return pl.pallas_call(
        functools.partial(_dispatch_kernel, tiles_per_source=tiles_per_source,
                          fill_rows=n_slots - n_tiles * TILE_ROW_COPIES),
        grid_spec=grid_spec,
        out_shape=jax.ShapeDtypeStruct((n_slots * ROW_SUBLANES, LANES), F32),
        name="dispatch_rows",
        compiler_params=_params("arbitrary", disable_bounds_checks=True),
    )(fill_start, fill_n, dest8, *hn_list)


EXPERT_ROW_BUFFERS = 3


def _expert_kernel(bexp_ref, bval_ref, fetch_ref, cast_ref, cset_ref, set_ref, next_ref, xs_hbm, wg_hbm, wu_hbm, wd_hbm, y_hbm,
                   wg_f, wu_f, wd_f, wg_b, wu_b, wd_b, xbuf, ybuf, in_sem, out_sem, w_sem, *, n):
    i = pl.program_id(0)
    block_rows = MOE_BLOCK * ROW_SUBLANES
    buf = i % EXPERT_ROW_BUFFERS

    def weight_copies(e):
        return [pltpu.make_async_copy(src.at[e], dst, w_sem.at[j])
                for j, (src, dst) in enumerate(((wg_hbm, wg_f), (wu_hbm, wu_f), (wd_hbm, wd_f)))]

    def cast_staged(which):
        for c in weight_copies(0):
            c.wait()
        wg_b[which] = wg_f[...].astype(BF16)
        wu_b[which] = wu_f[...].astype(BF16)
        wd_b[which] = wd_f[...].astype(BF16)

    def block_of(ref, j):
        return ref.at[pl.ds(pl.multiple_of(j * block_rows, block_rows), block_rows)]

    def copy_in(j):
        b = j % EXPERT_ROW_BUFFERS
        return pltpu.make_async_copy(block_of(xs_hbm, j), xbuf.at[b], in_sem.at[b])

    def copy_out(j):
        b = j % EXPERT_ROW_BUFFERS
        return pltpu.make_async_copy(ybuf.at[b], block_of(y_hbm, j), out_sem.at[b])

    @pl.when(i == 0)
    def _():
        for j in range(min(EXPERT_ROW_BUFFERS - 1, n)):
            copy_in(j).start()

    ahead = i + EXPERT_ROW_BUFFERS - 1
    pl.when(ahead < n)(lambda: copy_in(ahead).start())

    @pl.when(i == 0)
    def _():
        for c in weight_copies(bexp_ref[0]):
            c.start()
        cast_staged(0)

    pl.when(cast_ref[i] == 1)(lambda: cast_staged(cset_ref[i]))

    @pl.when(fetch_ref[i] == 1)
    def _():
        for c in weight_copies(next_ref[i]):
            c.start()

    which = set_ref[i]
    copy_in(i).wait()
    behind = i - (EXPERT_ROW_BUFFERS - 1)
    pl.when(behind >= 0)(lambda: copy_out(behind).wait())

    @pl.when(bval_ref[i] > 0)
    def _():
        xb = _load_rows(xbuf.at[buf], MOE_BLOCK).astype(BF16)
        gate = jnp.dot(xb, wg_b[which], preferred_element_type=F32)
        up = jnp.dot(xb, wu_b[which], preferred_element_type=F32)
        hid = (gate * jax.nn.sigmoid(gate) * up).astype(BF16)
        _store_rows(ybuf.at[buf], jnp.dot(hid, wd_b[which], preferred_element_type=F32), MOE_BLOCK)

    @pl.when(bval_ref[i] == 0)
    def _():
        ybuf[buf] = jnp.zeros((block_rows, LANES), F32)

    copy_out(i).start()

    @pl.when(i == n - 1)
    def _():
        for back in range(EXPERT_ROW_BUFFERS - 1):
            pl.when(i - back >= 0)(lambda back=back: copy_out(i - back).wait())


WEIGHT_CAST_LAG = 3


def _weight_schedule(block_expert):
    n = block_expert.shape[0]
    idx = jnp.arange(n, dtype=jnp.int32)
    run_start = jnp.concatenate([jnp.ones((1,), bool), block_expert[1:] != block_expert[:-1]])
    run_id = jnp.cumsum(run_start.astype(jnp.int32)) - 1
    run_first = jnp.min(jnp.where(run_id[None, :] == idx[:, None], idx[None, :], n), axis=1)
    next_first = jnp.concatenate([run_first[1:], jnp.full((1,), n, jnp.int32)])
    has_next = next_first < n
    cast_at = jnp.where(has_next, jnp.minimum(run_first + WEIGHT_CAST_LAG, next_first), -1)
    hit = idx[:, None] == cast_at[None, :]
    cast = jnp.any(hit, axis=1).astype(jnp.int32)
    cast_set = jnp.sum(jnp.where(hit, (idx[None, :] + 1) % 2, 0), axis=1).astype(jnp.int32)
    fetch = (run_start & has_next[run_id]).astype(jnp.int32)
    next_expert = block_expert[jnp.minimum(next_first[run_id], n - 1)]
    return fetch, cast, cast_set, run_id % 2, next_expert


def _routed_experts(xs, block_expert, block_valid, wg, wu, wd):
    n_blocks = block_expert.shape[0]
    shape = (EXPERT_ROW_BUFFERS, MOE_BLOCK * ROW_SUBLANES, LANES)
    any_spec = pl.BlockSpec(memory_space=pl.ANY)
    grid_spec = pltpu.PrefetchScalarGridSpec(
        num_scalar_prefetch=7,
        grid=(n_blocks,),
        in_specs=[any_spec, any_spec, any_spec, any_spec],
        out_specs=any_spec,
        scratch_shapes=[pltpu.VMEM((D_MODEL, D_EXPERT), F32), pltpu.VMEM((D_MODEL, D_EXPERT), F32),
                        pltpu.VMEM((D_EXPERT, D_MODEL), F32),
                        pltpu.VMEM((2, D_MODEL, D_EXPERT), BF16), pltpu.VMEM((2, D_MODEL, D_EXPERT), BF16),
                        pltpu.VMEM((2, D_EXPERT, D_MODEL), BF16),
                        pltpu.VMEM(shape, F32), pltpu.VMEM(shape, F32),
                        pltpu.SemaphoreType.DMA((EXPERT_ROW_BUFFERS,)),
                        pltpu.SemaphoreType.DMA((EXPERT_ROW_BUFFERS,)),
                        pltpu.SemaphoreType.DMA((3,))],
    )
    return pl.pallas_call(
        functools.partial(_expert_kernel, n=n_blocks),
        grid_spec=grid_spec,
        out_shape=jax.ShapeDtypeStruct(xs.shape, F32),
        name="routed_experts",
        compiler_params=_params("arbitrary"),
    )(block_expert, block_valid, *_weight_schedule(block_expert), xs, wg, wu, wd)


def _combine_kernel(dest_ref, dest_next_ref, x1_ref, route_ref, g_ref, yb_hbm, o_ref, ybuf, sem, *, n):
    i = pl.program_id(0)
    slot = i % 2

    def row_copy(idx8, r, k, s):
        dst = ybuf.at[s, pl.ds((k * ROW_TILE + r) * ROW_SUBLANES, ROW_SUBLANES)]
        return pltpu.make_async_copy(_row_tile(yb_hbm, idx8), dst, sem.at[s])

    @pl.when(i == 0)
    def _():
        def body(r, c):
            idx = [dest_ref[0, k * ROW_TILE + r] for k in range(TOP_K)]
            for k in range(TOP_K):
                row_copy(idx[k], r, k, 0).start()
            return c
        lax.fori_loop(0, ROW_TILE, body, 0)

    @pl.when(i + 1 < n)
    def _():
        for r0 in range(0, ROW_TILE, INDEX_BATCH):
            idx = [[dest_next_ref[0, k * ROW_TILE + r] for r in range(r0, r0 + INDEX_BATCH)] for k in range(TOP_K)]
            for r in range(r0, r0 + INDEX_BATCH):
                for k in range(TOP_K):
                    row_copy(idx[k][r - r0], r, k, 1 - slot).start(priority=k % DMA_THREADS)

    _tile_wait(yb_hbm, ybuf.at[slot], sem.at[slot])
    route = route_ref[...]
    buf = ybuf.at[slot]
    y = (_load_rows(buf, ROW_TILE) * route[:, 2:3]
         + _load_rows(buf, ROW_TILE, offset=ROW_TILE * ROW_SUBLANES) * route[:, 3:4])
    x2 = x1_ref[...] + y
    ms = jnp.mean(x2 * x2, axis=-1, keepdims=True)
    o_ref[...] = x2 * lax.rsqrt(ms + EPS) * g_ref[...]


def _combine_and_norm(dest8, x1, route, norm_w, yb):
    t = x1.shape[0]
    n_tiles = t // ROW_TILE
    dest_spec = lambda f: pl.BlockSpec((None, 1, TILE_ROW_COPIES), lambda i: (f(i), 0, 0), memory_space=pltpu.SMEM)
    return pl.pallas_call(
        functools.partial(_combine_kernel, n=n_tiles),
        grid=(n_tiles,),
        in_specs=[dest_spec(lambda i: i), dest_spec(lambda i: jnp.minimum(i + 1, n_tiles - 1)),
                  pl.BlockSpec((ROW_TILE, D_MODEL), lambda i: (i, 0)),
                  pl.BlockSpec((ROW_TILE, LANES), lambda i: (i, 0)),
                  _const_spec((1, D_MODEL)),
                  pl.BlockSpec(memory_space=pl.ANY)],
        out_specs=pl.BlockSpec((ROW_TILE, D_MODEL), lambda i: (i, 0)),
        out_shape=jax.ShapeDtypeStruct((t, D_MODEL), F32),
        scratch_shapes=[pltpu.VMEM((2, TILE_ROW_COPIES * ROW_SUBLANES, LANES), F32),
                        pltpu.SemaphoreType.DMA((2,))],
        name="combine_and_norm",
        compiler_params=_params("arbitrary", disable_bounds_checks=True),
    )(dest8, dest8, x1, route, norm_w, yb)


def _dispatch_plan(route_t, counts):
    t = route_t.shape[1]
    counts = counts[0, 0:N_EXPERTS].astype(jnp.int32)
    padded = (counts + MOE_BLOCK - 1) // MOE_BLOCK * MOE_BLOCK
    pad_end = jnp.cumsum(padded)
    pad_start = pad_end - padded
    expert = route_t[0:TOP_K].astype(jnp.int32)
    rank = route_t[4:4 + TOP_K].astype(jnp.int32)
    start_of = sum(jnp.where(expert == e, pad_start[e], 0) for e in range(N_EXPERTS))
    dest8 = (start_of + rank) * ROW_SUBLANES
    dest8 = dest8.reshape(TOP_K, t // ROW_TILE, ROW_TILE).transpose(1, 0, 2).reshape(t // ROW_TILE, 1, TILE_ROW_COPIES)
    n_blocks = -(-(t * TOP_K + N_EXPERTS * (MOE_BLOCK - 1)) // MOE_BLOCK)
    block_start = jnp.arange(n_blocks, dtype=jnp.int32) * MOE_BLOCK
    block_expert = jnp.minimum(jnp.sum((pad_end[None, :] <= block_start[:, None]).astype(jnp.int32), axis=1),
                               N_EXPERTS - 1)
    block_valid = (block_start < pad_end[-1]).astype(jnp.int32)
    n_slots = n_blocks * MOE_BLOCK
    fill_start = jnp.concatenate([pad_start + counts, pad_end[-1:]])
    fill_n = jnp.concatenate([padded - counts, n_slots - pad_end[-1:]])
    return dest8, fill_start, fill_n, block_expert, block_valid, n_slots


def _rel_bias_band(rel_table, qc):
    rows = qc * CHUNK
    window = ATTN_REACH + rows
    n = window + rows - 1
    diag = rel_table[:, np.clip(ATTN_REACH + rows - 1 - np.arange(n), -MAX_REL, MAX_REL) + MAX_REL].astype(F32)
    shifted = jnp.tile(diag, (1, rows + 1))[:, :rows * (n + 1)].reshape(N_HEADS_A, rows, n + 1)[:, :, :window]
    bias = shifted[:, ::-1, :] * LOG2E
    i = np.arange(rows)[:, None]
    j = np.arange(window)[None, :]
    first = (i // CHUNK) * CHUNK
    return jnp.where((j >= first) & (j < first + BAND), bias, NEG_INF)


def _pad_lanes(v, width=LANES):
    return jnp.pad(v.astype(F32), (0, width - v.shape[0])).reshape(1, width)


def _mixers(x, kv_cache, conv_state, ssm_state, w):
    batch, l, _ = x.shape
    nc = l // CHUNK
    t = batch * l
    x2d = x.reshape(t, D_MODEL)
    proj, dt_raw = _in_projection(x2d, w["norm_mix"], w["w_proj"])
    proj3 = proj.reshape(batch, l, PROJ_WIDTH)
    k_new = lambda rows: proj3[:, rows, K0:K0 + ATTN_WIDTH].astype(F32)
    v_new = lambda rows: proj3[:, rows, V0:V0 + ATTN_WIDTH].astype(F32)
    new_part = lambda col: (proj, pl.BlockSpec((l, ATTN_WIDTH), lambda b, c: (b, col // ATTN_WIDTH)), l)
    if kv_cache is None:
        qc = ATTN_CHUNKS_PER_STEP if nc % ATTN_CHUNKS_PER_STEP == 0 else 1
        attn, = _band_attention(proj, Q0 // ATTN_WIDTH, [new_part(K0)], [new_part(V0)],
                                _rel_bias_band(w["rel_table"], qc), batch=batch, nc=nc, pad_rows=ATTN_REACH, qc=qc)
        keep = slice(l - min(ATTN_REACH, l), l)
        k_keep, v_keep = k_new(keep), v_new(keep)
    else:
        assert nc == 1
        wlen = kv_cache[0].shape[1]
        cache_k, cache_v = (u.reshape(batch, wlen, ATTN_WIDTH) for u in kv_cache)
        cache_part = lambda u: (u, pl.BlockSpec((None, wlen, ATTN_WIDTH), lambda b, c: (b, 0, 0)), wlen)
        attn, k_keep, v_keep = _band_attention(
            proj, Q0 // ATTN_WIDTH, [cache_part(cache_k), new_part(K0)], [cache_part(cache_v), new_part(V0)],
            _rel_bias_band(w["rel_table"], 1), batch=batch, nc=nc, pad_rows=BAND - (wlen + l), qc=1, roll_out=True)
    tail = proj3[:, l - (CONV_WIDTH - 1):, XBC0:XBC0 + CONV_DIM].astype(F32)
    if conv_state is None:
        conv0 = st0 = None
        conv_new = tail
    else:
        conv0 = jnp.pad(conv_state, ((0, 0), (CARRY_ROWS - (CONV_WIDTH - 1), 0), (0, 0)))
        st0 = jnp.transpose(ssm_state.reshape(batch, D_INNER, D_STATE), (0, 2, 1))
        conv_new = jnp.concatenate([conv_state, tail], axis=1)[:, -(CONV_WIDTH - 1):]
    ssm, st_t = _ssd_mixer(proj, dt_raw, conv0, st0, w["conv_w"], w["conv_b"], w["dt_bias"], w["a_log"], w["d_exp"],
                           w["ssm_norm"], batch=batch, nc=nc)
    ssm_new = jnp.transpose(st_t, (0, 2, 1)).reshape(batch, SSM_HEADS, SSM_HEAD_DIM, D_STATE)
    heads = lambda u: u.reshape(batch, u.shape[1], N_HEADS_A, HEAD_DIM_A)
    return (x2d, attn, ssm, proj), (heads(k_keep), heads(v_keep), conv_new, ssm_new)


def kernel(x_prompt, x_sample, cache_attn_k, cache_attn_v, state_conv, state_ssm, norm_mix, w_in, rel_bias, conv_w, conv_b, dt_bias, a_log, d_skip, ssm_norm, w_branch_a, w_branch_b, w_out, norm_ffn, w_router_group, b_router_group, w_router_expert, b_router_expert, w_gate, w_up, w_down, norm_final):
    assert norm_mix.shape[0] == 1, "single-layer trunk"
    q, k, v, z, xbc, dtc, ga, gb = jnp.split(w_in[0], np.cumsum(
        (ATTN_WIDTH, ATTN_WIDTH, ATTN_WIDTH, D_INNER, CONV_DIM, SSM_HEADS, D_MODEL, D_MODEL))[:-1].tolist(), axis=1)
    w_proj = jnp.concatenate([z, ga, gb, xbc, q, k, v, jnp.pad(dtc, ((0, 0), (0, LANES - SSM_HEADS)))],
                             axis=1).astype(BF16)
    w_router = jnp.pad(jnp.concatenate([w_router_group[0], w_router_expert[0]], axis=1),
                       ((0, 0), (0, LANES - N_EXPERT_GROUPS - N_EXPERTS)))
    w_router_hi = w_router.astype(BF16)
    w_router_lo = (w_router - w_router_hi.astype(F32)).astype(BF16)
    b_router = _pad_lanes(jnp.concatenate([b_router_group[0], b_router_expert[0]]))
    w = dict(
        norm_mix=norm_mix[0].reshape(1, D_MODEL), w_proj=w_proj, rel_table=rel_bias[0],
        conv_w=conv_w[0], conv_b=conv_b[0].reshape(1, CONV_DIM), dt_bias=_pad_lanes(dt_bias[0]),
        a_log=_pad_lanes(a_log[0]), d_exp=jnp.repeat(d_skip[0], SSM_HEAD_DIM).reshape(1, D_INNER),
        ssm_norm=ssm_norm[0].reshape(1, D_INNER),
    )
    mix_w = (w_branch_a[0].astype(BF16), w_branch_b[0].astype(BF16), w_out[0].astype(BF16),
             norm_ffn[0].reshape(1, D_MODEL), jnp.concatenate([w_router_hi, w_router_lo], axis=1), b_router)
    norm_out = norm_final.reshape(1, D_MODEL)

    groups = [_mixers(x_prompt, None, None, None, w),
              _mixers(x_sample, (cache_attn_k[0], cache_attn_v[0]), state_conv[0], state_ssm[0], w)]
    counts = jnp.zeros((1, LANES), F32)
    mixed = []
    routes_t = []
    for acts, _ in groups:
        x1, hn, route, route_t, counts = _mix_and_route(*acts, *mix_w, counts)
        mixed.append((x1, hn, route))
        routes_t.append(route_t)
    dest8, fill_start, fill_n, block_expert, block_valid, n_slots = _dispatch_plan(
        jnp.concatenate(routes_t, axis=1), counts)
    xs = _dispatch(dest8, fill_start, fill_n, [m[1] for m in mixed], n_slots)
    yb = _routed_experts(xs, block_expert, block_valid, w_gate[0], w_up[0], w_down[0])
    outs, first = [], 0
    for (x1, _, route), (acts, _) in zip(mixed, groups):
        n_tiles = x1.shape[0] // ROW_TILE
        y = _combine_and_norm(dest8[first:first + n_tiles], x1, route, norm_out, yb)
        first += n_tiles
        outs.append(y)
    (yp, ys), ((kp, vp, cp, sp), (ks, vs, cs, ss)) = outs, [g[1] for g in groups]
    yp = yp.reshape(x_prompt.shape)
    ys = ys.reshape(x_sample.shape)
    return (yp, ys, kp[None], vp[None], cp[None], sp[None], ks[None], vs[None], cs[None], ss[None])
```

```python
import functools
import math

import numpy as np
import jax
import jax.numpy as jnp
from jax import lax
from jax.experimental import pallas as pl
from jax.experimental.pallas import tpu as pltpu

F32 = jnp.float32
BF16 = jnp.bfloat16

D_MODEL = 1024
CHUNK = 64
LEFT_CHUNKS = 8
ATTN_REACH = LEFT_CHUNKS * CHUNK
BAND = ATTN_REACH + CHUNK
N_HEADS_A = 8
HEAD_DIM_A = 64
ATTN_WIDTH = N_HEADS_A * HEAD_DIM_A
MAX_REL = 256
ATTN_SCALE = 1.0 / math.sqrt(HEAD_DIM_A)
NEG_INF = -1e30
SSM_HEADS = 16
SSM_HEAD_DIM = 64
D_INNER = SSM_HEADS * SSM_HEAD_DIM
SSM_GROUPS = 2
GROUP_WIDTH = D_INNER // SSM_GROUPS
D_STATE = 128
CONV_WIDTH = 4
CONV_DIM = D_INNER + 2 * SSM_GROUPS * D_STATE
N_EXPERT_GROUPS = 4
EXPERTS_PER_GROUP = 8
N_EXPERTS = N_EXPERT_GROUPS * EXPERTS_PER_GROUP
TOP_K = 2
D_EXPERT = 512
MOE_BLOCK = 256
EPS = 1e-6

LANES = 128
SUBLANES = 8
Z0, GA0, GB0, XBC0, Q0, K0, V0 = 0, 1024, 2048, 3072, 4608, 5120, 5632
PROJ_WIDTH = V0 + ATTN_WIDTH
PROJ_TILE = 1024
ROW_TILE = 256
ATTN_CHUNKS_PER_STEP = 2
SSD_STREAMS = 4
VMEM_LIMIT = 56 * 1024 * 1024

_NT = (((1,), (1,)), ((), ()))


def _params(*sem, **kw):
    return pltpu.CompilerParams(dimension_semantics=sem, vmem_limit_bytes=VMEM_LIMIT, **kw)


def _const_spec(shape):
    nd = len(shape)
    return pl.BlockSpec(shape, lambda *_: (0,) * nd)


def _inproj_kernel(x_ref, g_ref, w_ref, o_ref, dt_ref):
    x = x_ref[...]
    ms = jnp.mean(x * x, axis=-1, keepdims=True)
    h = (x * lax.rsqrt(ms + EPS) * g_ref[...]).astype(BF16)
    for j in range(0, PROJ_WIDTH, PROJ_TILE):
        o_ref[:, j:j + PROJ_TILE] = jnp.dot(h, w_ref[:, j:j + PROJ_TILE], preferred_element_type=F32).astype(BF16)
    dt_ref[...] = jnp.dot(h, w_ref[:, PROJ_WIDTH:PROJ_WIDTH + LANES], preferred_element_type=F32)


def _in_projection(x2d, norm_w, w_proj):
    t = x2d.shape[0]
    return pl.pallas_call(
        _inproj_kernel,
        grid=(t // ROW_TILE,),
        in_specs=[pl.BlockSpec((ROW_TILE, D_MODEL), lambda i: (i, 0)),
                  _const_spec((1, D_MODEL)),
                  _const_spec((D_MODEL, PROJ_WIDTH + LANES))],
        out_specs=[pl.BlockSpec((ROW_TILE, PROJ_WIDTH), lambda i: (i, 0)),
                   pl.BlockSpec((ROW_TILE, LANES), lambda i: (i, 0))],
        out_shape=[jax.ShapeDtypeStruct((t, PROJ_WIDTH), BF16), jax.ShapeDtypeStruct((t, LANES), F32)],
        name="in_projection",
        compiler_params=_params("parallel"),
    )(x2d, norm_w, w_proj)


LOG2E = math.log2(math.e)
ATTN_STAGES = 3


def _attn_kernel(q_ref, *refs, pad_rows, part_rows, part_by_head, qc, roll_out):
    n_parts = len(part_rows)
    k_parts, v_parts = refs[:n_parts], refs[n_parts:2 * n_parts]
    bias_ref, o_ref = refs[2 * n_parts:2 * n_parts + 2]
    rolled = refs[2 * n_parts + 2:-2]
    kpad, vpad = refs[-2:]
    rows = qc * CHUNK
    window = ATTN_REACH + rows
    c = pl.program_id(1)

    @pl.when(c == 0)
    def _():
        if pad_rows:
            kpad[0:pad_rows, :] = jnp.zeros((pad_rows, ATTN_WIDTH), BF16)
            vpad[0:pad_rows, :] = jnp.zeros((pad_rows, ATTN_WIDTH), BF16)
        def rows_of(ref, n, by_head):
            if not by_head:
                return ref[...].astype(BF16)
            heads = [ref[pl.ds(h, n, stride=N_HEADS_A), :] for h in range(N_HEADS_A)]
            return jnp.concatenate(heads, axis=1).astype(BF16)

        first = pad_rows
        for k_ref, v_ref, n, by_head in zip(k_parts, v_parts, part_rows, part_by_head):
            kpad[first:first + n, :] = rows_of(k_ref, n, by_head)
            vpad[first:first + n, :] = rows_of(v_ref, n, by_head)
            first += n
        if roll_out:
            kept, new = part_rows[0] - part_rows[1], part_rows[1]
            for out_ref, parts in zip(rolled, (k_parts, v_parts)):
                out_ref[0:kept * N_HEADS_A, :] = parts[0][new * N_HEADS_A:(new + kept) * N_HEADS_A, :]
                fresh = parts[1][...].astype(F32)
                for h in range(N_HEADS_A):
                    out_ref[pl.ds(kept * N_HEADS_A + h, new, stride=N_HEADS_A), :] = (
                        fresh[:, h * HEAD_DIM_A:(h + 1) * HEAD_DIM_A])

    start = pl.multiple_of(c * rows, rows)

    def body(mask_start):
        kw = kpad[pl.ds(start, window), :]
        vw = vpad[pl.ds(start, window), :]
        q = (q_ref[...].astype(F32) * (ATTN_SCALE * LOG2E)).astype(BF16)
        low = lax.broadcasted_iota(jnp.int32, (rows, LANES), 1) < HEAD_DIM_A
        if mask_start:
            valid = lax.broadcasted_iota(jnp.int32, (rows, window), 1) + start >= pad_rows
        outs = {}

        def head(h):
            hp, sub = divmod(h, 2)
            sl = slice(LANES * hp, LANES * (hp + 1))
            q2 = q[:, sl]
            qm = jnp.where(low if sub == 0 else jnp.logical_not(low), q2, jnp.zeros_like(q2))
            s = lax.dot_general(qm, kw[:, sl], _NT, preferred_element_type=F32) + bias_ref[h]
            if mask_start:
                s = jnp.where(valid, s, NEG_INF)
            yield
            m = jnp.max(s, axis=-1, keepdims=True)
            p = jnp.exp2(s - m)
            l = jnp.sum(p, axis=-1, keepdims=True)
            pb = p.astype(BF16)
            yield
            outs[h] = jnp.dot(pb, vw[:, sl], preferred_element_type=F32) / l
            if sub == 1:
                o_ref[:, sl] = jnp.where(low, outs[h - 1], outs[h]).astype(o_ref.dtype)

        heads = [head(h) for h in range(N_HEADS_A)]
        for t in range(N_HEADS_A + ATTN_STAGES - 1):
            for h in range(min(t, N_HEADS_A - 1), max(t - ATTN_STAGES, -1), -1):
                next(heads[h], None)

    if pad_rows:
        pl.when(start < pad_rows)(lambda: body(True))
        pl.when(start >= pad_rows)(lambda: body(False))
    else:
        body(False)


def _band_attention(q_arr, q_col, k_parts, v_parts, bias, *, batch, nc, pad_rows, qc, roll_out=False):
    steps = nc // qc
    rows = qc * CHUNK
    part_rows = tuple(p[2] for p in k_parts)
    lk = sum(part_rows)
    kern = functools.partial(_attn_kernel, pad_rows=pad_rows, part_rows=part_rows,
                             part_by_head=tuple(p[3] for p in k_parts), qc=qc, roll_out=roll_out)
    out_specs = [pl.BlockSpec((rows, ATTN_WIDTH), lambda b, c: (b * steps + c, 0))]
    out_shape = [jax.ShapeDtypeStruct((batch * nc * CHUNK, ATTN_WIDTH), BF16)]
    if roll_out:
        assert len(part_rows) == 2 and nc == 1 and k_parts[0][3] and not k_parts[1][3]
        cache_rows = part_rows[0] * N_HEADS_A
        out_specs += [pl.BlockSpec((None, cache_rows, HEAD_DIM_A), lambda b, c: (b, 0, 0))] * 2
        out_shape += [jax.ShapeDtypeStruct((batch, cache_rows, HEAD_DIM_A), F32)] * 2
    return pl.pallas_call(
        kern,
        grid=(batch, steps),
        in_specs=[pl.BlockSpec((rows, ATTN_WIDTH), lambda b, c: (b * steps + c, q_col))]
        + [p[1] for p in k_parts] + [p[1] for p in v_parts]
        + [_const_spec((N_HEADS_A, rows, ATTN_REACH + rows))],
        out_specs=out_specs,
        out_shape=out_shape,
        scratch_shapes=[pltpu.VMEM((pad_rows + lk, ATTN_WIDTH), BF16),
                        pltpu.VMEM((pad_rows + lk, ATTN_WIDTH), BF16)],
        name="band_attention",
        compiler_params=_params("parallel", "arbitrary"),
    )(q_arr, *[p[0] for p in k_parts], *[p[0] for p in v_parts], bias)


CARRY_ROWS = SUBLANES
SHIFT_TAPS = CONV_WIDTH - 1
SHIFT_K = 2 * CARRY_ROWS + CHUNK


def _split_bf16(x, parts):
    out = []
    for _ in range(parts - 1):
        h = x.astype(BF16)
        out.append(h)
        x = x - h.astype(F32)
    out.append(x.astype(BF16))
    return out


def _ssd_kernel(xbc_ref, z_ref, dt_ref, conv0_ref, st0_ref, cw_ref, cb_ref, dtb_ref, alog_ref,
                dexp_ref, nw_ref, tri_ref, exp_ref, sel_ref, eye_ref, shift_ref,
                y_ref, stout_ref, carry, st, *, has_init, streams, nc):
    c = pl.program_id(1)

    @pl.when(c == 0)
    def _():
        if has_init:
            carry[...] = conv0_ref[...]
            st[...] = st0_ref[...]
        else:
            carry[...] = jnp.zeros((streams, CARRY_ROWS, CONV_DIM), F32)
            st[...] = jnp.zeros((streams, D_STATE, D_INNER), F32)

    chunks = [_ssd_chunk(xbc_ref.at[s], z_ref.at[s], dt_ref.at[s], cw_ref, cb_ref, dtb_ref, alog_ref, dexp_ref,
                         nw_ref, tri_ref, exp_ref, sel_ref, eye_ref, shift_ref, y_ref.at[s], carry.at[s], st.at[s])
              for s in range(streams)]
    while chunks:
        chunks = [g for g in chunks if next(g, True) is None]

    @pl.when(c == nc - 1)
    def _():
        stout_ref[...] = st[...]


def _ssd_chunk(xbc_ref, z_ref, dt_ref, cw_ref, cb_ref, dtb_ref, alog_ref, dexp_ref, nw_ref,
               tri_ref, exp_ref, sel_ref, eye_ref, shift_ref, y_ref, carry, st):
    prev = carry[...]
    prev_hi = prev.astype(BF16).astype(F32)
    prev_parts = jnp.concatenate([prev_hi, prev - prev_hi], axis=0).astype(BF16)
    taps = jnp.dot(shift_ref[...], jnp.concatenate([prev_parts, xbc_ref[...]], axis=0), preferred_element_type=F32)

    dt_in = dt_ref[...] + dtb_ref[...]
    dt = jnp.maximum(dt_in, 0.0) + jnp.log1p(jnp.exp(-jnp.abs(dt_in)))
    da = dt * (-jnp.exp(alog_ref[...]))
    tri = tri_ref[...]
    cs3 = jnp.dot(tri, jnp.concatenate(_split_bf16(da, 3), axis=1), preferred_element_type=F32)
    yield

    cs = cs3[:, 0:LANES] + cs3[:, LANES:2 * LANES] + cs3[:, 2 * LANES:3 * LANES]
    cs_last = cs[CHUNK - 1:CHUNK, :]
    ecs = jnp.exp(cs)
    to_end = jnp.exp(cs_last - cs)

    stack = jnp.concatenate(_split_bf16(dt, 1) + _split_bf16(to_end, 1) + _split_bf16(ecs, 2)
                            + _split_bf16(cs, 3), axis=0)
    wide = jnp.dot(stack, exp_ref[...], preferred_element_type=F32)

    lane = lax.broadcasted_iota(jnp.int32, (CHUNK, LANES), 1)
    even = (lane % 2) == 0
    csm = jnp.concatenate([jnp.where(even, cs, 0.0), jnp.where(even, 0.0, cs)], axis=0)
    srow3 = lax.dot_general(sel_ref[...], jnp.concatenate(_split_bf16(csm, 3), axis=0), _NT,
                            preferred_element_type=F32)
    yield

    rows = [wide[i * CHUNK:(i + 1) * CHUNK, :] for i in range(7)]
    dt_w = rows[0]
    to_end_w = rows[1]
    ecs_w = rows[2] + rows[3]
    cs_w = rows[4] + rows[5] + rows[6]
    srow = srow3[:, 0:LANES] + srow3[:, LANES:2 * LANES] + srow3[:, 2 * LANES:3 * LANES]

    xdt_b, xw_b, skip, bc = [], [], [], []
    for j in range(CONV_DIM // LANES):
        sl = slice(j * LANES, (j + 1) * LANES)
        conv = cb_ref[:, sl] + xbc_ref[:, sl].astype(F32) * cw_ref[CONV_WIDTH - 1:CONV_WIDTH, sl]
        for tap in range(SHIFT_TAPS):
            conv = conv + taps[tap * CHUNK:(tap + 1) * CHUNK, sl] * cw_ref[tap:tap + 1, sl]
        u = conv * jax.nn.sigmoid(conv)
        if j < D_INNER // LANES:
            xdt = u * dt_w[:, sl]
            xdt_b.append(xdt.astype(BF16))
            xw_b.append((xdt * to_end_w[:, sl]).astype(BF16))
            skip.append(u * dexp_ref[:, sl])
        else:
            bc.append(u.astype(BF16))
    low = lane < SSM_HEAD_DIM
    t_idx = lax.broadcasted_iota(jnp.int32, (CHUNK, LANES), 0)
    causal = t_idx >= (lane % SSM_HEAD_DIM)
    eye = eye_ref[...]
    pairs = SSM_HEADS // SSM_GROUPS // 2
    yield

    for g in range(SSM_GROUPS):
        b_g, c_g = bc[g], bc[SSM_GROUPS + g]
        gsl = slice(g * GROUP_WIDTH, (g + 1) * GROUP_WIDTH)
        st_g = st[:, gsl]
        y_off = jnp.dot(c_g, st_g.astype(BF16), preferred_element_type=F32)
        b_t = lax.dot_general(eye, b_g, _NT, preferred_element_type=F32).astype(BF16)
        xw_g = jnp.concatenate(xw_b[g * pairs:(g + 1) * pairs], axis=1)
        st[:, gsl] = st_g * ecs_w[CHUNK - 1:CHUNK, gsl] + jnp.dot(b_t, xw_g, preferred_element_type=F32)
        cb2 = lax.dot_general(c_g, jnp.concatenate([b_g, b_g], axis=0), _NT, preferred_element_type=F32)
        yield
        gated, sumsq = [], 0.0
        for kk in range(pairs):
            k = g * pairs + kk
            psl = slice(k * LANES, (k + 1) * LANES)
            decay = jnp.where(causal, jnp.exp(cs_w[:, psl] - srow[k:k + 1, :]), 0.0)
            gmat = (cb2 * decay).astype(BF16)
            xp = xdt_b[k]
            xblk = jnp.concatenate([jnp.where(low, xp, jnp.zeros_like(xp)),
                                    jnp.where(low, jnp.zeros_like(xp), xp)], axis=0)
            y = (jnp.dot(gmat, xblk, preferred_element_type=F32)
                 + y_off[:, kk * LANES:(kk + 1) * LANES] * ecs_w[:, psl] + skip[k])
            zz = z_ref[:, psl].astype(F32)
            y = y * (zz * jax.nn.sigmoid(zz))
            gated.append(y)
            sumsq = sumsq + jnp.sum(y * y, axis=-1, keepdims=True)
        scale = lax.rsqrt(sumsq * (1.0 / GROUP_WIDTH) + EPS)
        for kk in range(pairs):
            psl = slice((g * pairs + kk) * LANES, (g * pairs + kk + 1) * LANES)
            y_ref[:, psl] = (gated[kk] * scale * nw_ref[:, psl]).astype(y_ref.dtype)
        yield
    carry[...] = xbc_ref[CHUNK - 2 * CARRY_ROWS:CHUNK, :].astype(F32)[CARRY_ROWS:, :]


def _ssd_constants():
    tri = np.tril(np.ones((CHUNK, CHUNK), np.float32))
    expand = np.zeros((LANES, D_INNER), np.float32)
    for e in range(SSM_HEADS):
        expand[e, e * SSM_HEAD_DIM:(e + 1) * SSM_HEAD_DIM] = 1.0
    sel = np.zeros((16, LANES), np.float32)
    for e in range(SSM_HEADS):
        sel[e // 2, e] = 1.0
    eye = np.eye(D_STATE, dtype=np.float32)
    shift = np.zeros((SHIFT_TAPS * CHUNK, SHIFT_K), np.float32)
    for j in range(SHIFT_TAPS):
        for t in range(CHUNK):
            m = t - (CONV_WIDTH - 1) + j
            if m >= 0:
                shift[j * CHUNK + t, 2 * CARRY_ROWS + m] = 1.0
            else:
                shift[j * CHUNK + t, CARRY_ROWS + m] = 1.0
                shift[j * CHUNK + t, 2 * CARRY_ROWS + m] = 1.0
    return tuple(jnp.asarray(a, BF16) for a in (tri, expand, sel, eye, shift))


def _ssd_mixer(proj, dt_raw, conv0, st0, conv_w, conv_b, dt_bias_p, a_log_p, d_exp, norm_w, *, batch, nc):
    streams = SSD_STREAMS if batch % SSD_STREAMS == 0 else 1
    groups = batch // streams
    l = nc * CHUNK
    has_init = conv0 is not None
    if not has_init:
        conv0 = jnp.zeros((1, streams, CARRY_ROWS, CONV_DIM), F32)
        st0 = jnp.zeros((1, streams, D_STATE, D_INNER), F32)
        init_map = lambda b, c: (0, 0, 0, 0)
    else:
        conv0 = conv0.reshape(groups, streams, CARRY_ROWS, CONV_DIM)
        st0 = st0.reshape(groups, streams, D_STATE, D_INNER)
        init_map = lambda b, c: (b, 0, 0, 0)
    tri, expand, sel, eye, shift = _ssd_constants()
    proj4 = proj.reshape(groups, streams, l, PROJ_WIDTH)
    dt4 = dt_raw.reshape(groups, streams, l, LANES)
    chunk_of = lambda w, col: pl.BlockSpec((None, streams, CHUNK, w), lambda b, c: (b, 0, c, col))
    state_spec = lambda m: pl.BlockSpec((None, streams, D_STATE, D_INNER), m)
    kern = functools.partial(_ssd_kernel, has_init=has_init, streams=streams, nc=nc)
    y, st_out = pl.pallas_call(
        kern,
        grid=(groups, nc),
        in_specs=[chunk_of(CONV_DIM, XBC0 // CONV_DIM), chunk_of(D_INNER, Z0 // D_INNER), chunk_of(LANES, 0),
                  pl.BlockSpec((None, streams, CARRY_ROWS, CONV_DIM), init_map), state_spec(init_map),
                  _const_spec((CONV_WIDTH, CONV_DIM)), _const_spec((1, CONV_DIM)),
                  _const_spec((1, LANES)), _const_spec((1, LANES)),
                  _const_spec((1, D_INNER)), _const_spec((1, D_INNER)),
                  _const_spec((CHUNK, CHUNK)), _const_spec((LANES, D_INNER)),
                  _const_spec((16, LANES)), _const_spec((D_STATE, D_STATE)),
                  _const_spec((SHIFT_TAPS * CHUNK, SHIFT_K))],
        out_specs=[chunk_of(D_INNER, 0), state_spec(lambda b, c: (b, 0, 0, 0))],
        out_shape=[jax.ShapeDtypeStruct((groups, streams, l, D_INNER), BF16),
                   jax.ShapeDtypeStruct((groups, streams, D_STATE, D_INNER), F32)],
        scratch_shapes=[pltpu.VMEM((streams, CARRY_ROWS, CONV_DIM), F32),
                        pltpu.VMEM((streams, D_STATE, D_INNER), F32)],
        name="ssd_mixer",
        compiler_params=_params("parallel", "arbitrary"),
    )(proj4, proj4, dt4, conv0, st0, conv_w, conv_b, dt_bias_p, a_log_p, d_exp, norm_w, tri, expand, sel, eye, shift)
    return y.reshape(batch * l, D_INNER), st_out.reshape(batch, D_STATE, D_INNER)


ROW_SUBLANES = D_MODEL // LANES
assert ROW_SUBLANES == SUBLANES


def _store_rows(ref, x, rows, offset=0):
    for s in range(ROW_SUBLANES):
        ref[pl.ds(offset + s, rows, stride=ROW_SUBLANES), :] = x[:, s * LANES:(s + 1) * LANES]


def _load_rows(ref, rows, offset=0):
    return jnp.concatenate([ref[pl.ds(offset + s, rows, stride=ROW_SUBLANES), :] for s in range(ROW_SUBLANES)],
                           axis=1)


def _row_tile(ref, idx8):
    return ref.at[pl.ds(pl.multiple_of(idx8, ROW_SUBLANES), ROW_SUBLANES)]


def _mix_kernel(x_ref, attn_ref, ssm_ref, ga_ref, gb_ref, wa_ref, wb_ref, wo_ref, g_ref, wr_ref, br_ref, slt_ref,
                count0_ref, x1_ref, hn_ref, route_ref, route_t_ref, counts_ref, carry, *, subtiles):
    @pl.when(pl.program_id(0) == 0)
    def _():
        carry[...] = count0_ref[...]

    running = {"counts": carry[...]}
    tiles = [_mix_tile(j * ROW_TILE, running, x_ref, attn_ref, ssm_ref, ga_ref, gb_ref, wa_ref, wb_ref, wo_ref, g_ref,
                       wr_ref, br_ref, slt_ref, x1_ref, hn_ref, route_ref, route_t_ref) for j in range(subtiles)]
    while tiles:
        tiles = [g for g in tiles if next(g, True) is None]
    carry[...] = running["counts"]
    counts_ref[...] = running["counts"]


def _mix_tile(r0, running, x_ref, attn_ref, ssm_ref, ga_ref, gb_ref, wa_ref, wb_ref, wo_ref, g_ref, wr_ref, br_ref,
              slt_ref, x1_ref, hn_ref, route_ref, route_t_ref):
    rows = slice(r0, r0 + ROW_TILE)
    a = jnp.dot(attn_ref[rows, :], wa_ref[...], preferred_element_type=F32)
    s = jnp.dot(ssm_ref[rows, :], wb_ref[...], preferred_element_type=F32)
    yield
    mixed = jax.nn.sigmoid(ga_ref[rows, :].astype(F32)) * a + jax.nn.sigmoid(gb_ref[rows, :].astype(F32)) * s
    x1 = x_ref[rows, :] + jnp.dot(mixed.astype(BF16), wo_ref[...], preferred_element_type=F32)
    yield
    x1_ref[rows, :] = x1
    ms = jnp.mean(x1 * x1, axis=-1, keepdims=True)
    hn = x1 * lax.rsqrt(ms + EPS) * g_ref[...]
    _store_rows(hn_ref, hn, ROW_TILE, offset=r0 * ROW_SUBLANES)

    hi = hn.astype(BF16)
    lo = (hn - hi.astype(F32)).astype(BF16)
    both = jnp.dot(hi, wr_ref[...], preferred_element_type=F32)
    lo_part = jnp.dot(lo, wr_ref[:, 0:LANES], preferred_element_type=F32)
    yield
    logits = both[:, 0:LANES] + both[:, LANES:2 * LANES] + lo_part + br_ref[...]
    lane = lax.broadcasted_iota(jnp.int32, logits.shape, 1)
    lane_f = lane.astype(F32)
    big = float(LANES)
    is_g = lane < N_EXPERT_GROUPS
    gl = jnp.where(is_g, logits, -jnp.inf)
    gmax = jnp.max(gl, axis=-1, keepdims=True)
    gidx = jnp.min(jnp.where(gl == gmax, lane_f, big), axis=-1, keepdims=True)
    gsum = jnp.sum(jnp.where(is_g, jnp.exp(gl - gmax), 0.0), axis=-1, keepdims=True)
    g_top = 1.0 / gsum
    first = N_EXPERT_GROUPS + gidx * EXPERTS_PER_GROUP
    in_group = (lane_f >= first) & (lane_f < first + EXPERTS_PER_GROUP)
    el = jnp.where(in_group, logits, -jnp.inf)
    m1 = jnp.max(el, axis=-1, keepdims=True)
    i1 = jnp.min(jnp.where(el == m1, lane_f, big), axis=-1, keepdims=True)
    el2 = jnp.where(lane_f == i1, -jnp.inf, el)
    m2 = jnp.max(el2, axis=-1, keepdims=True)
    i2 = jnp.min(jnp.where(el2 == m2, lane_f, big), axis=-1, keepdims=True)
    r = jnp.exp(m2 - m1)
    w1 = g_top / (1.0 + r)
    w2 = g_top * r / (1.0 + r)
    e1 = i1 - N_EXPERT_GROUPS
    e2 = i2 - N_EXPERT_GROUPS

    oh1 = lane_f == e1
    oh2 = lane_f == e2
    hot1 = jnp.where(oh1, 1.0, 0.0)
    hot2 = jnp.where(oh2, 1.0, 0.0)
    onehots = jnp.concatenate([hot1, hot2], axis=1).astype(BF16)
    before = jnp.dot(slt_ref[...], onehots, preferred_element_type=F32)
    cnt1 = jnp.sum(hot1, axis=0, keepdims=True)
    cnt2 = jnp.sum(hot2, axis=0, keepdims=True)
    yield
    base = running["counts"]
    rank1 = jnp.sum(jnp.where(oh1, before[:, 0:LANES] + base, 0.0), axis=-1, keepdims=True)
    rank2 = jnp.sum(jnp.where(oh2, before[:, LANES:2 * LANES] + (base + cnt1), 0.0), axis=-1, keepdims=True)
    running["counts"] = base + cnt1 + cnt2

    route = jnp.zeros_like(logits)
    for col, val in enumerate((e1, e2, w1, w2, rank1, rank2)):
        route = jnp.where(lane == col, val, route)
    route_ref[rows, :] = route
    route_t_ref[:, rows] = jnp.transpose(route)[0:SUBLANES, :]


MIX_SUBTILES = 2


def _mix_and_route(x2d, attn, ssm, proj, wa, wb, wo, norm_w, w_router, b_router, count0):
    t = x2d.shape[0]
    subtiles = MIX_SUBTILES if t % (MIX_SUBTILES * ROW_TILE) == 0 else 1
    step_rows = subtiles * ROW_TILE
    tile = lambda w, col: pl.BlockSpec((step_rows, w), lambda i: (i, col))
    slt = jnp.asarray(np.tril(np.ones((ROW_TILE, ROW_TILE), np.float32), k=-1), BF16)
    return pl.pallas_call(
        functools.partial(_mix_kernel, subtiles=subtiles),
        grid=(t // step_rows,),
        in_specs=[tile(D_MODEL, 0), tile(ATTN_WIDTH, 0), tile(D_INNER, 0),
                  tile(D_MODEL, GA0 // D_MODEL), tile(D_MODEL, GB0 // D_MODEL),
                  _const_spec((ATTN_WIDTH, D_MODEL)), _const_spec((D_INNER, D_MODEL)),
                  _const_spec((D_MODEL, D_MODEL)), _const_spec((1, D_MODEL)),
                  _const_spec((D_MODEL, 2 * LANES)), _const_spec((1, LANES)),
                  _const_spec((ROW_TILE, ROW_TILE)), _const_spec((1, LANES))],
        out_specs=[tile(D_MODEL, 0), pl.BlockSpec((step_rows * ROW_SUBLANES, LANES), lambda i: (i, 0)),
                   tile(LANES, 0), pl.BlockSpec((SUBLANES, step_rows), lambda i: (0, i)), _const_spec((1, LANES))],
        out_shape=[jax.ShapeDtypeStruct((t, D_MODEL), F32),
                   jax.ShapeDtypeStruct((t * ROW_SUBLANES, LANES), F32),
                   jax.ShapeDtypeStruct((t, LANES), F32),
                   jax.ShapeDtypeStruct((SUBLANES, t), F32),
                   jax.ShapeDtypeStruct((1, LANES), F32)],
        scratch_shapes=[pltpu.VMEM((1, LANES), F32)],
        name="mix_and_route",
        compiler_params=_params("arbitrary"),
    )(x2d, attn, ssm, proj, proj, wa, wb, wo, norm_w, w_router, b_router, slt, count0)


TILE_ROW_COPIES = ROW_TILE * TOP_K
INDEX_BATCH = 8
DISPATCH_BUFFERS = 3
DMA_THREADS = 2


def _tile_wait(ref_a, ref_b, sem):
    n = TILE_ROW_COPIES * ROW_SUBLANES
    pltpu.make_async_copy(ref_a.at[pl.ds(0, n)], ref_b.at[pl.ds(0, n)], sem).wait()


def _dispatch_kernel(fill_start_ref, fill_n_ref, dest_ref, *refs, tiles_per_source, fill_rows):
    n_src = len(tiles_per_source)
    sources = refs[:n_src]
    xs_hbm, hbuf, zbuf, in_sem, out_sem, fill_sem = refs[n_src:]
    i = pl.program_id(0)
    n = pl.num_programs(0)
    tile_rows = ROW_TILE * ROW_SUBLANES

    def tile_copy(src, j, b):
        return pltpu.make_async_copy(src.at[pl.ds(pl.multiple_of(j * tile_rows, tile_rows), tile_rows)],
                                     hbuf.at[b], in_sem.at[b])

    def tile_in_start(j):
        b = j % DISPATCH_BUFFERS
        first = 0
        for src, count in zip(sources, tiles_per_source):
            pl.when((j >= first) & (j < first + count))(lambda src=src, first=first: tile_copy(src, j - first, b).start())
            first += count

    @pl.when(i == 0)
    def _():
        zbuf[...] = jnp.zeros((ROW_SUBLANES, LANES), F32)
        tile_in_start(i)
        pl.when(n > 1)(lambda: tile_in_start(i + 1))

    buf = i % DISPATCH_BUFFERS
    tile_copy(sources[0], 0, buf).wait()
    for r0 in range(0, ROW_TILE, INDEX_BATCH):
        dests = [[dest_ref[0, k * ROW_TILE + r] for r in range(r0, r0 + INDEX_BATCH)] for k in range(TOP_K)]
        for r in range(r0, r0 + INDEX_BATCH):
            src = hbuf.at[buf, pl.ds(r * ROW_SUBLANES, ROW_SUBLANES)]
            for k in range(TOP_K):
                dst = _row_tile(xs_hbm, dests[k][r - r0])
                pltpu.make_async_copy(src, dst, out_sem.at[buf]).start(priority=k % DMA_THREADS)

    @pl.when(i > 0)
    def _():
        _tile_wait(xs_hbm, xs_hbm, out_sem.at[(i - 1) % DISPATCH_BUFFERS])

    pl.when(i + 2 < n)(lambda: tile_in_start(i + 2))

    @pl.when(i == n - 1)
    def _():
        _tile_wait(xs_hbm, xs_hbm, out_sem.at[buf])

        def per_range(e, carry):
            first = fill_start_ref[e]

            def start(j, c):
                pltpu.make_async_copy(zbuf, _row_tile(xs_hbm, (first + j) * ROW_SUBLANES), fill_sem.at[0]).start()
                return c

            lax.fori_loop(0, fill_n_ref[e], start, 0)
            return carry

        lax.fori_loop(0, N_EXPERTS + 1, per_range, 0)
        filled = pl.ds(0, fill_rows * ROW_SUBLANES)
        pltpu.make_async_copy(xs_hbm.at[filled], xs_hbm.at[filled], fill_sem.at[0]).wait()


def _dispatch(dest8, fill_start, fill_n, hn_list, n_slots):
    n_tiles = dest8.shape[0]
    tiles_per_source = tuple(h.shape[0] // (ROW_TILE * ROW_SUBLANES) for h in hn_list)
    assert sum(tiles_per_source) == n_tiles
    grid_spec = pltpu.PrefetchScalarGridSpec(
        num_scalar_prefetch=2,
        grid=(n_tiles,),
        in_specs=[pl.BlockSpec((None, 1, TILE_ROW_COPIES), lambda i, fs, fn: (i, 0, 0), memory_space=pltpu.SMEM)]
        + [pl.BlockSpec(memory_space=pl.ANY)] * len(hn_list),
        out_specs=pl.BlockSpec(memory_space=pl.ANY),
        scratch_shapes=[pltpu.VMEM((DISPATCH_BUFFERS, ROW_TILE * ROW_SUBLANES, LANES), F32),
                        pltpu.VMEM((ROW_SUBLANES, LANES), F32),
                        pltpu.SemaphoreType.DMA((DISPATCH_BUFFERS,)), pltpu.SemaphoreType.DMA((DISPATCH_BUFFERS,)),
                        pltpu.SemaphoreType.DMA((1,))],
    )
    return pl.pallas_call(
        functools.partial(_dispatch_kernel, tiles_per_source=tiles_per_source,
                          fill_rows=n_slots - n_tiles * TILE_ROW_COPIES),
        grid_spec=grid_spec,
        out_shape=jax.ShapeDtypeStruct((n_slots * ROW_SUBLANES, LANES), F32),
        name="dispatch_rows",
        compiler_params=_params("arbitrary", disable_bounds_checks=True),
    )(fill_start, fill_n, dest8, *hn_list)


EXPERT_ROW_BUFFERS = 3


def _expert_kernel(bexp_ref, bval_ref, fetch_ref, cast_ref, cset_ref, set_ref, next_ref, xs_hbm, wg_hbm, wu_hbm, wd_hbm, y_hbm,
                   wg_f, wu_f, wd_f, wg_b, wu_b, wd_b, xbuf, ybuf, in_sem, out_sem, w_sem, *, n):
    i = pl.program_id(0)
    block_rows = MOE_BLOCK * ROW_SUBLANES
    buf = i % EXPERT_ROW_BUFFERS

    def weight_copies(e):
        return [pltpu.make_async_copy(src.at[e], dst, w_sem.at[j])
                for j, (src, dst) in enumerate(((wg_hbm, wg_f), (wu_hbm, wu_f), (wd_hbm, wd_f)))]

    def cast_staged(which):
        for c in weight_copies(0):
            c.wait()
        wg_b[which] = wg_f[...].astype(BF16)
        wu_b[which] = wu_f[...].astype(BF16)
        wd_b[which] = wd_f[...].astype(BF16)

    def block_of(ref, j):
        return ref.at[pl.ds(pl.multiple_of(j * block_rows, block_rows), block_rows)]

    def copy_in(j):
        b = j % EXPERT_ROW_BUFFERS
        return pltpu.make_async_copy(block_of(xs_hbm, j), xbuf.at[b], in_sem.at[b])

    def copy_out(j):
        b = j % EXPERT_ROW_BUFFERS
        return pltpu.make_async_copy(ybuf.at[b], block_of(y_hbm, j), out_sem.at[b])

    @pl.when(i == 0)
    def _():
        for j in range(min(EXPERT_ROW_BUFFERS - 1, n)):
            copy_in(j).start()

    ahead = i + EXPERT_ROW_BUFFERS - 1
    pl.when(ahead < n)(lambda: copy_in(ahead).start())

    @pl.when(i == 0)
    def _():
        for c in weight_copies(bexp_ref[0]):
            c.start()
        cast_staged(0)

    pl.when(cast_ref[i] == 1)(lambda: cast_staged(cset_ref[i]))

    @pl.when(fetch_ref[i] == 1)
    def _():
        for c in weight_copies(next_ref[i]):
            c.start()

    which = set_ref[i]
    copy_in(i).wait()
    behind = i - (EXPERT_ROW_BUFFERS - 1)
    pl.when(behind >= 0)(lambda: copy_out(behind).wait())

    @pl.when(bval_ref[i] > 0)
    def _():
        xb = _load_rows(xbuf.at[buf], MOE_BLOCK).astype(BF16)
        gate = jnp.dot(xb, wg_b[which], preferred_element_type=F32)
        up = jnp.dot(xb, wu_b[which], preferred_element_type=F32)
        hid = (gate * jax.nn.sigmoid(gate) * up).astype(BF16)
        _store_rows(ybuf.at[buf], jnp.dot(hid, wd_b[which], preferred_element_type=F32), MOE_BLOCK)

    @pl.when(bval_ref[i] == 0)
    def _():
        ybuf[buf] = jnp.zeros((block_rows, LANES), F32)

    copy_out(i).start()

    @pl.when(i == n - 1)
    def _():
        for back in range(EXPERT_ROW_BUFFERS - 1):
            pl.when(i - back >= 0)(lambda back=back: copy_out(i - back).wait())


WEIGHT_CAST_LAG = 3


def _weight_schedule(block_expert):
    n = block_expert.shape[0]
    idx = jnp.arange(n, dtype=jnp.int32)
    run_start = jnp.concatenate([jnp.ones((1,), bool), block_expert[1:] != block_expert[:-1]])
    run_id = jnp.cumsum(run_start.astype(jnp.int32)) - 1
    run_first = jnp.min(jnp.where(run_id[None, :] == idx[:, None], idx[None, :], n), axis=1)
    next_first = jnp.concatenate([run_first[1:], jnp.full((1,), n, jnp.int32)])
    has_next = next_first < n
    cast_at = jnp.where(has_next, jnp.minimum(run_first + WEIGHT_CAST_LAG, next_first), -1)
    hit = idx[:, None] == cast_at[None, :]
    cast = jnp.any(hit, axis=1).astype(jnp.int32)
    cast_set = jnp.sum(jnp.where(hit, (idx[None, :] + 1) % 2, 0), axis=1).astype(jnp.int32)
    fetch = (run_start & has_next[run_id]).astype(jnp.int32)
    next_expert = block_expert[jnp.minimum(next_first[run_id], n - 1)]
    return fetch, cast, cast_set, run_id % 2, next_expert


def _routed_experts(xs, block_expert, block_valid, wg, wu, wd):
    n_blocks = block_expert.shape[0]
    shape = (EXPERT_ROW_BUFFERS, MOE_BLOCK * ROW_SUBLANES, LANES)
    any_spec = pl.BlockSpec(memory_space=pl.ANY)
    grid_spec = pltpu.PrefetchScalarGridSpec(
        num_scalar_prefetch=7,
        grid=(n_blocks,),
        in_specs=[any_spec, any_spec, any_spec, any_spec],
        out_specs=any_spec,
        scratch_shapes=[pltpu.VMEM((D_MODEL, D_EXPERT), F32), pltpu.VMEM((D_MODEL, D_EXPERT), F32),
                        pltpu.VMEM((D_EXPERT, D_MODEL), F32),
                        pltpu.VMEM((2, D_MODEL, D_EXPERT), BF16), pltpu.VMEM((2, D_MODEL, D_EXPERT), BF16),
                        pltpu.VMEM((2, D_EXPERT, D_MODEL), BF16),
                        pltpu.VMEM(shape, F32), pltpu.VMEM(shape, F32),
                        pltpu.SemaphoreType.DMA((EXPERT_ROW_BUFFERS,)),
                        pltpu.SemaphoreType.DMA((EXPERT_ROW_BUFFERS,)),
                        pltpu.SemaphoreType.DMA((3,))],
    )
    return pl.pallas_call(
        functools.partial(_expert_kernel, n=n_blocks),
        grid_spec=grid_spec,
        out_shape=jax.ShapeDtypeStruct(xs.shape, F32),
        name="routed_experts",
        compiler_params=_params("arbitrary"),
    )(block_expert, block_valid, *_weight_schedule(block_expert), xs, wg, wu, wd)


def _combine_kernel(dest_ref, dest_next_ref, x1_ref, route_ref, g_ref, yb_hbm, o_ref, ybuf, sem, *, n):
    i = pl.program_id(0)
    slot = i % 2

    def row_copy(idx8, r, k, s):
        dst = ybuf.at[s, pl.ds((k * ROW_TILE + r) * ROW_SUBLANES, ROW_SUBLANES)]
        return pltpu.make_async_copy(_row_tile(yb_hbm, idx8), dst, sem.at[s])

    @pl.when(i == 0)
    def _():
        def body(r, c):
            idx = [dest_ref[0, k * ROW_TILE + r] for k in range(TOP_K)]
            for k in range(TOP_K):
                row_copy(idx[k], r, k, 0).start()
            return c
        lax.fori_loop(0, ROW_TILE, body, 0)

    @pl.when(i + 1 < n)
    def _():
        for r0 in range(0, ROW_TILE, INDEX_BATCH):
            idx = [[dest_next_ref[0, k * ROW_TILE + r] for r in range(r0, r0 + INDEX_BATCH)] for k in range(TOP_K)]
            for r in range(r0, r0 + INDEX_BATCH):
                for k in range(TOP_K):
                    row_copy(idx[k][r - r0], r, k, 1 - slot).start(priority=k % DMA_THREADS)

    _tile_wait(yb_hbm, ybuf.at[slot], sem.at[slot])
    route = route_ref[...]
    buf = ybuf.at[slot]
    y = (_load_rows(buf, ROW_TILE) * route[:, 2:3]
         + _load_rows(buf, ROW_TILE, offset=ROW_TILE * ROW_SUBLANES) * route[:, 3:4])
    x2 = x1_ref[...] + y
    ms = jnp.mean(x2 * x2, axis=-1, keepdims=True)
    o_ref[...] = x2 * lax.rsqrt(ms + EPS) * g_ref[...]


def _combine_and_norm(dest8, x1, route, norm_w, yb):
    t = x1.shape[0]
    n_tiles = t // ROW_TILE
    dest_spec = lambda f: pl.BlockSpec((None, 1, TILE_ROW_COPIES), lambda i: (f(i), 0, 0), memory_space=pltpu.SMEM)
    return pl.pallas_call(
        functools.partial(_combine_kernel, n=n_tiles),
        grid=(n_tiles,),
        in_specs=[dest_spec(lambda i: i), dest_spec(lambda i: jnp.minimum(i + 1, n_tiles - 1)),
                  pl.BlockSpec((ROW_TILE, D_MODEL), lambda i: (i, 0)),
                  pl.BlockSpec((ROW_TILE, LANES), lambda i: (i, 0)),
                  _const_spec((1, D_MODEL)),
                  pl.BlockSpec(memory_space=pl.ANY)],
        out_specs=pl.BlockSpec((ROW_TILE, D_MODEL), lambda i: (i, 0)),
        out_shape=jax.ShapeDtypeStruct((t, D_MODEL), F32),
        scratch_shapes=[pltpu.VMEM((2, TILE_ROW_COPIES * ROW_SUBLANES, LANES), F32),
                        pltpu.SemaphoreType.DMA((2,))],
        name="combine_and_norm",
        compiler_params=_params("arbitrary", disable_bounds_checks=True),
    )(dest8, dest8, x1, route, norm_w, yb)


def _dispatch_plan(route_t, counts):
    t = route_t.shape[1]
    counts = counts[0, 0:N_EXPERTS].astype(jnp.int32)
    padded = (counts + MOE_BLOCK - 1) // MOE_BLOCK * MOE_BLOCK
    pad_end = jnp.cumsum(padded)
    pad_start = pad_end - padded
    expert = route_t[0:TOP_K].astype(jnp.int32)
    rank = route_t[4:4 + TOP_K].astype(jnp.int32)
    start_of = sum(jnp.where(expert == e, pad_start[e], 0) for e in range(N_EXPERTS))
    dest8 = (start_of + rank) * ROW_SUBLANES
    dest8 = dest8.reshape(TOP_K, t // ROW_TILE, ROW_TILE).transpose(1, 0, 2).reshape(t // ROW_TILE, 1, TILE_ROW_COPIES)
    n_blocks = -(-(t * TOP_K + N_EXPERTS * (MOE_BLOCK - 1)) // MOE_BLOCK)
    block_start = jnp.arange(n_blocks, dtype=jnp.int32) * MOE_BLOCK
    block_expert = jnp.minimum(jnp.sum((pad_end[None, :] <= block_start[:, None]).astype(jnp.int32), axis=1),
                               N_EXPERTS - 1)
    block_valid = (block_start < pad_end[-1]).astype(jnp.int32)
    n_slots = n_blocks * MOE_BLOCK
    fill_start = jnp.concatenate([pad_start + counts, pad_end[-1:]])
    fill_n = jnp.concatenate([padded - counts, n_slots - pad_end[-1:]])
    return dest8, fill_start, fill_n, block_expert, block_valid, n_slots


def _rel_bias_band(rel_table, qc):
    rows = qc * CHUNK
    window = ATTN_REACH + rows
    n = window + rows - 1
    diag = rel_table[:, np.clip(ATTN_REACH + rows - 1 - np.arange(n), -MAX_REL, MAX_REL) + MAX_REL].astype(F32)
    shifted = jnp.tile(diag, (1, rows + 1))[:, :rows * (n + 1)].reshape(N_HEADS_A, rows, n + 1)[:, :, :window]
    bias = shifted[:, ::-1, :] * LOG2E
    i = np.arange(rows)[:, None]
    j = np.arange(window)[None, :]
    first = (i // CHUNK) * CHUNK
    return jnp.where((j >= first) & (j < first + BAND), bias, NEG_INF)


def _pad_lanes(v, width=LANES):
    return jnp.pad(v.astype(F32), (0, width - v.shape[0])).reshape(1, width)


def _mixers(x, kv_cache, conv_state, ssm_state, w):
    batch, l, _ = x.shape
    nc = l // CHUNK
    t = batch * l
    x2d = x.reshape(t, D_MODEL)
    proj, dt_raw = _in_projection(x2d, w["norm_mix"], w["w_proj"])
    proj3 = proj.reshape(batch, l, PROJ_WIDTH)
    k_new = lambda rows: proj3[:, rows, K0:K0 + ATTN_WIDTH].astype(F32)
    v_new = lambda rows: proj3[:, rows, V0:V0 + ATTN_WIDTH].astype(F32)
    new_part = lambda col: (proj, pl.BlockSpec((l, ATTN_WIDTH), lambda b, c: (b, col // ATTN_WIDTH)), l, False)
    heads = lambda u: u.reshape(batch, -1, N_HEADS_A, HEAD_DIM_A)
    if kv_cache is None:
        qc = ATTN_CHUNKS_PER_STEP if nc % ATTN_CHUNKS_PER_STEP == 0 else 1
        attn, = _band_attention(proj, Q0 // ATTN_WIDTH, [new_part(K0)], [new_part(V0)],
                                _rel_bias_band(w["rel_table"], qc), batch=batch, nc=nc, pad_rows=ATTN_REACH, qc=qc)
        keep = slice(l - min(ATTN_REACH, l), l)
        k_keep, v_keep = heads(k_new(keep)), heads(v_new(keep))
    else:
        assert nc == 1
        wlen = kv_cache[0].shape[1]
        cache_part = lambda u: (u.reshape(batch, wlen * N_HEADS_A, HEAD_DIM_A),
                                pl.BlockSpec((None, wlen * N_HEADS_A, HEAD_DIM_A), lambda b, c: (b, 0, 0)), wlen, True)
        attn, k_keep, v_keep = _band_attention(
            proj, Q0 // ATTN_WIDTH, [cache_part(kv_cache[0]), new_part(K0)], [cache_part(kv_cache[1]), new_part(V0)],
            _rel_bias_band(w["rel_table"], 1), batch=batch, nc=nc, pad_rows=BAND - (wlen + l), qc=1, roll_out=True)
        k_keep, v_keep = heads(k_keep), heads(v_keep)
    tail = proj3[:, l - (CONV_WIDTH - 1):, XBC0:XBC0 + CONV_DIM].astype(F32)
    if conv_state is None:
        conv0 = st0 = None
        conv_new = tail
    else:
        conv0 = jnp.pad(conv_state, ((0, 0), (CARRY_ROWS - (CONV_WIDTH - 1), 0), (0, 0)))
        st0 = jnp.transpose(ssm_state.reshape(batch, D_INNER, D_STATE), (0, 2, 1))
        conv_new = jnp.concatenate([conv_state, tail], axis=1)[:, -(CONV_WIDTH - 1):]
    ssm, st_t = _ssd_mixer(proj, dt_raw, conv0, st0, w["conv_w"], w["conv_b"], w["dt_bias"], w["a_log"], w["d_exp"],
                           w["ssm_norm"], batch=batch, nc=nc)
    ssm_new = jnp.transpose(st_t, (0, 2, 1)).reshape(batch, SSM_HEADS, SSM_HEAD_DIM, D_STATE)
    return (x2d, attn, ssm, proj), (k_keep, v_keep, conv_new, ssm_new)


def kernel(x_prompt, x_sample, cache_attn_k, cache_attn_v, state_conv, state_ssm, norm_mix, w_in, rel_bias, conv_w, conv_b, dt_bias, a_log, d_skip, ssm_norm, w_branch_a, w_branch_b, w_out, norm_ffn, w_router_group, b_router_group, w_router_expert, b_router_expert, w_gate, w_up, w_down, norm_final):
    assert norm_mix.shape[0] == 1, "single-layer trunk"
    q, k, v, z, xbc, dtc, ga, gb = jnp.split(w_in[0], np.cumsum(
        (ATTN_WIDTH, ATTN_WIDTH, ATTN_WIDTH, D_INNER, CONV_DIM, SSM_HEADS, D_MODEL, D_MODEL))[:-1].tolist(), axis=1)
    w_proj = jnp.concatenate([z, ga, gb, xbc, q, k, v, jnp.pad(dtc, ((0, 0), (0, LANES - SSM_HEADS)))],
                             axis=1).astype(BF16)
    w_router = jnp.pad(jnp.concatenate([w_router_group[0], w_router_expert[0]], axis=1),
                       ((0, 0), (0, LANES - N_EXPERT_GROUPS - N_EXPERTS)))
    w_router_hi = w_router.astype(BF16)
    w_router_lo = (w_router - w_router_hi.astype(F32)).astype(BF16)
    b_router = _pad_lanes(jnp.concatenate([b_router_group[0], b_router_expert[0]]))
    w = dict(
        norm_mix=norm_mix[0].reshape(1, D_MODEL), w_proj=w_proj, rel_table=rel_bias[0],
        conv_w=conv_w[0], conv_b=conv_b[0].reshape(1, CONV_DIM), dt_bias=_pad_lanes(dt_bias[0]),
        a_log=_pad_lanes(a_log[0]), d_exp=jnp.repeat(d_skip[0], SSM_HEAD_DIM).reshape(1, D_INNER),
        ssm_norm=ssm_norm[0].reshape(1, D_INNER),
    )
    mix_w = (w_branch_a[0].astype(BF16), w_branch_b[0].astype(BF16), w_out[0].astype(BF16),
             norm_ffn[0].reshape(1, D_MODEL), jnp.concatenate([w_router_hi, w_router_lo], axis=1), b_router)
    norm_out = norm_final.reshape(1, D_MODEL)

    groups = [_mixers(x_prompt, None, None, None, w),
              _mixers(x_sample, (cache_attn_k[0], cache_attn_v[0]), state_conv[0], state_ssm[0], w)]
    counts = jnp.zeros((1, LANES), F32)
    mixed = []
    routes_t = []
    for acts, _ in groups:
        x1, hn, route, route_t, counts = _mix_and_route(*acts, *mix_w, counts)
        mixed.append((x1, hn, route))
        routes_t.append(route_t)
    dest8, fill_start, fill_n, block_expert, block_valid, n_slots = _dispatch_plan(
        jnp.concatenate(routes_t, axis=1), counts)
    xs = _dispatch(dest8, fill_start, fill_n, [m[1] for m in mixed], n_slots)
    yb = _routed_experts(xs, block_expert, block_valid, w_gate[0], w_up[0], w_down[0])
    outs, first = [], 0
    for (x1, _, route), (acts, _) in zip(mixed, groups):
        n_tiles = x1.shape[0] // ROW_TILE
        y = _combine_and_norm(dest8[first:first + n_tiles], x1, route, norm_out, yb)
        first += n_tiles
        outs.append(y)
    (yp, ys), ((kp, vp, cp, sp), (ks, vs, cs, ss)) = outs, [g[1] for g in groups]
    yp = yp.reshape(x_prompt.shape)
    ys = ys.reshape(x_sample.shape)
    return (yp, ys, kp[None], vp[None], cp[None], sp[None], ks[None], vs[None], cs[None], ss[None])
```

```python
import functools
import math

import numpy as np
import jax
import jax.numpy as jnp
from jax import lax
from jax.experimental import pallas as pl
from jax.experimental.pallas import tpu as pltpu

F32 = jnp.float32
BF16 = jnp.bfloat16

D_MODEL = 1024
CHUNK = 64
LEFT_CHUNKS = 8
ATTN_REACH = LEFT_CHUNKS * CHUNK
BAND = ATTN_REACH + CHUNK
N_HEADS_A = 8
HEAD_DIM_A = 64
ATTN_WIDTH = N_HEADS_A * HEAD_DIM_A
MAX_REL = 256
ATTN_SCALE = 1.0 / math.sqrt(HEAD_DIM_A)
NEG_INF = -1e30
SSM_HEADS = 16
SSM_HEAD_DIM = 64
D_INNER = SSM_HEADS * SSM_HEAD_DIM
SSM_GROUPS = 2
GROUP_WIDTH = D_INNER // SSM_GROUPS
D_STATE = 128
CONV_WIDTH = 4
CONV_DIM = D_INNER + 2 * SSM_GROUPS * D_STATE
N_EXPERT_GROUPS = 4
EXPERTS_PER_GROUP = 8
N_EXPERTS = N_EXPERT_GROUPS * EXPERTS_PER_GROUP
TOP_K = 2
D_EXPERT = 512
MOE_BLOCK = 256
EPS = 1e-6

LANES = 128
SUBLANES = 8
Z0, GA0, GB0, XBC0, Q0, K0, V0 = 0, 1024, 2048, 3072, 4608, 5120, 5632
PROJ_WIDTH = V0 + ATTN_WIDTH
PROJ_TILE = 1024
ROW_TILE = 256
ATTN_CHUNKS_PER_STEP = 2
SSD_STREAMS = 4
VMEM_LIMIT = 56 * 1024 * 1024

_NT = (((1,), (1,)), ((), ()))


def _params(*sem, **kw):
    return pltpu.CompilerParams(dimension_semantics=sem, vmem_limit_bytes=VMEM_LIMIT, **kw)


def _const_spec(shape):
    nd = len(shape)
    return pl.BlockSpec(shape, lambda *_: (0,) * nd)


def _inproj_kernel(x_ref, g_ref, w_ref, o_ref, dt_ref):
    x = x_ref[...]
    ms = jnp.mean(x * x, axis=-1, keepdims=True)
    h = (x * lax.rsqrt(ms + EPS) * g_ref[...]).astype(BF16)
    for j in range(0, PROJ_WIDTH, PROJ_TILE):
        o_ref[:, j:j + PROJ_TILE] = jnp.dot(h, w_ref[:, j:j + PROJ_TILE], preferred_element_type=F32).astype(BF16)
    dt_ref[...] = jnp.dot(h, w_ref[:, PROJ_WIDTH:PROJ_WIDTH + LANES], preferred_element_type=F32)


PROJ_ROWS = 512


def _in_projection(x2d, norm_w, w_proj):
    t = x2d.shape[0]
    rows = PROJ_ROWS if t % PROJ_ROWS == 0 else ROW_TILE
    return pl.pallas_call(
        _inproj_kernel,
        grid=(t // rows,),
        in_specs=[pl.BlockSpec((rows, D_MODEL), lambda i: (i, 0)),
                  _const_spec((1, D_MODEL)),
                  _const_spec((D_MODEL, PROJ_WIDTH + LANES))],
        out_specs=[pl.BlockSpec((rows, PROJ_WIDTH), lambda i: (i, 0)),
                   pl.BlockSpec((rows, LANES), lambda i: (i, 0))],
        out_shape=[jax.ShapeDtypeStruct((t, PROJ_WIDTH), BF16), jax.ShapeDtypeStruct((t, LANES), F32)],
        name="in_projection",
        compiler_params=_params("parallel"),
    )(x2d, norm_w, w_proj)


LOG2E = math.log2(math.e)
ATTN_STAGES = 3


def _attn_kernel(q_ref, *refs, pad_rows, part_rows, qc, roll_out):
    n_parts = len(part_rows)
    k_parts, v_parts = refs[:n_parts], refs[n_parts:2 * n_parts]
    bias_ref, o_ref = refs[2 * n_parts:2 * n_parts + 2]
    rolled = refs[2 * n_parts + 2:-2]
    kpad, vpad = refs[-2:]
    rows = qc * CHUNK
    window = ATTN_REACH + rows
    c = pl.program_id(1)

    @pl.when(c == 0)
    def _():
        if pad_rows:
            kpad[0:pad_rows, :] = jnp.zeros((pad_rows, ATTN_WIDTH), BF16)
            vpad[0:pad_rows, :] = jnp.zeros((pad_rows, ATTN_WIDTH), BF16)
        first = pad_rows
        for k_ref, v_ref, n in zip(k_parts, v_parts, part_rows):
            kpad[first:first + n, :] = k_ref[...].astype(BF16)
            vpad[first:first + n, :] = v_ref[...].astype(BF16)
            first += n
        if roll_out:
            kept, new = part_rows[0] - part_rows[1], part_rows[1]
            for out_ref, parts in zip(rolled, (k_parts, v_parts)):
                out_ref[0:kept, :] = parts[0][new:new + kept, :]
                out_ref[kept:kept + new, :] = parts[1][...].astype(F32)

    start = pl.multiple_of(c * rows, rows)

    def body(mask_start):
        kw = kpad[pl.ds(start, window), :]
        vw = vpad[pl.ds(start, window), :]
        q = (q_ref[...].astype(F32) * (ATTN_SCALE * LOG2E)).astype(BF16)
        low = lax.broadcasted_iota(jnp.int32, (rows, LANES), 1) < HEAD_DIM_A
        if mask_start:
            valid = lax.broadcasted_iota(jnp.int32, (rows, window), 1) + start >= pad_rows
        outs = {}

        def head(h):
            hp, sub = divmod(h, 2)
            sl = slice(LANES * hp, LANES * (hp + 1))
            q2 = q[:, sl]
            qm = jnp.where(low if sub == 0 else jnp.logical_not(low), q2, jnp.zeros_like(q2))
            s = lax.dot_general(qm, kw[:, sl], _NT, preferred_element_type=F32) + bias_ref[h]
            if mask_start:
                s = jnp.where(valid, s, NEG_INF)
            yield
            m = jnp.max(s, axis=-1, keepdims=True)
            p = jnp.exp2(s - m)
            l = jnp.sum(p, axis=-1, keepdims=True)
            pb = p.astype(BF16)
            yield
            outs[h] = jnp.dot(pb, vw[:, sl], preferred_element_type=F32) / l
            if sub == 1:
                o_ref[:, sl] = jnp.where(low, outs[h - 1], outs[h]).astype(o_ref.dtype)

        heads = [head(h) for h in range(N_HEADS_A)]
        for t in range(N_HEADS_A + ATTN_STAGES - 1):
            for h in range(min(t, N_HEADS_A - 1), max(t - ATTN_STAGES, -1), -1):
                next(heads[h], None)

    if pad_rows:
        pl.when(start < pad_rows)(lambda: body(True))
        pl.when(start >= pad_rows)(lambda: body(False))
    else:
        body(False)


def _band_attention(q_arr, q_col, k_parts, v_parts, bias, *, batch, nc, pad_rows, qc, roll_out=False):
    steps = nc // qc
    rows = qc * CHUNK
    part_rows = tuple(p[2] for p in k_parts)
    lk = sum(part_rows)
    kern = functools.partial(_attn_kernel, pad_rows=pad_rows, part_rows=part_rows, qc=qc, roll_out=roll_out)
    out_specs = [pl.BlockSpec((rows, ATTN_WIDTH), lambda b, c: (b * steps + c, 0))]
    out_shape = [jax.ShapeDtypeStruct((batch * nc * CHUNK, ATTN_WIDTH), BF16)]
    if roll_out:
        assert len(part_rows) == 2 and nc == 1
        out_specs += [pl.BlockSpec((None, part_rows[0], ATTN_WIDTH), lambda b, c: (b, 0, 0))] * 2
        out_shape += [jax.ShapeDtypeStruct((batch, part_rows[0], ATTN_WIDTH), F32)] * 2
    return pl.pallas_call(
        kern,
        grid=(batch, steps),
        in_specs=[pl.BlockSpec((rows, ATTN_WIDTH), lambda b, c: (b * steps + c, q_col))]
        + [p[1] for p in k_parts] + [p[1] for p in v_parts]
        + [_const_spec((N_HEADS_A, rows, ATTN_REACH + rows))],
        out_specs=out_specs,
        out_shape=out_shape,
        scratch_shapes=[pltpu.VMEM((pad_rows + lk, ATTN_WIDTH), BF16),
                        pltpu.VMEM((pad_rows + lk, ATTN_WIDTH), BF16)],
        name="band_attention",
        compiler_params=_params("parallel", "arbitrary"),
    )(q_arr, *[p[0] for p in k_parts], *[p[0] for p in v_parts], bias)


CARRY_ROWS = SUBLANES
SHIFT_TAPS = CONV_WIDTH - 1
SHIFT_K = 2 * CARRY_ROWS + CHUNK


def _split_bf16(x, parts):
    out = []
    for _ in range(parts - 1):
        h = x.astype(BF16)
        out.append(h)
        x = x - h.astype(F32)
    out.append(x.astype(BF16))
    return out


def _ssd_kernel(xbc_ref, z_ref, dt_ref, conv0_ref, st0_ref, cw_ref, cb_ref, dtb_ref, alog_ref,
                dexp_ref, nw_ref, tri_ref, exp_ref, sel_ref, eye_ref, shift_ref,
                y_ref, stout_ref, carry, st, *, has_init, streams, nc):
    c = pl.program_id(1)

    @pl.when(c == 0)
    def _():
        if has_init:
            carry[...] = conv0_ref[...]
            st[...] = st0_ref[...]
        else:
            carry[...] = jnp.zeros((streams, CARRY_ROWS, CONV_DIM), F32)
            st[...] = jnp.zeros((streams, D_STATE, D_INNER), F32)

    chunks = [_ssd_chunk(xbc_ref.at[s], z_ref.at[s], dt_ref.at[s], cw_ref, cb_ref, dtb_ref, alog_ref, dexp_ref,
                         nw_ref, tri_ref, exp_ref, sel_ref, eye_ref, shift_ref, y_ref.at[s], carry.at[s], st.at[s])
              for s in range(streams)]
    while chunks:
        chunks = [g for g in chunks if next(g, True) is None]

    @pl.when(c == nc - 1)
    def _():
        stout_ref[...] = st[...]


def _ssd_chunk(xbc_ref, z_ref, dt_ref, cw_ref, cb_ref, dtb_ref, alog_ref, dexp_ref, nw_ref,
               tri_ref, exp_ref, sel_ref, eye_ref, shift_ref, y_ref, carry, st):
    prev = carry[...]
    prev_hi = prev.astype(BF16).astype(F32)
    prev_parts = jnp.concatenate([prev_hi, prev - prev_hi], axis=0).astype(BF16)
    taps = jnp.dot(shift_ref[...], jnp.concatenate([prev_parts, xbc_ref[...]], axis=0), preferred_element_type=F32)

    dt_in = dt_ref[...] + dtb_ref[...]
    dt = jnp.maximum(dt_in, 0.0) + jnp.log1p(jnp.exp(-jnp.abs(dt_in)))
    da = dt * (-jnp.exp(alog_ref[...]))
    tri = tri_ref[...]
    cs3 = jnp.dot(tri, jnp.concatenate(_split_bf16(da, 3), axis=1), preferred_element_type=F32)
    yield

    cs = cs3[:, 0:LANES] + cs3[:, LANES:2 * LANES] + cs3[:, 2 * LANES:3 * LANES]
    cs_last = cs[CHUNK - 1:CHUNK, :]
    ecs = jnp.exp(cs)
    to_end = jnp.exp(cs_last - cs)

    stack = jnp.concatenate(_split_bf16(dt, 1) + _split_bf16(to_end, 1) + _split_bf16(ecs, 2)
                            + _split_bf16(cs, 3), axis=0)
    wide = jnp.dot(stack, exp_ref[...], preferred_element_type=F32)

    lane = lax.broadcasted_iota(jnp.int32, (CHUNK, LANES), 1)
    even = (lane % 2) == 0
    csm = jnp.concatenate([jnp.where(even, cs, 0.0), jnp.where(even, 0.0, cs)], axis=0)
    srow3 = lax.dot_general(sel_ref[...], jnp.concatenate(_split_bf16(csm, 3), axis=0), _NT,
                            preferred_element_type=F32)
    yield

    rows = [wide[i * CHUNK:(i + 1) * CHUNK, :] for i in range(7)]
    dt_w = rows[0]
    to_end_w = rows[1]
    ecs_w = rows[2] + rows[3]
    cs_w = rows[4] + rows[5] + rows[6]
    srow = srow3[:, 0:LANES] + srow3[:, LANES:2 * LANES] + srow3[:, 2 * LANES:3 * LANES]

    xdt_b, xw_b, skip, bc = [], [], [], []
    for j in range(CONV_DIM // LANES):
        sl = slice(j * LANES, (j + 1) * LANES)
        conv = cb_ref[:, sl] + xbc_ref[:, sl].astype(F32) * cw_ref[CONV_WIDTH - 1:CONV_WIDTH, sl]
        for tap in range(SHIFT_TAPS):
            conv = conv + taps[tap * CHUNK:(tap + 1) * CHUNK, sl] * cw_ref[tap:tap + 1, sl]
        u = conv * jax.nn.sigmoid(conv)
        if j < D_INNER // LANES:
            xdt = u * dt_w[:, sl]
            xdt_b.append(xdt.astype(BF16))
            xw_b.append((xdt * to_end_w[:, sl]).astype(BF16))
            skip.append(u * dexp_ref[:, sl])
        else:
            bc.append(u.astype(BF16))
    low = lane < SSM_HEAD_DIM
    t_idx = lax.broadcasted_iota(jnp.int32, (CHUNK, LANES), 0)
    causal = t_idx >= (lane % SSM_HEAD_DIM)
    eye = eye_ref[...]
    pairs = SSM_HEADS // SSM_GROUPS // 2
    yield

    for g in range(SSM_GROUPS):
        b_g, c_g = bc[g], bc[SSM_GROUPS + g]
        gsl = slice(g * GROUP_WIDTH, (g + 1) * GROUP_WIDTH)
        st_g = st[:, gsl]
        y_off = jnp.dot(c_g, st_g.astype(BF16), preferred_element_type=F32)
        b_t = lax.dot_general(eye, b_g, _NT, preferred_element_type=F32).astype(BF16)
        xw_g = jnp.concatenate(xw_b[g * pairs:(g + 1) * pairs], axis=1)
        st[:, gsl] = st_g * ecs_w[CHUNK - 1:CHUNK, gsl] + jnp.dot(b_t, xw_g, preferred_element_type=F32)
        cb2 = lax.dot_general(c_g, jnp.concatenate([b_g, b_g], axis=0), _NT, preferred_element_type=F32)
        yield
        gated, sumsq = [], 0.0
        for kk in range(pairs):
            k = g * pairs + kk
            psl = slice(k * LANES, (k + 1) * LANES)
            decay = jnp.where(causal, jnp.exp(cs_w[:, psl] - srow[k:k + 1, :]), 0.0)
            gmat = (cb2 * decay).astype(BF16)
            xp = xdt_b[k]
            xblk = jnp.concatenate([jnp.where(low, xp, jnp.zeros_like(xp)),
                                    jnp.where(low, jnp.zeros_like(xp), xp)], axis=0)
            y = (jnp.dot(gmat, xblk, preferred_element_type=F32)
                 + y_off[:, kk * LANES:(kk + 1) * LANES] * ecs_w[:, psl] + skip[k])
            zz = z_ref[:, psl].astype(F32)
            y = y * (zz * jax.nn.sigmoid(zz))
            gated.append(y)
            sumsq = sumsq + jnp.sum(y * y, axis=-1, keepdims=True)
        scale = lax.rsqrt(sumsq * (1.0 / GROUP_WIDTH) + EPS)
        for kk in range(pairs):
            psl = slice((g * pairs + kk) * LANES, (g * pairs + kk + 1) * LANES)
            y_ref[:, psl] = (gated[kk] * scale * nw_ref[:, psl]).astype(y_ref.dtype)
        yield
    carry[...] = xbc_ref[CHUNK - 2 * CARRY_ROWS:CHUNK, :].astype(F32)[CARRY_ROWS:, :]


def _ssd_constants():
    tri = np.tril(np.ones((CHUNK, CHUNK), np.float32))
    expand = np.zeros((LANES, D_INNER), np.float32)
    for e in range(SSM_HEADS):
        expand[e, e * SSM_HEAD_DIM:(e + 1) * SSM_HEAD_DIM] = 1.0
    sel = np.zeros((16, LANES), np.float32)
    for e in range(SSM_HEADS):
        sel[e // 2, e] = 1.0
    eye = np.eye(D_STATE, dtype=np.float32)
    shift = np.zeros((SHIFT_TAPS * CHUNK, SHIFT_K), np.float32)
    for j in range(SHIFT_TAPS):
        for t in range(CHUNK):
            m = t - (CONV_WIDTH - 1) + j
            if m >= 0:
                shift[j * CHUNK + t, 2 * CARRY_ROWS + m] = 1.0
            else:
                shift[j * CHUNK + t, CARRY_ROWS + m] = 1.0
                shift[j * CHUNK + t, 2 * CARRY_ROWS + m] = 1.0
    return tuple(jnp.asarray(a, BF16) for a in (tri, expand, sel, eye, shift))


def _ssd_mixer(proj, dt_raw, conv0, st0, conv_w, conv_b, dt_bias_p, a_log_p, d_exp, norm_w, *, batch, nc):
    streams = SSD_STREAMS if batch % SSD_STREAMS == 0 else 1
    groups = batch // streams
    l = nc * CHUNK
    has_init = conv0 is not None
    if not has_init:
        conv0 = jnp.zeros((1, streams, CARRY_ROWS, CONV_DIM), F32)
        st0 = jnp.zeros((1, streams, D_STATE, D_INNER), F32)
        init_map = lambda b, c: (0, 0, 0, 0)
    else:
        conv0 = conv0.reshape(groups, streams, CARRY_ROWS, CONV_DIM)
        st0 = st0.reshape(groups, streams, D_STATE, D_INNER)
        init_map = lambda b, c: (b, 0, 0, 0)
    tri, expand, sel, eye, shift = _ssd_constants()
    proj4 = proj.reshape(groups, streams, l, PROJ_WIDTH)
    dt4 = dt_raw.reshape(groups, streams, l, LANES)
    chunk_of = lambda w, col: pl.BlockSpec((None, streams, CHUNK, w), lambda b, c: (b, 0, c, col))
    state_spec = lambda m: pl.BlockSpec((None, streams, D_STATE, D_INNER), m)
    kern = functools.partial(_ssd_kernel, has_init=has_init, streams=streams, nc=nc)
    y, st_out = pl.pallas_call(
        kern,
        grid=(groups, nc),
        in_specs=[chunk_of(CONV_DIM, XBC0 // CONV_DIM), chunk_of(D_INNER, Z0 // D_INNER), chunk_of(LANES, 0),
                  pl.BlockSpec((None, streams, CARRY_ROWS, CONV_DIM), init_map), state_spec(init_map),
                  _const_spec((CONV_WIDTH, CONV_DIM)), _const_spec((1, CONV_DIM)),
                  _const_spec((1, LANES)), _const_spec((1, LANES)),
                  _const_spec((1, D_INNER)), _const_spec((1, D_INNER)),
                  _const_spec((CHUNK, CHUNK)), _const_spec((LANES, D_INNER)),
                  _const_spec((16, LANES)), _const_spec((D_STATE, D_STATE)),
                  _const_spec((SHIFT_TAPS * CHUNK, SHIFT_K))],
        out_specs=[chunk_of(D_INNER, 0), state_spec(lambda b, c: (b, 0, 0, 0))],
        out_shape=[jax.ShapeDtypeStruct((groups, streams, l, D_INNER), BF16),
                   jax.ShapeDtypeStruct((groups, streams, D_STATE, D_INNER), F32)],
        scratch_shapes=[pltpu.VMEM((streams, CARRY_ROWS, CONV_DIM), F32),
                        pltpu.VMEM((streams, D_STATE, D_INNER), F32)],
        name="ssd_mixer",
        compiler_params=_params("parallel", "arbitrary"),
    )(proj4, proj4, dt4, conv0, st0, conv_w, conv_b, dt_bias_p, a_log_p, d_exp, norm_w, tri, expand, sel, eye, shift)
    return y.reshape(batch * l, D_INNER), st_out.reshape(batch, D_STATE, D_INNER)


ROW_SUBLANES = D_MODEL // LANES
assert ROW_SUBLANES == SUBLANES


def _store_rows(ref, x, rows, offset=0):
    for s in range(ROW_SUBLANES):
        ref[pl.ds(offset + s, rows, stride=ROW_SUBLANES), :] = x[:, s * LANES:(s + 1) * LANES]


def _load_rows(ref, rows, offset=0):
    return jnp.concatenate([ref[pl.ds(offset + s, rows, stride=ROW_SUBLANES), :] for s in range(ROW_SUBLANES)],
                           axis=1)


def _row_tile(ref, idx8):
    return ref.at[pl.ds(pl.multiple_of(idx8, ROW_SUBLANES), ROW_SUBLANES)]


def _mix_kernel(x_ref, attn_ref, ssm_ref, ga_ref, gb_ref, wa_ref, wb_ref, wo_ref, g_ref, wr_ref, br_ref, slt_ref,
                count0_ref, x1_ref, hn_ref, route_ref, route_t_ref, counts_ref, carry, *, subtiles):
    @pl.when(pl.program_id(0) == 0)
    def _():
        carry[...] = count0_ref[...]

    running = {"counts": carry[...]}
    tiles = [_mix_tile(j * ROW_TILE, running, x_ref, attn_ref, ssm_ref, ga_ref, gb_ref, wa_ref, wb_ref, wo_ref, g_ref,
                       wr_ref, br_ref, slt_ref, x1_ref, hn_ref, route_ref, route_t_ref) for j in range(subtiles)]
    while tiles:
        tiles = [g for g in tiles if next(g, True) is None]
    carry[...] = running["counts"]
    counts_ref[...] = running["counts"]


def _mix_tile(r0, running, x_ref, attn_ref, ssm_ref, ga_ref, gb_ref, wa_ref, wb_ref, wo_ref, g_ref, wr_ref, br_ref,
              slt_ref, x1_ref, hn_ref, route_ref, route_t_ref):
    rows = slice(r0, r0 + ROW_TILE)
    a = jnp.dot(attn_ref[rows, :], wa_ref[...], preferred_element_type=F32)
    s = jnp.dot(ssm_ref[rows, :], wb_ref[...], preferred_element_type=F32)
    yield
    mixed = jax.nn.sigmoid(ga_ref[rows, :].astype(F32)) * a + jax.nn.sigmoid(gb_ref[rows, :].astype(F32)) * s
    x1 = x_ref[rows, :] + jnp.dot(mixed.astype(BF16), wo_ref[...], preferred_element_type=F32)
    yield
    x1_ref[rows, :] = x1
    ms = jnp.mean(x1 * x1, axis=-1, keepdims=True)
    hn = x1 * lax.rsqrt(ms + EPS) * g_ref[...]
    _store_rows(hn_ref, hn, ROW_TILE, offset=r0 * ROW_SUBLANES)

    hi = hn.astype(BF16)
    lo = (hn - hi.astype(F32)).astype(BF16)
    both = jnp.dot(hi, wr_ref[...], preferred_element_type=F32)
    lo_part = jnp.dot(lo, wr_ref[:, 0:LANES], preferred_element_type=F32)
    yield
    logits = both[:, 0:LANES] + both[:, LANES:2 * LANES] + lo_part + br_ref[...]
    lane = lax.broadcasted_iota(jnp.int32, logits.shape, 1)
    lane_f = lane.astype(F32)
    big = float(LANES)
    is_g = lane < N_EXPERT_GROUPS
    gl = jnp.where(is_g, logits, -jnp.inf)
    gmax = jnp.max(gl, axis=-1, keepdims=True)
    gidx = jnp.min(jnp.where(gl == gmax, lane_f, big), axis=-1, keepdims=True)
    gsum = jnp.sum(jnp.where(is_g, jnp.exp(gl - gmax), 0.0), axis=-1, keepdims=True)
    g_top = 1.0 / gsum
    first = N_EXPERT_GROUPS + gidx * EXPERTS_PER_GROUP
    in_group = (lane_f >= first) & (lane_f < first + EXPERTS_PER_GROUP)
    el = jnp.where(in_group, logits, -jnp.inf)
    m1 = jnp.max(el, axis=-1, keepdims=True)
    i1 = jnp.min(jnp.where(el == m1, lane_f, big), axis=-1, keepdims=True)
    el2 = jnp.where(lane_f == i1, -jnp.inf, el)
    m2 = jnp.max(el2, axis=-1, keepdims=True)
    i2 = jnp.min(jnp.where(el2 == m2, lane_f, big), axis=-1, keepdims=True)
    r = jnp.exp(m2 - m1)
    w1 = g_top / (1.0 + r)
    w2 = g_top * r / (1.0 + r)
    e1 = i1 - N_EXPERT_GROUPS
    e2 = i2 - N_EXPERT_GROUPS

    oh1 = lane_f == e1
    oh2 = lane_f == e2
    hot1 = jnp.where(oh1, 1.0, 0.0)
    hot2 = jnp.where(oh2, 1.0, 0.0)
    onehots = jnp.concatenate([hot1, hot2], axis=1).astype(BF16)
    before = jnp.dot(slt_ref[...], onehots, preferred_element_type=F32)
    cnt1 = jnp.sum(hot1, axis=0, keepdims=True)
    cnt2 = jnp.sum(hot2, axis=0, keepdims=True)
    yield
    base = running["counts"]
    rank1 = jnp.sum(jnp.where(oh1, before[:, 0:LANES] + base, 0.0), axis=-1, keepdims=True)
    rank2 = jnp.sum(jnp.where(oh2, before[:, LANES:2 * LANES] + (base + cnt1), 0.0), axis=-1, keepdims=True)
    running["counts"] = base + cnt1 + cnt2

    route = jnp.zeros_like(logits)
    for col, val in enumerate((e1, e2, w1, w2, rank1, rank2)):
        route = jnp.where(lane == col, val, route)
    route_ref[rows, :] = route
    route_t_ref[:, rows] = jnp.transpose(route)[0:SUBLANES, :]


MIX_SUBTILES = 2


def _mix_and_route(x2d, attn, ssm, proj, wa, wb, wo, norm_w, w_router, b_router, count0):
    t = x2d.shape[0]
    subtiles = MIX_SUBTILES if t % (MIX_SUBTILES * ROW_TILE) == 0 else 1
    step_rows = subtiles * ROW_TILE
    tile = lambda w, col: pl.BlockSpec((step_rows, w), lambda i: (i, col))
    slt = jnp.asarray(np.tril(np.ones((ROW_TILE, ROW_TILE), np.float32), k=-1), BF16)
    return pl.pallas_call(
        functools.partial(_mix_kernel, subtiles=subtiles),
        grid=(t // step_rows,),
        in_specs=[tile(D_MODEL, 0), tile(ATTN_WIDTH, 0), tile(D_INNER, 0),
                  tile(D_MODEL, GA0 // D_MODEL), tile(D_MODEL, GB0 // D_MODEL),
                  _const_spec((ATTN_WIDTH, D_MODEL)), _const_spec((D_INNER, D_MODEL)),
                  _const_spec((D_MODEL, D_MODEL)), _const_spec((1, D_MODEL)),
                  _const_spec((D_MODEL, 2 * LANES)), _const_spec((1, LANES)),
                  _const_spec((ROW_TILE, ROW_TILE)), _const_spec((1, LANES))],
        out_specs=[tile(D_MODEL, 0), pl.BlockSpec((step_rows * ROW_SUBLANES, LANES), lambda i: (i, 0)),
                   tile(LANES, 0), pl.BlockSpec((SUBLANES, step_rows), lambda i: (0, i)), _const_spec((1, LANES))],
        out_shape=[jax.ShapeDtypeStruct((t, D_MODEL), F32),
                   jax.ShapeDtypeStruct((t * ROW_SUBLANES, LANES), F32),
                   jax.ShapeDtypeStruct((t, LANES), F32),
                   jax.ShapeDtypeStruct((SUBLANES, t), F32),
                   jax.ShapeDtypeStruct((1, LANES), F32)],
        scratch_shapes=[pltpu.VMEM((1, LANES), F32)],
        name="mix_and_route",
        compiler_params=_params("arbitrary"),
    )(x2d, attn, ssm, proj, proj, wa, wb, wo, norm_w, w_router, b_router, slt, count0)


TILE_ROW_COPIES = ROW_TILE * TOP_K
INDEX_BATCH = 8
DISPATCH_BUFFERS = 3
DMA_THREADS = 2


def _tile_wait(ref_a, ref_b, sem):
    n = TILE_ROW_COPIES * ROW_SUBLANES
    pltpu.make_async_copy(ref_a.at[pl.ds(0, n)], ref_b.at[pl.ds(0, n)], sem).wait()


def _dispatch_kernel(fill_start_ref, fill_n_ref, dest_ref, *refs, tiles_per_source, fill_rows):
    n_src = len(tiles_per_source)
    sources = refs[:n_src]
    xs_hbm, hbuf, zbuf, in_sem, out_sem, fill_sem = refs[n_src:]
    i = pl.program_id(0)
    n = pl.num_programs(0)
    tile_rows = ROW_TILE * ROW_SUBLANES

    def tile_copy(src, j, b):
        return pltpu.make_async_copy(src.at[pl.ds(pl.multiple_of(j * tile_rows, tile_rows), tile_rows)],
                                     hbuf.at[b], in_sem.at[b])

    def tile_in_start(j):
        b = j % DISPATCH_BUFFERS
        first = 0
        for src, count in zip(sources, tiles_per_source):
            pl.when((j >= first) & (j < first + count))(lambda src=src, first=first: tile_copy(src, j - first, b).start())
            first += count

    @pl.when(i == 0)
    def _():
        zbuf[...] = jnp.zeros((ROW_SUBLANES, LANES), F32)
        tile_in_start(i)
        pl.when(n > 1)(lambda: tile_in_start(i + 1))

    buf = i % DISPATCH_BUFFERS
    tile_copy(sources[0], 0, buf).wait()
    for r0 in range(0, ROW_TILE, INDEX_BATCH):
        dests = [[dest_ref[0, k * ROW_TILE + r] for r in range(r0, r0 + INDEX_BATCH)] for k in range(TOP_K)]
        for r in range(r0, r0 + INDEX_BATCH):
            src = hbuf.at[buf, pl.ds(r * ROW_SUBLANES, ROW_SUBLANES)]
            for k in range(TOP_K):
                dst = _row_tile(xs_hbm, dests[k][r - r0])
                pltpu.make_async_copy(src, dst, out_sem.at[buf]).start(priority=k % DMA_THREADS)

    @pl.when(i > 0)
    def _():
        _tile_wait(xs_hbm, xs_hbm, out_sem.at[(i - 1) % DISPATCH_BUFFERS])

    pl.when(i + 2 < n)(lambda: tile_in_start(i + 2))

    @pl.when(i == n - 1)
    def _():
        _tile_wait(xs_hbm, xs_hbm, out_sem.at[buf])

        def per_range(e, carry):
            first = fill_start_ref[e]

            def start(j, c):
                pltpu.make_async_copy(zbuf, _row_tile(xs_hbm, (first + j) * ROW_SUBLANES), fill_sem.at[0]).start()
                return c

            lax.fori_loop(0, fill_n_ref[e], start, 0)
            return carry

        lax.fori_loop(0, N_EXPERTS + 1, per_range, 0)
        filled = pl.ds(0, fill_rows * ROW_SUBLANES)
        pltpu.make_async_copy(xs_hbm.at[filled], xs_hbm.at[filled], fill_sem.at[0]).wait()


def _dispatch(dest8, fill_start, fill_n, hn_list, n_slots):
    n_tiles = dest8.shape[0]
    tiles_per_source = tuple(h.shape[0] // (ROW_TILE * ROW_SUBLANES) for h in hn_list)
    assert sum(tiles_per_source) == n_tiles
    grid_spec = pltpu.PrefetchScalarGridSpec(
        num_scalar_prefetch=2,
        grid=(n_tiles,),
        in_specs=[pl.BlockSpec((None, 1, TILE_ROW_COPIES), lambda i, fs, fn: (i, 0, 0), memory_space=pltpu.SMEM)]
        + [pl.BlockSpec(memory_space=pl.ANY)] * len(hn_list),
        out_specs=pl.BlockSpec(memory_space=pl.ANY),
        scratch_shapes=[pltpu.VMEM((DISPATCH_BUFFERS, ROW_TILE * ROW_SUBLANES, LANES), F32),
                        pltpu.VMEM((ROW_SUBLANES, LANES), F32),
                        pltpu.SemaphoreType.DMA((DISPATCH_BUFFERS,)), pltpu.SemaphoreType.DMA((DISPATCH_BUFFERS,)),
                        pltpu.SemaphoreType.DMA((1,))],
    )
    return pl.pallas_call(
        functools.partial(_dispatch_kernel, tiles_per_source=tiles_per_source,
                          fill_rows=n_slots - n_tiles * TILE_ROW_COPIES),
        grid_spec=grid_spec,
        out_shape=jax.ShapeDtypeStruct((n_slots * ROW_SUBLANES, LANES), F32),
        name="dispatch_rows",
        compiler_params=_params("arbitrary", disable_bounds_checks=True),
    )(fill_start, fill_n, dest8, *hn_list)


EXPERT_ROW_BUFFERS = 3


def _expert_kernel(bexp_ref, bval_ref, fetch_ref, cast_ref, cset_ref, set_ref, next_ref, xs_hbm, wg_hbm, wu_hbm, wd_hbm, y_hbm,
                   wg_f, wu_f, wd_f, wg_b, wu_b, wd_b, xbuf, ybuf, in_sem, out_sem, w_sem, *, n):
    i = pl.program_id(0)
    block_rows = MOE_BLOCK * ROW_SUBLANES
    buf = i % EXPERT_ROW_BUFFERS

    def weight_copies(e):
        return [pltpu.make_async_copy(src.at[e], dst, w_sem.at[j])
                for j, (src, dst) in enumerate(((wg_hbm, wg_f), (wu_hbm, wu_f), (wd_hbm, wd_f)))]

    def cast_staged(which):
        for c in weight_copies(0):
            c.wait()
        wg_b[which] = wg_f[...].astype(BF16)
        wu_b[which] = wu_f[...].astype(BF16)
        wd_b[which] = wd_f[...].astype(BF16)

    def block_of(ref, j):
        return ref.at[pl.ds(pl.multiple_of(j * block_rows, block_rows), block_rows)]

    def copy_in(j):
        b = j % EXPERT_ROW_BUFFERS
        return pltpu.make_async_copy(block_of(xs_hbm, j), xbuf.at[b], in_sem.at[b])

    def copy_out(j):
        b = j % EXPERT_ROW_BUFFERS
        return pltpu.make_async_copy(ybuf.at[b], block_of(y_hbm, j), out_sem.at[b])

    @pl.when(i == 0)
    def _():
        for j in range(min(EXPERT_ROW_BUFFERS - 1, n)):
            copy_in(j).start()

    ahead = i + EXPERT_ROW_BUFFERS - 1
    pl.when(ahead < n)(lambda: copy_in(ahead).start())

    @pl.when(i == 0)
    def _():
        for c in weight_copies(bexp_ref[0]):
            c.start()
        cast_staged(0)

    pl.when(cast_ref[i] == 1)(lambda: cast_staged(cset_ref[i]))

    @pl.when(fetch_ref[i] == 1)
    def _():
        for c in weight_copies(next_ref[i]):
            c.start()

    which = set_ref[i]
    copy_in(i).wait()
    behind = i - (EXPERT_ROW_BUFFERS - 1)
    pl.when(behind >= 0)(lambda: copy_out(behind).wait())

    @pl.when(bval_ref[i] > 0)
    def _():
        xb = _load_rows(xbuf.at[buf], MOE_BLOCK).astype(BF16)
        gate = jnp.dot(xb, wg_b[which], preferred_element_type=F32)
        up = jnp.dot(xb, wu_b[which], preferred_element_type=F32)
        hid = (gate * jax.nn.sigmoid(gate) * up).astype(BF16)
        _store_rows(ybuf.at[buf], jnp.dot(hid, wd_b[which], preferred_element_type=F32), MOE_BLOCK)

    @pl.when(bval_ref[i] == 0)
    def _():
        ybuf[buf] = jnp.zeros((block_rows, LANES), F32)

    copy_out(i).start()

    @pl.when(i == n - 1)
    def _():
        for back in range(EXPERT_ROW_BUFFERS - 1):
            pl.when(i - back >= 0)(lambda back=back: copy_out(i - back).wait())


WEIGHT_CAST_LAG = 3


def _weight_schedule(block_expert):
    n = block_expert.shape[0]
    idx = jnp.arange(n, dtype=jnp.int32)
    run_start = jnp.concatenate([jnp.ones((1,), bool), block_expert[1:] != block_expert[:-1]])
    run_id = jnp.cumsum(run_start.astype(jnp.int32)) - 1
    run_first = jnp.min(jnp.where(run_id[None, :] == idx[:, None], idx[None, :], n), axis=1)
    next_first = jnp.concatenate([run_first[1:], jnp.full((1,), n, jnp.int32)])
    has_next = next_first < n
    cast_at = jnp.where(has_next, jnp.minimum(run_first + WEIGHT_CAST_LAG, next_first), -1)
    hit = idx[:, None] == cast_at[None, :]
    cast = jnp.any(hit, axis=1).astype(jnp.int32)
    cast_set = jnp.sum(jnp.where(hit, (idx[None, :] + 1) % 2, 0), axis=1).astype(jnp.int32)
    fetch = (run_start & has_next[run_id]).astype(jnp.int32)
    next_expert = block_expert[jnp.minimum(next_first[run_id], n - 1)]
    return fetch, cast, cast_set, run_id % 2, next_expert


def _routed_experts(xs, block_expert, block_valid, wg, wu, wd):
    n_blocks = block_expert.shape[0]
    shape = (EXPERT_ROW_BUFFERS, MOE_BLOCK * ROW_SUBLANES, LANES)
    any_spec = pl.BlockSpec(memory_space=pl.ANY)
    grid_spec = pltpu.PrefetchScalarGridSpec(
        num_scalar_prefetch=7,
        grid=(n_blocks,),
        in_specs=[any_spec, any_spec, any_spec, any_spec],
        out_specs=any_spec,
        scratch_shapes=[pltpu.VMEM((D_MODEL, D_EXPERT), F32), pltpu.VMEM((D_MODEL, D_EXPERT), F32),
                        pltpu.VMEM((D_EXPERT, D_MODEL), F32),
                        pltpu.VMEM((2, D_MODEL, D_EXPERT), BF16), pltpu.VMEM((2, D_MODEL, D_EXPERT), BF16),
                        pltpu.VMEM((2, D_EXPERT, D_MODEL), BF16),
                        pltpu.VMEM(shape, F32), pltpu.VMEM(shape, F32),
                        pltpu.SemaphoreType.DMA((EXPERT_ROW_BUFFERS,)),
                        pltpu.SemaphoreType.DMA((EXPERT_ROW_BUFFERS,)),
                        pltpu.SemaphoreType.DMA((3,))],
    )
    return pl.pallas_call(
        functools.partial(_expert_kernel, n=n_blocks),
        grid_spec=grid_spec,
        out_shape=jax.ShapeDtypeStruct(xs.shape, F32),
        name="routed_experts",
        compiler_params=_params("arbitrary"),
    )(block_expert, block_valid, *_weight_schedule(block_expert), xs, wg, wu, wd)


def _combine_kernel(dest_ref, dest_next_ref, x1_ref, route_ref, g_ref, yb_hbm, o_ref, ybuf, sem, *, n):
    i = pl.program_id(0)
    slot = i % 2

    def row_copy(idx8, r, k, s):
        dst = ybuf.at[s, pl.ds((k * ROW_TILE + r) * ROW_SUBLANES, ROW_SUBLANES)]
        return pltpu.make_async_copy(_row_tile(yb_hbm, idx8), dst, sem.at[s])

    @pl.when(i == 0)
    def _():
        def body(r, c):
            idx = [dest_ref[0, k * ROW_TILE + r] for k in range(TOP_K)]
            for k in range(TOP_K):
                row_copy(idx[k], r, k, 0).start()
            return c
        lax.fori_loop(0, ROW_TILE, body, 0)

    @pl.when(i + 1 < n)
    def _():
        for r0 in range(0, ROW_TILE, INDEX_BATCH):
            idx = [[dest_next_ref[0, k * ROW_TILE + r] for r in range(r0, r0 + INDEX_BATCH)] for k in range(TOP_K)]
            for r in range(r0, r0 + INDEX_BATCH):
                for k in range(TOP_K):
                    row_copy(idx[k][r - r0], r, k, 1 - slot).start(priority=k % DMA_THREADS)

    _tile_wait(yb_hbm, ybuf.at[slot], sem.at[slot])
    route = route_ref[...]
    buf = ybuf.at[slot]
    y = (_load_rows(buf, ROW_TILE) * route[:, 2:3]
         + _load_rows(buf, ROW_TILE, offset=ROW_TILE * ROW_SUBLANES) * route[:, 3:4])
    x2 = x1_ref[...] + y
    ms = jnp.mean(x2 * x2, axis=-1, keepdims=True)
    o_ref[...] = x2 * lax.rsqrt(ms + EPS) * g_ref[...]


def _combine_and_norm(dest8, x1, route, norm_w, yb):
    t = x1.shape[0]
    n_tiles = t // ROW_TILE
    dest_spec = lambda f: pl.BlockSpec((None, 1, TILE_ROW_COPIES), lambda i: (f(i), 0, 0), memory_space=pltpu.SMEM)
    return pl.pallas_call(
        functools.partial(_combine_kernel, n=n_tiles),
        grid=(n_tiles,),
        in_specs=[dest_spec(lambda i: i), dest_spec(lambda i: jnp.minimum(i + 1, n_tiles - 1)),
                  pl.BlockSpec((ROW_TILE, D_MODEL), lambda i: (i, 0)),
                  pl.BlockSpec((ROW_TILE, LANES), lambda i: (i, 0)),
                  _const_spec((1, D_MODEL)),
                  pl.BlockSpec(memory_space=pl.ANY)],
        out_specs=pl.BlockSpec((ROW_TILE, D_MODEL), lambda i: (i, 0)),
        out_shape=jax.ShapeDtypeStruct((t, D_MODEL), F32),
        scratch_shapes=[pltpu.VMEM((2, TILE_ROW_COPIES * ROW_SUBLANES, LANES), F32),
                        pltpu.SemaphoreType.DMA((2,))],
        name="combine_and_norm",
        compiler_params=_params("arbitrary", disable_bounds_checks=True),
    )(dest8, dest8, x1, route, norm_w, yb)


def _dispatch_plan(route_t, counts):
    t = route_t.shape[1]
    counts = counts[0, 0:N_EXPERTS].astype(jnp.int32)
    padded = (counts + MOE_BLOCK - 1) // MOE_BLOCK * MOE_BLOCK
    pad_end = jnp.cumsum(padded)
    pad_start = pad_end - padded
    expert = route_t[0:TOP_K].astype(jnp.int32)
    rank = route_t[4:4 + TOP_K].astype(jnp.int32)
    start_of = sum(jnp.where(expert == e, pad_start[e], 0) for e in range(N_EXPERTS))
    dest8 = (start_of + rank) * ROW_SUBLANES
    dest8 = dest8.reshape(TOP_K, t // ROW_TILE, ROW_TILE).transpose(1, 0, 2).reshape(t // ROW_TILE, 1, TILE_ROW_COPIES)
    n_blocks = -(-(t * TOP_K + N_EXPERTS * (MOE_BLOCK - 1)) // MOE_BLOCK)
    block_start = jnp.arange(n_blocks, dtype=jnp.int32) * MOE_BLOCK
    block_expert = jnp.minimum(jnp.sum((pad_end[None, :] <= block_start[:, None]).astype(jnp.int32), axis=1),
                               N_EXPERTS - 1)
    block_valid = (block_start < pad_end[-1]).astype(jnp.int32)
    n_slots = n_blocks * MOE_BLOCK
    fill_start = jnp.concatenate([pad_start + counts, pad_end[-1:]])
    fill_n = jnp.concatenate([padded - counts, n_slots - pad_end[-1:]])
    return dest8, fill_start, fill_n, block_expert, block_valid, n_slots


def _rel_bias_band(rel_table, qc):
    rows = qc * CHUNK
    window = ATTN_REACH + rows
    n = window + rows - 1
    diag = rel_table[:, np.clip(ATTN_REACH + rows - 1 - np.arange(n), -MAX_REL, MAX_REL) + MAX_REL].astype(F32)
    shifted = jnp.tile(diag, (1, rows + 1))[:, :rows * (n + 1)].reshape(N_HEADS_A, rows, n + 1)[:, :, :window]
    bias = shifted[:, ::-1, :] * LOG2E
    i = np.arange(rows)[:, None]
    j = np.arange(window)[None, :]
    first = (i // CHUNK) * CHUNK
    return jnp.where((j >= first) & (j < first + BAND), bias, NEG_INF)


def _pad_lanes(v, width=LANES):
    return jnp.pad(v.astype(F32), (0, width - v.shape[0])).reshape(1, width)


def _mixers(x, kv_cache, conv_state, ssm_state, w):
    batch, l, _ = x.shape
    nc = l // CHUNK
    t = batch * l
    x2d = x.reshape(t, D_MODEL)
    proj, dt_raw = _in_projection(x2d, w["norm_mix"], w["w_proj"])
    proj3 = proj.reshape(batch, l, PROJ_WIDTH)
    k_new = lambda rows: proj3[:, rows, K0:K0 + ATTN_WIDTH].astype(F32)
    v_new = lambda rows: proj3[:, rows, V0:V0 + ATTN_WIDTH].astype(F32)
    new_part = lambda col: (proj, pl.BlockSpec((l, ATTN_WIDTH), lambda b, c: (b, col // ATTN_WIDTH)), l)
    if kv_cache is None:
        qc = ATTN_CHUNKS_PER_STEP if nc % ATTN_CHUNKS_PER_STEP == 0 else 1
        attn, = _band_attention(proj, Q0 // ATTN_WIDTH, [new_part(K0)], [new_part(V0)],
                                _rel_bias_band(w["rel_table"], qc), batch=batch, nc=nc, pad_rows=ATTN_REACH, qc=qc)
        keep = slice(l - min(ATTN_REACH, l), l)
        k_keep, v_keep = k_new(keep), v_new(keep)
    else:
        assert nc == 1
        wlen = kv_cache[0].shape[1]
        cache_k, cache_v = (u.reshape(batch, wlen, ATTN_WIDTH) for u in kv_cache)
        cache_part = lambda u: (u, pl.BlockSpec((None, wlen, ATTN_WIDTH), lambda b, c: (b, 0, 0)), wlen)
        attn, k_keep, v_keep = _band_attention(
            proj, Q0 // ATTN_WIDTH, [cache_part(cache_k), new_part(K0)], [cache_part(cache_v), new_part(V0)],
            _rel_bias_band(w["rel_table"], 1), batch=batch, nc=nc, pad_rows=BAND - (wlen + l), qc=1, roll_out=True)
    tail = proj3[:, l - (CONV_WIDTH - 1):, XBC0:XBC0 + CONV_DIM].astype(F32)
    if conv_state is None:
        conv0 = st0 = None
        conv_new = tail
    else:
        conv0 = jnp.pad(conv_state, ((0, 0), (CARRY_ROWS - (CONV_WIDTH - 1), 0), (0, 0)))
        st0 = jnp.transpose(ssm_state.reshape(batch, D_INNER, D_STATE), (0, 2, 1))
        conv_new = jnp.concatenate([conv_state, tail], axis=1)[:, -(CONV_WIDTH - 1):]
    ssm, st_t = _ssd_mixer(proj, dt_raw, conv0, st0, w["conv_w"], w["conv_b"], w["dt_bias"], w["a_log"], w["d_exp"],
                           w["ssm_norm"], batch=batch, nc=nc)
    ssm_new = jnp.transpose(st_t, (0, 2, 1)).reshape(batch, SSM_HEADS, SSM_HEAD_DIM, D_STATE)
    heads = lambda u: u.reshape(batch, u.shape[1], N_HEADS_A, HEAD_DIM_A)
    return (x2d, attn, ssm, proj), (heads(k_keep), heads(v_keep), conv_new, ssm_new)


def kernel(x_prompt, x_sample, cache_attn_k, cache_attn_v, state_conv, state_ssm, norm_mix, w_in, rel_bias, conv_w, conv_b, dt_bias, a_log, d_skip, ssm_norm, w_branch_a, w_branch_b, w_out, norm_ffn, w_router_group, b_router_group, w_router_expert, b_router_expert, w_gate, w_up, w_down, norm_final):
    assert norm_mix.shape[0] == 1, "single-layer trunk"
    q, k, v, z, xbc, dtc, ga, gb = jnp.split(w_in[0], np.cumsum(
        (ATTN_WIDTH, ATTN_WIDTH, ATTN_WIDTH, D_INNER, CONV_DIM, SSM_HEADS, D_MODEL, D_MODEL))[:-1].tolist(), axis=1)
    w_proj = jnp.concatenate([z, ga, gb, xbc, q, k, v, jnp.pad(dtc, ((0, 0), (0, LANES - SSM_HEADS)))],
                             axis=1).astype(BF16)
    w_router = jnp.pad(jnp.concatenate([w_router_group[0], w_router_expert[0]], axis=1),
                       ((0, 0), (0, LANES - N_EXPERT_GROUPS - N_EXPERTS)))
    w_router_hi = w_router.astype(BF16)
    w_router_lo = (w_router - w_router_hi.astype(F32)).astype(BF16)
    b_router = _pad_lanes(jnp.concatenate([b_router_group[0], b_router_expert[0]]))
    w = dict(
        norm_mix=norm_mix[0].reshape(1, D_MODEL), w_proj=w_proj, rel_table=rel_bias[0],
        conv_w=conv_w[0], conv_b=conv_b[0].reshape(1, CONV_DIM), dt_bias=_pad_lanes(dt_bias[0]),
        a_log=_pad_lanes(a_log[0]), d_exp=jnp.repeat(d_skip[0], SSM_HEAD_DIM).reshape(1, D_INNER),
        ssm_norm=ssm_norm[0].reshape(1, D_INNER),
    )
    mix_w = (w_branch_a[0].astype(BF16), w_branch_b[0].astype(BF16), w_out[0].astype(BF16),
             norm_ffn[0].reshape(1, D_MODEL), jnp.concatenate([w_router_hi, w_router_lo], axis=1), b_router)
    norm_out = norm_final.reshape(1, D_MODEL)

    groups = [_mixers(x_prompt, None, None, None, w),
              _mixers(x_sample, (cache_attn_k[0], cache_attn_v[0]), state_conv[0], state_ssm[0], w)]
    counts = jnp.zeros((1, LANES), F32)
    mixed = []
    routes_t = []
    for acts, _ in groups:
        x1, hn, route, route_t, counts = _mix_and_route(*acts, *mix_w, counts)
        mixed.append((x1, hn, route))
        routes_t.append(route_t)
    dest8, fill_start, fill_n, block_expert, block_valid, n_slots = _dispatch_plan(
        jnp.concatenate(routes_t, axis=1), counts)
    xs = _dispatch(dest8, fill_start, fill_n, [m[1] for m in mixed], n_slots)
    yb = _routed_experts(xs, block_expert, block_valid, w_gate[0], w_up[0], w_down[0])
    outs, first = [], 0
    for (x1, _, route), (acts, _) in zip(mixed, groups):
        n_tiles = x1.shape[0] // ROW_TILE
        y = _combine_and_norm(dest8[first:first + n_tiles], x1, route, norm_out, yb)
        first += n_tiles
        outs.append(y)
    (yp, ys), ((kp, vp, cp, sp), (ks, vs, cs, ss)) = outs, [g[1] for g in groups]
    yp = yp.reshape(x_prompt.shape)
    ys = ys.reshape(x_sample.shape)
    return (yp, ys, kp[None], vp[None], cp[None], sp[None], ks[None], vs[None], cs[None], ss[None])
```

```python
import functools
import math

import numpy as np
import jax
import jax.numpy as jnp
from jax import lax
from jax.experimental import pallas as pl
from jax.experimental.pallas import tpu as pltpu

F32 = jnp.float32
BF16 = jnp.bfloat16

D_MODEL = 1024
CHUNK = 64
LEFT_CHUNKS = 8
ATTN_REACH = LEFT_CHUNKS * CHUNK
BAND = ATTN_REACH + CHUNK
N_HEADS_A = 8
HEAD_DIM_A = 64
ATTN_WIDTH = N_HEADS_A * HEAD_DIM_A
MAX_REL = 256
ATTN_SCALE = 1.0 / math.sqrt(HEAD_DIM_A)
NEG_INF = -1e30
SSM_HEADS = 16
SSM_HEAD_DIM = 64
D_INNER = SSM_HEADS * SSM_HEAD_DIM
SSM_GROUPS = 2
GROUP_WIDTH = D_INNER // SSM_GROUPS
D_STATE = 128
CONV_WIDTH = 4
CONV_DIM = D_INNER + 2 * SSM_GROUPS * D_STATE
N_EXPERT_GROUPS = 4
EXPERTS_PER_GROUP = 8
N_EXPERTS = N_EXPERT_GROUPS * EXPERTS_PER_GROUP
TOP_K = 2
D_EXPERT = 512
MOE_BLOCK = 256
EPS = 1e-6

LANES = 128
SUBLANES = 8
Z0, GA0, GB0, XBC0, Q0, K0, V0 = 0, 1024, 2048, 3072, 4608, 5120, 5632
PROJ_WIDTH = V0 + ATTN_WIDTH
PROJ_TILE = 1024
ROW_TILE = 256
ATTN_CHUNKS_PER_STEP = 2
SSD_STREAMS = 4
VMEM_LIMIT = 56 * 1024 * 1024

_NT = (((1,), (1,)), ((), ()))


def _params(*sem, **kw):
    return pltpu.CompilerParams(dimension_semantics=sem, vmem_limit_bytes=VMEM_LIMIT, **kw)


def _const_spec(shape):
    nd = len(shape)
    return pl.BlockSpec(shape, lambda *_: (0,) * nd)


def _inproj_kernel(x_ref, g_ref, w_ref, o_ref, dt_ref):
    x = x_ref[...]
    ms = jnp.mean(x * x, axis=-1, keepdims=True)
    h = (x * lax.rsqrt(ms + EPS) * g_ref[...]).astype(BF16)
    for j in range(0, PROJ_WIDTH, PROJ_TILE):
        o_ref[:, j:j + PROJ_TILE] = jnp.dot(h, w_ref[:, j:j + PROJ_TILE], preferred_element_type=F32).astype(BF16)
    dt_ref[...] = jnp.dot(h, w_ref[:, PROJ_WIDTH:PROJ_WIDTH + LANES], preferred_element_type=F32)


PROJ_ROWS = 512


def _in_projection(x2d, norm_w, w_proj):
    t = x2d.shape[0]
    rows = PROJ_ROWS if t % PROJ_ROWS == 0 else ROW_TILE
    return pl.pallas_call(
        _inproj_kernel,
        grid=(t // rows,),
        in_specs=[pl.BlockSpec((rows, D_MODEL), lambda i: (i, 0)),
                  _const_spec((1, D_MODEL)),
                  _const_spec((D_MODEL, PROJ_WIDTH + LANES))],
        out_specs=[pl.BlockSpec((rows, PROJ_WIDTH), lambda i: (i, 0)),
                   pl.BlockSpec((rows, LANES), lambda i: (i, 0))],
        out_shape=[jax.ShapeDtypeStruct((t, PROJ_WIDTH), BF16), jax.ShapeDtypeStruct((t, LANES), F32)],
        name="in_projection",
        compiler_params=_params("parallel"),
    )(x2d, norm_w, w_proj)


LOG2E = math.log2(math.e)
ATTN_STAGES = 5


def _attn_kernel(q_ref, *refs, pad_rows, part_rows, qc, roll_out):
    n_parts = len(part_rows)
    k_parts, v_parts = refs[:n_parts], refs[n_parts:2 * n_parts]
    bias_ref, o_ref = refs[2 * n_parts:2 * n_parts + 2]
    rolled = refs[2 * n_parts + 2:-2]
    kpad, vpad = refs[-2:]
    rows = qc * CHUNK
    window = ATTN_REACH + rows
    c = pl.program_id(1)

    @pl.when(c == 0)
    def _():
        if pad_rows:
            kpad[0:pad_rows, :] = jnp.zeros((pad_rows, ATTN_WIDTH), BF16)
            vpad[0:pad_rows, :] = jnp.zeros((pad_rows, ATTN_WIDTH), BF16)
        first = pad_rows
        for k_ref, v_ref, n in zip(k_parts, v_parts, part_rows):
            kpad[first:first + n, :] = k_ref[...].astype(BF16)
            vpad[first:first + n, :] = v_ref[...].astype(BF16)
            first += n
        if roll_out:
            kept, new = part_rows[0] - part_rows[1], part_rows[1]
            for out_ref, parts in zip(rolled, (k_parts, v_parts)):
                out_ref[0:kept, :] = parts[0][new:new + kept, :]
                out_ref[kept:kept + new, :] = parts[1][...].astype(F32)

    start = pl.multiple_of(c * rows, rows)

    def body(mask_start):
        kw = kpad[pl.ds(start, window), :]
        vw = vpad[pl.ds(start, window), :]
        q = (q_ref[...].astype(F32) * (ATTN_SCALE * LOG2E)).astype(BF16)
        low = lax.broadcasted_iota(jnp.int32, (rows, LANES), 1) < HEAD_DIM_A
        if mask_start:
            valid = lax.broadcasted_iota(jnp.int32, (rows, window), 1) + start >= pad_rows
        outs = {}

        def head(h):
            hp, sub = divmod(h, 2)
            sl = slice(LANES * hp, LANES * (hp + 1))
            q2 = q[:, sl]
            qm = jnp.where(low if sub == 0 else jnp.logical_not(low), q2, jnp.zeros_like(q2))
            s = lax.dot_general(qm, kw[:, sl], _NT, preferred_element_type=F32) + bias_ref[h]
            if mask_start:
                s = jnp.where(valid, s, NEG_INF)
            yield
            m = jnp.max(s, axis=-1, keepdims=True)
            yield
            p = jnp.exp2(s - m)
            yield
            l = jnp.sum(p, axis=-1, keepdims=True)
            pb = p.astype(BF16)
            yield
            outs[h] = jnp.dot(pb, vw[:, sl], preferred_element_type=F32) / l
            if sub == 1:
                o_ref[:, sl] = jnp.where(low, outs[h - 1], outs[h]).astype(o_ref.dtype)

        heads = [head(h) for h in range(N_HEADS_A)]
        for t in range(N_HEADS_A + ATTN_STAGES - 1):
            for h in range(min(t, N_HEADS_A - 1), max(t - ATTN_STAGES, -1), -1):
                next(heads[h], None)

    if pad_rows:
        pl.when(start < pad_rows)(lambda: body(True))
        pl.when(start >= pad_rows)(lambda: body(False))
    else:
        body(False)


def _band_attention(q_arr, q_col, k_parts, v_parts, bias, *, batch, nc, pad_rows, qc, roll_out=False):
    steps = nc // qc
    rows = qc * CHUNK
    part_rows = tuple(p[2] for p in k_parts)
    lk = sum(part_rows)
    kern = functools.partial(_attn_kernel, pad_rows=pad_rows, part_rows=part_rows, qc=qc, roll_out=roll_out)
    out_specs = [pl.BlockSpec((rows, ATTN_WIDTH), lambda b, c: (b * steps + c, 0))]
    out_shape = [jax.ShapeDtypeStruct((batch * nc * CHUNK, ATTN_WIDTH), BF16)]
    if roll_out:
        assert len(part_rows) == 2 and nc == 1
        out_specs += [pl.BlockSpec((None, part_rows[0], ATTN_WIDTH), lambda b, c: (b, 0, 0))] * 2
        out_shape += [jax.ShapeDtypeStruct((batch, part_rows[0], ATTN_WIDTH), F32)] * 2
    return pl.pallas_call(
        kern,
        grid=(batch, steps),
        in_specs=[pl.BlockSpec((rows, ATTN_WIDTH), lambda b, c: (b * steps + c, q_col))]
        + [p[1] for p in k_parts] + [p[1] for p in v_parts]
        + [_const_spec((N_HEADS_A, rows, ATTN_REACH + rows))],
        out_specs=out_specs,
        out_shape=out_shape,
        scratch_shapes=[pltpu.VMEM((pad_rows + lk, ATTN_WIDTH), BF16),
                        pltpu.VMEM((pad_rows + lk, ATTN_WIDTH), BF16)],
        name="band_attention",
        compiler_params=_params("parallel", "arbitrary"),
    )(q_arr, *[p[0] for p in k_parts], *[p[0] for p in v_parts], bias)


CARRY_ROWS = SUBLANES
SHIFT_TAPS = CONV_WIDTH - 1
SHIFT_K = 2 * CARRY_ROWS + CHUNK


def _split_bf16(x, parts):
    out = []
    for _ in range(parts - 1):
        h = x.astype(BF16)
        out.append(h)
        x = x - h.astype(F32)
    out.append(x.astype(BF16))
    return out


def _ssd_kernel(xbc_ref, z_ref, dt_ref, conv0_ref, st0_ref, cw_ref, cb_ref, dtb_ref, alog_ref,
                dexp_ref, nw_ref, tri_ref, exp_ref, sel_ref, eye_ref, shift_ref,
                y_ref, stout_ref, carry, st, *, has_init, streams, nc):
    c = pl.program_id(1)

    @pl.when(c == 0)
    def _():
        if has_init:
            carry[...] = conv0_ref[...]
            st[...] = st0_ref[...]
        else:
            carry[...] = jnp.zeros((streams, CARRY_ROWS, CONV_DIM), F32)
            st[...] = jnp.zeros((streams, D_STATE, D_INNER), F32)

    chunks = [_ssd_chunk(xbc_ref.at[s], z_ref.at[s], dt_ref.at[s], cw_ref, cb_ref, dtb_ref, alog_ref, dexp_ref,
                         nw_ref, tri_ref, exp_ref, sel_ref, eye_ref, shift_ref, y_ref.at[s], carry.at[s], st.at[s])
              for s in range(streams)]
    while chunks:
        chunks = [g for g in chunks if next(g, True) is None]

    @pl.when(c == nc - 1)
    def _():
        stout_ref[...] = st[...]


def _ssd_chunk(xbc_ref, z_ref, dt_ref, cw_ref, cb_ref, dtb_ref, alog_ref, dexp_ref, nw_ref,
               tri_ref, exp_ref, sel_ref, eye_ref, shift_ref, y_ref, carry, st):
    prev = carry[...]
    prev_hi = prev.astype(BF16).astype(F32)
    prev_parts = jnp.concatenate([prev_hi, prev - prev_hi], axis=0).astype(BF16)
    taps = jnp.dot(shift_ref[...], jnp.concatenate([prev_parts, xbc_ref[...]], axis=0), preferred_element_type=F32)

    dt_in = dt_ref[...] + dtb_ref[...]
    dt = jnp.maximum(dt_in, 0.0) + jnp.log1p(jnp.exp(-jnp.abs(dt_in)))
    da = dt * (-jnp.exp(alog_ref[...]))
    tri = tri_ref[...]
    cs3 = jnp.dot(tri, jnp.concatenate(_split_bf16(da, 3), axis=1), preferred_element_type=F32)
    yield

    cs = cs3[:, 0:LANES] + cs3[:, LANES:2 * LANES] + cs3[:, 2 * LANES:3 * LANES]
    cs_last = cs[CHUNK - 1:CHUNK, :]
    ecs = jnp.exp(cs)
    to_end = jnp.exp(cs_last - cs)

    stack = jnp.concatenate(_split_bf16(dt, 1) + _split_bf16(to_end, 1) + _split_bf16(ecs, 2)
                            + _split_bf16(cs, 3), axis=0)
    wide = jnp.dot(stack, exp_ref[...], preferred_element_type=F32)

    lane = lax.broadcasted_iota(jnp.int32, (CHUNK, LANES), 1)
    even = (lane % 2) == 0
    csm = jnp.concatenate([jnp.where(even, cs, 0.0), jnp.where(even, 0.0, cs)], axis=0)
    srow3 = lax.dot_general(sel_ref[...], jnp.concatenate(_split_bf16(csm, 3), axis=0), _NT,
                            preferred_element_type=F32)
    yield

    rows = [wide[i * CHUNK:(i + 1) * CHUNK, :] for i in range(7)]
    dt_w = rows[0]
    to_end_w = rows[1]
    ecs_w = rows[2] + rows[3]
    cs_w = rows[4] + rows[5] + rows[6]
    srow = srow3[:, 0:LANES] + srow3[:, LANES:2 * LANES] + srow3[:, 2 * LANES:3 * LANES]

    xdt_b, xw_b, skip, bc = [], [], [], []
    for j in range(CONV_DIM // LANES):
        sl = slice(j * LANES, (j + 1) * LANES)
        conv = cb_ref[:, sl] + xbc_ref[:, sl].astype(F32) * cw_ref[CONV_WIDTH - 1:CONV_WIDTH, sl]
        for tap in range(SHIFT_TAPS):
            conv = conv + taps[tap * CHUNK:(tap + 1) * CHUNK, sl] * cw_ref[tap:tap + 1, sl]
        u = conv * jax.nn.sigmoid(conv)
        if j < D_INNER // LANES:
            xdt = u * dt_w[:, sl]
            xdt_b.append(xdt.astype(BF16))
            xw_b.append((xdt * to_end_w[:, sl]).astype(BF16))
            skip.append(u * dexp_ref[:, sl])
        else:
            bc.append(u.astype(BF16))
    low = lane < SSM_HEAD_DIM
    t_idx = lax.broadcasted_iota(jnp.int32, (CHUNK, LANES), 0)
    causal = t_idx >= (lane % SSM_HEAD_DIM)
    eye = eye_ref[...]
    pairs = SSM_HEADS // SSM_GROUPS // 2
    yield

    for g in range(SSM_GROUPS):
        b_g, c_g = bc[g], bc[SSM_GROUPS + g]
        gsl = slice(g * GROUP_WIDTH, (g + 1) * GROUP_WIDTH)
        st_g = st[:, gsl]
        y_off = jnp.dot(c_g, st_g.astype(BF16), preferred_element_type=F32)
        b_t = lax.dot_general(eye, b_g, _NT, preferred_element_type=F32).astype(BF16)
        xw_g = jnp.concatenate(xw_b[g * pairs:(g + 1) * pairs], axis=1)
        st[:, gsl] = st_g * ecs_w[CHUNK - 1:CHUNK, gsl] + jnp.dot(b_t, xw_g, preferred_element_type=F32)
        cb2 = lax.dot_general(c_g, jnp.concatenate([b_g, b_g], axis=0), _NT, preferred_element_type=F32)
        yield
        gated, sumsq = [], 0.0
        for kk in range(pairs):
            k = g * pairs + kk
            psl = slice(k * LANES, (k + 1) * LANES)
            decay = jnp.where(causal, jnp.exp(cs_w[:, psl] - srow[k:k + 1, :]), 0.0)
            gmat = (cb2 * decay).astype(BF16)
            xp = xdt_b[k]
            xblk = jnp.concatenate([jnp.where(low, xp, jnp.zeros_like(xp)),
                                    jnp.where(low, jnp.zeros_like(xp), xp)], axis=0)
            y = (jnp.dot(gmat, xblk, preferred_element_type=F32)
                 + y_off[:, kk * LANES:(kk + 1) * LANES] * ecs_w[:, psl] + skip[k])
            zz = z_ref[:, psl].astype(F32)
            y = y * (zz * jax.nn.sigmoid(zz))
            gated.append(y)
            sumsq = sumsq + jnp.sum(y * y, axis=-1, keepdims=True)
        scale = lax.rsqrt(sumsq * (1.0 / GROUP_WIDTH) + EPS)
        for kk in range(pairs):
            psl = slice((g * pairs + kk) * LANES, (g * pairs + kk + 1) * LANES)
            y_ref[:, psl] = (gated[kk] * scale * nw_ref[:, psl]).astype(y_ref.dtype)
        yield
    carry[...] = xbc_ref[CHUNK - 2 * CARRY_ROWS:CHUNK, :].astype(F32)[CARRY_ROWS:, :]


def _ssd_constants():
    tri = np.tril(np.ones((CHUNK, CHUNK), np.float32))
    expand = np.zeros((LANES, D_INNER), np.float32)
    for e in range(SSM_HEADS):
        expand[e, e * SSM_HEAD_DIM:(e + 1) * SSM_HEAD_DIM] = 1.0
    sel = np.zeros((16, LANES), np.float32)
    for e in range(SSM_HEADS):
        sel[e // 2, e] = 1.0
    eye = np.eye(D_STATE, dtype=np.float32)
    shift = np.zeros((SHIFT_TAPS * CHUNK, SHIFT_K), np.float32)
    for j in range(SHIFT_TAPS):
        for t in range(CHUNK):
            m = t - (CONV_WIDTH - 1) + j
            if m >= 0:
                shift[j * CHUNK + t, 2 * CARRY_ROWS + m] = 1.0
            else:
                shift[j * CHUNK + t, CARRY_ROWS + m] = 1.0
                shift[j * CHUNK + t, 2 * CARRY_ROWS + m] = 1.0
    return tuple(jnp.asarray(a, BF16) for a in (tri, expand, sel, eye, shift))


def _ssd_mixer(proj, dt_raw, conv0, st0, conv_w, conv_b, dt_bias_p, a_log_p, d_exp, norm_w, *, batch, nc):
    streams = SSD_STREAMS if batch % SSD_STREAMS == 0 else 1
    groups = batch // streams
    l = nc * CHUNK
    has_init = conv0 is not None
    if not has_init:
        conv0 = jnp.zeros((1, streams, CARRY_ROWS, CONV_DIM), F32)
        st0 = jnp.zeros((1, streams, D_STATE, D_INNER), F32)
        init_map = lambda b, c: (0, 0, 0, 0)
    else:
        conv0 = conv0.reshape(groups, streams, CARRY_ROWS, CONV_DIM)
        st0 = st0.reshape(groups, streams, D_STATE, D_INNER)
        init_map = lambda b, c: (b, 0, 0, 0)
    tri, expand, sel, eye, shift = _ssd_constants()
    proj4 = proj.reshape(groups, streams, l, PROJ_WIDTH)
    dt4 = dt_raw.reshape(groups, streams, l, LANES)
    chunk_of = lambda w, col: pl.BlockSpec((None, streams, CHUNK, w), lambda b, c: (b, 0, c, col))
    state_spec = lambda m: pl.BlockSpec((None, streams, D_STATE, D_INNER), m)
    kern = functools.partial(_ssd_kernel, has_init=has_init, streams=streams, nc=nc)
    y, st_out = pl.pallas_call(
        kern,
        grid=(groups, nc),
        in_specs=[chunk_of(CONV_DIM, XBC0 // CONV_DIM), chunk_of(D_INNER, Z0 // D_INNER), chunk_of(LANES, 0),
                  pl.BlockSpec((None, streams, CARRY_ROWS, CONV_DIM), init_map), state_spec(init_map),
                  _const_spec((CONV_WIDTH, CONV_DIM)), _const_spec((1, CONV_DIM)),
                  _const_spec((1, LANES)), _const_spec((1, LANES)),
                  _const_spec((1, D_INNER)), _const_spec((1, D_INNER)),
                  _const_spec((CHUNK, CHUNK)), _const_spec((LANES, D_INNER)),
                  _const_spec((16, LANES)), _const_spec((D_STATE, D_STATE)),
                  _const_spec((SHIFT_TAPS * CHUNK, SHIFT_K))],
        out_specs=[chunk_of(D_INNER, 0), state_spec(lambda b, c: (b, 0, 0, 0))],
        out_shape=[jax.ShapeDtypeStruct((groups, streams, l, D_INNER), BF16),
                   jax.ShapeDtypeStruct((groups, streams, D_STATE, D_INNER), F32)],
        scratch_shapes=[pltpu.VMEM((streams, CARRY_ROWS, CONV_DIM), F32),
                        pltpu.VMEM((streams, D_STATE, D_INNER), F32)],
        name="ssd_mixer",
        compiler_params=_params("parallel", "arbitrary"),
    )(proj4, proj4, dt4, conv0, st0, conv_w, conv_b, dt_bias_p, a_log_p, d_exp, norm_w, tri, expand, sel, eye, shift)
    return y.reshape(batch * l, D_INNER), st_out.reshape(batch, D_STATE, D_INNER)


ROW_SUBLANES = D_MODEL // LANES
assert ROW_SUBLANES == SUBLANES


def _store_rows(ref, x, rows, offset=0):
    for s in range(ROW_SUBLANES):
        ref[pl.ds(offset + s, rows, stride=ROW_SUBLANES), :] = x[:, s * LANES:(s + 1) * LANES]


def _load_rows(ref, rows, offset=0):
    return jnp.concatenate([ref[pl.ds(offset + s, rows, stride=ROW_SUBLANES), :] for s in range(ROW_SUBLANES)],
                           axis=1)


def _row_tile(ref, idx8):
    return ref.at[pl.ds(pl.multiple_of(idx8, ROW_SUBLANES), ROW_SUBLANES)]


def _mix_kernel(x_ref, attn_ref, ssm_ref, ga_ref, gb_ref, wa_ref, wb_ref, wo_ref, g_ref, wr_ref, br_ref, slt_ref,
                count0_ref, x1_ref, hn_ref, route_ref, route_t_ref, counts_ref, carry, *, subtiles):
    @pl.when(pl.program_id(0) == 0)
    def _():
        carry[...] = count0_ref[...]

    running = {"counts": carry[...]}
    tiles = [_mix_tile(j * ROW_TILE, running, x_ref, attn_ref, ssm_ref, ga_ref, gb_ref, wa_ref, wb_ref, wo_ref, g_ref,
                       wr_ref, br_ref, slt_ref, x1_ref, hn_ref, route_ref, route_t_ref) for j in range(subtiles)]
    while tiles:
        tiles = [g for g in tiles if next(g, True) is None]
    carry[...] = running["counts"]
    counts_ref[...] = running["counts"]


def _mix_tile(r0, running, x_ref, attn_ref, ssm_ref, ga_ref, gb_ref, wa_ref, wb_ref, wo_ref, g_ref, wr_ref, br_ref,
              slt_ref, x1_ref, hn_ref, route_ref, route_t_ref):
    rows = slice(r0, r0 + ROW_TILE)
    a = jnp.dot(attn_ref[rows, :], wa_ref[...], preferred_element_type=F32)
    s = jnp.dot(ssm_ref[rows, :], wb_ref[...], preferred_element_type=F32)
    yield
    mixed = jax.nn.sigmoid(ga_ref[rows, :].astype(F32)) * a + jax.nn.sigmoid(gb_ref[rows, :].astype(F32)) * s
    x1 = x_ref[rows, :] + jnp.dot(mixed.astype(BF16), wo_ref[...], preferred_element_type=F32)
    yield
    x1_ref[rows, :] = x1
    ms = jnp.mean(x1 * x1, axis=-1, keepdims=True)
    hn = x1 * lax.rsqrt(ms + EPS) * g_ref[...]
    _store_rows(hn_ref, hn, ROW_TILE, offset=r0 * ROW_SUBLANES)

    hi = hn.astype(BF16)
    lo = (hn - hi.astype(F32)).astype(BF16)
    both = jnp.dot(hi, wr_ref[...], preferred_element_type=F32)
    lo_part = jnp.dot(lo, wr_ref[:, 0:LANES], preferred_element_type=F32)
    yield
    logits = both[:, 0:LANES] + both[:, LANES:2 * LANES] + lo_part + br_ref[...]
    lane = lax.broadcasted_iota(jnp.int32, logits.shape, 1)
    lane_f = lane.astype(F32)
    big = float(LANES)
    is_g = lane < N_EXPERT_GROUPS
    gl = jnp.where(is_g, logits, -jnp.inf)
    gmax = jnp.max(gl, axis=-1, keepdims=True)
    gidx = jnp.min(jnp.where(gl == gmax, lane_f, big), axis=-1, keepdims=True)
    gsum = jnp.sum(jnp.where(is_g, jnp.exp(gl - gmax), 0.0), axis=-1, keepdims=True)
    g_top = 1.0 / gsum
    first = N_EXPERT_GROUPS + gidx * EXPERTS_PER_GROUP
    in_group = (lane_f >= first) & (lane_f < first + EXPERTS_PER_GROUP)
    el = jnp.where(in_group, logits, -jnp.inf)
    m1 = jnp.max(el, axis=-1, keepdims=True)
    i1 = jnp.min(jnp.where(el == m1, lane_f, big), axis=-1, keepdims=True)
    el2 = jnp.where(lane_f == i1, -jnp.inf, el)
    m2 = jnp.max(el2, axis=-1, keepdims=True)
    i2 = jnp.min(jnp.where(el2 == m2, lane_f, big), axis=-1, keepdims=True)
    r = jnp.exp(m2 - m1)
    w1 = g_top / (1.0 + r)
    w2 = g_top * r / (1.0 + r)
    e1 = i1 - N_EXPERT_GROUPS
    e2 = i2 - N_EXPERT_GROUPS

    oh1 = lane_f == e1
    oh2 = lane_f == e2
    hot1 = jnp.where(oh1, 1.0, 0.0)
    hot2 = jnp.where(oh2, 1.0, 0.0)
    onehots = jnp.concatenate([hot1, hot2], axis=1).astype(BF16)
    before = jnp.dot(slt_ref[...], onehots, preferred_element_type=F32)
    cnt1 = jnp.sum(hot1, axis=0, keepdims=True)
    cnt2 = jnp.sum(hot2, axis=0, keepdims=True)
    yield
    base = running["counts"]
    rank1 = jnp.sum(jnp.where(oh1, before[:, 0:LANES] + base, 0.0), axis=-1, keepdims=True)
    rank2 = jnp.sum(jnp.where(oh2, before[:, LANES:2 * LANES] + (base + cnt1), 0.0), axis=-1, keepdims=True)
    running["counts"] = base + cnt1 + cnt2

    route = jnp.zeros_like(logits)
    for col, val in enumerate((e1, e2, w1, w2, rank1, rank2)):
        route = jnp.where(lane == col, val, route)
    route_ref[rows, :] = route
    route_t_ref[:, rows] = jnp.transpose(route)[0:SUBLANES, :]


MIX_SUBTILES = 2


def _mix_and_route(x2d, attn, ssm, proj, wa, wb, wo, norm_w, w_router, b_router, count0):
    t = x2d.shape[0]
    subtiles = MIX_SUBTILES if t % (MIX_SUBTILES * ROW_TILE) == 0 else 1
    step_rows = subtiles * ROW_TILE
    tile = lambda w, col: pl.BlockSpec((step_rows, w), lambda i: (i, col))
    slt = jnp.asarray(np.tril(np.ones((ROW_TILE, ROW_TILE), np.float32), k=-1), BF16)
    return pl.pallas_call(
        functools.partial(_mix_kernel, subtiles=subtiles),
        grid=(t // step_rows,),
        in_specs=[tile(D_MODEL, 0), tile(ATTN_WIDTH, 0), tile(D_INNER, 0),
                  tile(D_MODEL, GA0 // D_MODEL), tile(D_MODEL, GB0 // D_MODEL),
                  _const_spec((ATTN_WIDTH, D_MODEL)), _const_spec((D_INNER, D_MODEL)),
                  _const_spec((D_MODEL, D_MODEL)), _const_spec((1, D_MODEL)),
                  _const_spec((D_MODEL, 2 * LANES)), _const_spec((1, LANES)),
                  _const_spec((ROW_TILE, ROW_TILE)), _const_spec((1, LANES))],
        out_specs=[tile(D_MODEL, 0), pl.BlockSpec((step_rows * ROW_SUBLANES, LANES), lambda i: (i, 0)),
                   tile(LANES, 0), pl.BlockSpec((SUBLANES, step_rows), lambda i: (0, i)), _const_spec((1, LANES))],
        out_shape=[jax.ShapeDtypeStruct((t, D_MODEL), F32),
                   jax.ShapeDtypeStruct((t * ROW_SUBLANES, LANES), F32),
                   jax.ShapeDtypeStruct((t, LANES), F32),
                   jax.ShapeDtypeStruct((SUBLANES, t), F32),
                   jax.ShapeDtypeStruct((1, LANES), F32)],
        scratch_shapes=[pltpu.VMEM((1, LANES), F32)],
        name="mix_and_route",
        compiler_params=_params("arbitrary"),
    )(x2d, attn, ssm, proj, proj, wa, wb, wo, norm_w, w_router, b_router, slt, count0)


TILE_ROW_COPIES = ROW_TILE * TOP_K
INDEX_BATCH = 8
DISPATCH_BUFFERS = 3
DMA_THREADS = 2


def _tile_wait(ref_a, ref_b, sem):
    n = TILE_ROW_COPIES * ROW_SUBLANES
    pltpu.make_async_copy(ref_a.at[pl.ds(0, n)], ref_b.at[pl.ds(0, n)], sem).wait()


def _dispatch_kernel(fill_start_ref, fill_n_ref, dest_ref, *refs, tiles_per_source, fill_rows):
    n_src = len(tiles_per_source)
    sources = refs[:n_src]
    xs_hbm, hbuf, zbuf, in_sem, out_sem, fill_sem = refs[n_src:]
    i = pl.program_id(0)
    n = pl.num_programs(0)
    tile_rows = ROW_TILE * ROW_SUBLANES

    def tile_copy(src, j, b):
        return pltpu.make_async_copy(src.at[pl.ds(pl.multiple_of(j * tile_rows, tile_rows), tile_rows)],
                                     hbuf.at[b], in_sem.at[b])

    def tile_in_start(j):
        b = j % DISPATCH_BUFFERS
        first = 0
        for src, count in zip(sources, tiles_per_source):
            pl.when((j >= first) & (j < first + count))(lambda src=src, first=first: tile_copy(src, j - first, b).start())
            first += count

    @pl.when(i == 0)
    def _():
        zbuf[...] = jnp.zeros((ROW_SUBLANES, LANES), F32)
        tile_in_start(i)
        pl.when(n > 1)(lambda: tile_in_start(i + 1))

    buf = i % DISPATCH_BUFFERS
    tile_copy(sources[0], 0, buf).wait()
    for r0 in range(0, ROW_TILE, INDEX_BATCH):
        dests = [[dest_ref[0, k * ROW_TILE + r] for r in range(r0, r0 + INDEX_BATCH)] for k in range(TOP_K)]
        for r in range(r0, r0 + INDEX_BATCH):
            src = hbuf.at[buf, pl.ds(r * ROW_SUBLANES, ROW_SUBLANES)]
            for k in range(TOP_K):
                dst = _row_tile(xs_hbm, dests[k][r - r0])
                pltpu.make_async_copy(src, dst, out_sem.at[buf]).start(priority=k % DMA_THREADS)

    @pl.when(i > 0)
    def _():
        _tile_wait(xs_hbm, xs_hbm, out_sem.at[(i - 1) % DISPATCH_BUFFERS])

    pl.when(i + 2 < n)(lambda: tile_in_start(i + 2))

    @pl.when(i == n - 1)
    def _():
        _tile_wait(xs_hbm, xs_hbm, out_sem.at[buf])

        def per_range(e, carry):
            first = fill_start_ref[e]

            def start(j, c):
                pltpu.make_async_copy(zbuf, _row_tile(xs_hbm, (first + j) * ROW_SUBLANES), fill_sem.at[0]).start()
                return c

            lax.fori_loop(0, fill_n_ref[e], start, 0)
            return carry

        lax.fori_loop(0, N_EXPERTS + 1, per_range, 0)
        filled = pl.ds(0, fill_rows * ROW_SUBLANES)
        pltpu.make_async_copy(xs_hbm.at[filled], xs_hbm.at[filled], fill_sem.at[0]).wait()


def _dispatch(dest8, fill_start, fill_n, hn_list, n_slots):
    n_tiles = dest8.shape[0]
    tiles_per_source = tuple(h.shape[0] // (ROW_TILE * ROW_SUBLANES) for h in hn_list)
    assert sum(tiles_per_source) == n_tiles
    grid_spec = pltpu.PrefetchScalarGridSpec(
        num_scalar_prefetch=2,
        grid=(n_tiles,),
        in_specs=[pl.BlockSpec((None, 1, TILE_ROW_COPIES), lambda i, fs, fn: (i, 0, 0), memory_space=pltpu.SMEM)]
        + [pl.BlockSpec(memory_space=pl.ANY)] * len(hn_list),
        out_specs=pl.BlockSpec(memory_space=pl.ANY),
        scratch_shapes=[pltpu.VMEM((DISPATCH_BUFFERS, ROW_TILE * ROW_SUBLANES, LANES), F32),
                        pltpu.VMEM((ROW_SUBLANES, LANES), F32),
                        pltpu.SemaphoreType.DMA((DISPATCH_BUFFERS,)), pltpu.SemaphoreType.DMA((DISPATCH_BUFFERS,)),
                        pltpu.SemaphoreType.DMA((1,))],
    )
    return pl.pallas_call(
        functools.partial(_dispatch_kernel, tiles_per_source=tiles_per_source,
                          fill_rows=n_slots - n_tiles * TILE_ROW_COPIES),
        grid_spec=grid_spec,
        out_shape=jax.ShapeDtypeStruct((n_slots * ROW_SUBLANES, LANES), F32),
        name="dispatch_rows",
        compiler_params=_params("arbitrary", disable_bounds_checks=True),
    )(fill_start, fill_n, dest8, *hn_list)


EXPERT_ROW_BUFFERS = 3
EXPERT_SPLIT = 2


def _expert_kernel(bexp_ref, bval_ref, fetch_ref, cast_ref, cset_ref, set_ref, next_ref, xs_hbm, wg_hbm, wu_hbm, wd_hbm, y_hbm,
                   wg_f, wu_f, wd_f, wg_b, wu_b, wd_b, xbuf, ybuf, in_sem, out_sem, w_sem, *, n):
    i = pl.program_id(0)
    block_rows = MOE_BLOCK * ROW_SUBLANES
    buf = i % EXPERT_ROW_BUFFERS

    def weight_copies(e):
        return [pltpu.make_async_copy(src.at[e], dst, w_sem.at[j])
                for j, (src, dst) in enumerate(((wg_hbm, wg_f), (wu_hbm, wu_f), (wd_hbm, wd_f)))]

    def cast_staged(which):
        for c in weight_copies(0):
            c.wait()
        wg_b[which] = wg_f[...].astype(BF16)
        wu_b[which] = wu_f[...].astype(BF16)
        wd_b[which] = wd_f[...].astype(BF16)

    def block_of(ref, j):
        return ref.at[pl.ds(pl.multiple_of(j * block_rows, block_rows), block_rows)]

    def copy_in(j):
        b = j % EXPERT_ROW_BUFFERS
        return pltpu.make_async_copy(block_of(xs_hbm, j), xbuf.at[b], in_sem.at[b])

    def copy_out(j):
        b = j % EXPERT_ROW_BUFFERS
        return pltpu.make_async_copy(ybuf.at[b], block_of(y_hbm, j), out_sem.at[b])

    @pl.when(i == 0)
    def _():
        for j in range(min(EXPERT_ROW_BUFFERS - 1, n)):
            copy_in(j).start()

    ahead = i + EXPERT_ROW_BUFFERS - 1
    pl.when(ahead < n)(lambda: copy_in(ahead).start())

    @pl.when(i == 0)
    def _():
        for c in weight_copies(bexp_ref[0]):
            c.start()
        cast_staged(0)

    pl.when(cast_ref[i] == 1)(lambda: cast_staged(cset_ref[i]))

    @pl.when(fetch_ref[i] == 1)
    def _():
        for c in weight_copies(next_ref[i]):
            c.start()

    which = set_ref[i]
    copy_in(i).wait()
    behind = i - (EXPERT_ROW_BUFFERS - 1)
    pl.when(behind >= 0)(lambda: copy_out(behind).wait())

    @pl.when(bval_ref[i] > 0)
    def _():
        def half(j):
            rows = MOE_BLOCK // EXPERT_SPLIT
            off = j * rows * ROW_SUBLANES
            xb = _load_rows(xbuf.at[buf], rows, offset=off).astype(BF16)
            gate = jnp.dot(xb, wg_b[which], preferred_element_type=F32)
            up = jnp.dot(xb, wu_b[which], preferred_element_type=F32)
            yield
            hid = (gate * jax.nn.sigmoid(gate) * up).astype(BF16)
            out = jnp.dot(hid, wd_b[which], preferred_element_type=F32)
            yield
            _store_rows(ybuf.at[buf], out, rows, offset=off)

        parts = [half(j) for j in range(EXPERT_SPLIT)]
        while parts:
            parts = [g for g in parts if next(g, True) is None]

    @pl.when(bval_ref[i] == 0)
    def _():
        ybuf[buf] = jnp.zeros((block_rows, LANES), F32)

    copy_out(i).start()

    @pl.when(i == n - 1)
    def _():
        for back in range(EXPERT_ROW_BUFFERS - 1):
            pl.when(i - back >= 0)(lambda back=back: copy_out(i - back).wait())


WEIGHT_CAST_LAG = 3


def _weight_schedule(block_expert):
    n = block_expert.shape[0]
    idx = jnp.arange(n, dtype=jnp.int32)
    run_start = jnp.concatenate([jnp.ones((1,), bool), block_expert[1:] != block_expert[:-1]])
    run_id = jnp.cumsum(run_start.astype(jnp.int32)) - 1
    run_first = jnp.min(jnp.where(run_id[None, :] == idx[:, None], idx[None, :], n), axis=1)
    next_first = jnp.concatenate([run_first[1:], jnp.full((1,), n, jnp.int32)])
    has_next = next_first < n
    cast_at = jnp.where(has_next, jnp.minimum(run_first + WEIGHT_CAST_LAG, next_first), -1)
    hit = idx[:, None] == cast_at[None, :]
    cast = jnp.any(hit, axis=1).astype(jnp.int32)
    cast_set = jnp.sum(jnp.where(hit, (idx[None, :] + 1) % 2, 0), axis=1).astype(jnp.int32)
    fetch = (run_start & has_next[run_id]).astype(jnp.int32)
    next_expert = block_expert[jnp.minimum(next_first[run_id], n - 1)]
    return fetch, cast, cast_set, run_id % 2, next_expert


def _routed_experts(xs, block_expert, block_valid, wg, wu, wd):
    n_blocks = block_expert.shape[0]
    shape = (EXPERT_ROW_BUFFERS, MOE_BLOCK * ROW_SUBLANES, LANES)
    any_spec = pl.BlockSpec(memory_space=pl.ANY)
    grid_spec = pltpu.PrefetchScalarGridSpec(
        num_scalar_prefetch=7,
        grid=(n_blocks,),
        in_specs=[any_spec, any_spec, any_spec, any_spec],
        out_specs=any_spec,
        scratch_shapes=[pltpu.VMEM((D_MODEL, D_EXPERT), F32), pltpu.VMEM((D_MODEL, D_EXPERT), F32),
                        pltpu.VMEM((D_EXPERT, D_MODEL), F32),
                        pltpu.VMEM((2, D_MODEL, D_EXPERT), BF16), pltpu.VMEM((2, D_MODEL, D_EXPERT), BF16),
                        pltpu.VMEM((2, D_EXPERT, D_MODEL), BF16),
                        pltpu.VMEM(shape, F32), pltpu.VMEM(shape, F32),
                        pltpu.SemaphoreType.DMA((EXPERT_ROW_BUFFERS,)),
                        pltpu.SemaphoreType.DMA((EXPERT_ROW_BUFFERS,)),
                        pltpu.SemaphoreType.DMA((3,))],
    )
    return pl.pallas_call(
        functools.partial(_expert_kernel, n=n_blocks),
        grid_spec=grid_spec,
        out_shape=jax.ShapeDtypeStruct(xs.shape, F32),
        name="routed_experts",
        compiler_params=_params("arbitrary"),
    )(block_expert, block_valid, *_weight_schedule(block_expert), xs, wg, wu, wd)


def _combine_kernel(dest_ref, dest_next_ref, x1_ref, route_ref, g_ref, yb_hbm, o_ref, ybuf, sem, *, n):
    i = pl.program_id(0)
    slot = i % 2

    def row_copy(idx8, r, k, s):
        dst = ybuf.at[s, pl.ds((k * ROW_TILE + r) * ROW_SUBLANES, ROW_SUBLANES)]
        return pltpu.make_async_copy(_row_tile(yb_hbm, idx8), dst, sem.at[s])

    @pl.when(i == 0)
    def _():
        def body(r, c):
            idx = [dest_ref[0, k * ROW_TILE + r] for k in range(TOP_K)]
            for k in range(TOP_K):
                row_copy(idx[k], r, k, 0).start()
            return c
        lax.fori_loop(0, ROW_TILE, body, 0)

    @pl.when(i + 1 < n)
    def _():
        for r0 in range(0, ROW_TILE, INDEX_BATCH):
            idx = [[dest_next_ref[0, k * ROW_TILE + r] for r in range(r0, r0 + INDEX_BATCH)] for k in range(TOP_K)]
            for r in range(r0, r0 + INDEX_BATCH):
                for k in range(TOP_K):
                    row_copy(idx[k][r - r0], r, k, 1 - slot).start(priority=k % DMA_THREADS)

    _tile_wait(yb_hbm, ybuf.at[slot], sem.at[slot])
    route = route_ref[...]
    buf = ybuf.at[slot]
    y = (_load_rows(buf, ROW_TILE) * route[:, 2:3]
         + _load_rows(buf, ROW_TILE, offset=ROW_TILE * ROW_SUBLANES) * route[:, 3:4])
    x2 = x1_ref[...] + y
    ms = jnp.mean(x2 * x2, axis=-1, keepdims=True)
    o_ref[...] = x2 * lax.rsqrt(ms + EPS) * g_ref[...]


def _combine_and_norm(dest8, x1, route, norm_w, yb):
    t = x1.shape[0]
    n_tiles = t // ROW_TILE
    dest_spec = lambda f: pl.BlockSpec((None, 1, TILE_ROW_COPIES), lambda i: (f(i), 0, 0), memory_space=pltpu.SMEM)
    return pl.pallas_call(
        functools.partial(_combine_kernel, n=n_tiles),
        grid=(n_tiles,),
        in_specs=[dest_spec(lambda i: i), dest_spec(lambda i: jnp.minimum(i + 1, n_tiles - 1)),
                  pl.BlockSpec((ROW_TILE, D_MODEL), lambda i: (i, 0)),
                  pl.BlockSpec((ROW_TILE, LANES), lambda i: (i, 0)),
                  _const_spec((1, D_MODEL)),
                  pl.BlockSpec(memory_space=pl.ANY)],
        out_specs=pl.BlockSpec((ROW_TILE, D_MODEL), lambda i: (i, 0)),
        out_shape=jax.ShapeDtypeStruct((t, D_MODEL), F32),
        scratch_shapes=[pltpu.VMEM((2, TILE_ROW_COPIES * ROW_SUBLANES, LANES), F32),
                        pltpu.SemaphoreType.DMA((2,))],
        name="combine_and_norm",
        compiler_params=_params("arbitrary", disable_bounds_checks=True),
    )(dest8, dest8, x1, route, norm_w, yb)


def _dispatch_plan(route_t, counts):
    t = route_t.shape[1]
    counts = counts[0, 0:N_EXPERTS].astype(jnp.int32)
    padded = (counts + MOE_BLOCK - 1) // MOE_BLOCK * MOE_BLOCK
    pad_end = jnp.cumsum(padded)
    pad_start = pad_end - padded
    expert = route_t[0:TOP_K].astype(jnp.int32)
    rank = route_t[4:4 + TOP_K].astype(jnp.int32)
    start_of = sum(jnp.where(expert == e, pad_start[e], 0) for e in range(N_EXPERTS))
    dest8 = (start_of + rank) * ROW_SUBLANES
    dest8 = dest8.reshape(TOP_K, t // ROW_TILE, ROW_TILE).transpose(1, 0, 2).reshape(t // ROW_TILE, 1, TILE_ROW_COPIES)
    n_blocks = -(-(t * TOP_K + N_EXPERTS * (MOE_BLOCK - 1)) // MOE_BLOCK)
    block_start = jnp.arange(n_blocks, dtype=jnp.int32) * MOE_BLOCK
    block_expert = jnp.minimum(jnp.sum((pad_end[None, :] <= block_start[:, None]).astype(jnp.int32), axis=1),
                               N_EXPERTS - 1)
    block_valid = (block_start < pad_end[-1]).astype(jnp.int32)
    n_slots = n_blocks * MOE_BLOCK
    fill_start = jnp.concatenate([pad_start + counts, pad_end[-1:]])
    fill_n = jnp.concatenate([padded - counts, n_slots - pad_end[-1:]])
    return dest8, fill_start, fill_n, block_expert, block_valid, n_slots


def _rel_bias_band(rel_table, qc):
    rows = qc * CHUNK
    window = ATTN_REACH + rows
    n = window + rows - 1
    diag = rel_table[:, np.clip(ATTN_REACH + rows - 1 - np.arange(n), -MAX_REL, MAX_REL) + MAX_REL].astype(F32)
    shifted = jnp.tile(diag, (1, rows + 1))[:, :rows * (n + 1)].reshape(N_HEADS_A, rows, n + 1)[:, :, :window]
    bias = shifted[:, ::-1, :] * LOG2E
    i = np.arange(rows)[:, None]
    j = np.arange(window)[None, :]
    first = (i // CHUNK) * CHUNK
    return jnp.where((j >= first) & (j < first + BAND), bias, NEG_INF)


def _pad_lanes(v, width=LANES):
    return jnp.pad(v.astype(F32), (0, width - v.shape[0])).reshape(1, width)


def _mixers(x, kv_cache, conv_state, ssm_state, w):
    batch, l, _ = x.shape
    nc = l // CHUNK
    t = batch * l
    x2d = x.reshape(t, D_MODEL)
    proj, dt_raw = _in_projection(x2d, w["norm_mix"], w["w_proj"])
    proj3 = proj.reshape(batch, l, PROJ_WIDTH)
    k_new = lambda rows: proj3[:, rows, K0:K0 + ATTN_WIDTH].astype(F32)
    v_new = lambda rows: proj3[:, rows, V0:V0 + ATTN_WIDTH].astype(F32)
    new_part = lambda col: (proj, pl.BlockSpec((l, ATTN_WIDTH), lambda b, c: (b, col // ATTN_WIDTH)), l)
    if kv_cache is None:
        qc = ATTN_CHUNKS_PER_STEP if nc % ATTN_CHUNKS_PER_STEP == 0 else 1
        attn, = _band_attention(proj, Q0 // ATTN_WIDTH, [new_part(K0)], [new_part(V0)],
                                _rel_bias_band(w["rel_table"], qc), batch=batch, nc=nc, pad_rows=ATTN_REACH, qc=qc)
        keep = slice(l - min(ATTN_REACH, l), l)
        k_keep, v_keep = k_new(keep), v_new(keep)
    else:
        assert nc == 1
        wlen = kv_cache[0].shape[1]
        cache_k, cache_v = (u.reshape(batch, wlen, ATTN_WIDTH) for u in kv_cache)
        cache_part = lambda u: (u, pl.BlockSpec((None, wlen, ATTN_WIDTH), lambda b, c: (b, 0, 0)), wlen)
        attn, k_keep, v_keep = _band_attention(
            proj, Q0 // ATTN_WIDTH, [cache_part(cache_k), new_part(K0)], [cache_part(cache_v), new_part(V0)],
            _rel_bias_band(w["rel_table"], 1), batch=batch, nc=nc, pad_rows=BAND - (wlen + l), qc=1, roll_out=True)
    tail = proj3[:, l - (CONV_WIDTH - 1):, XBC0:XBC0 + CONV_DIM].astype(F32)
    if conv_state is None:
        conv0 = st0 = None
        conv_new = tail
    else:
        conv0 = jnp.pad(conv_state, ((0, 0), (CARRY_ROWS - (CONV_WIDTH - 1), 0), (0, 0)))
        st0 = jnp.transpose(ssm_state.reshape(batch, D_INNER, D_STATE), (0, 2, 1))
        conv_new = jnp.concatenate([conv_state, tail], axis=1)[:, -(CONV_WIDTH - 1):]
    ssm, st_t = _ssd_mixer(proj, dt_raw, conv0, st0, w["conv_w"], w["conv_b"], w["dt_bias"], w["a_log"], w["d_exp"],
                           w["ssm_norm"], batch=batch, nc=nc)
    ssm_new = jnp.transpose(st_t, (0, 2, 1)).reshape(batch, SSM_HEADS, SSM_HEAD_DIM, D_STATE)
    heads = lambda u: u.reshape(batch, u.shape[1], N_HEADS_A, HEAD_DIM_A)
    return (x2d, attn, ssm, proj), (heads(k_keep), heads(v_keep), conv_new, ssm_new)


def kernel(x_prompt, x_sample, cache_attn_k, cache_attn_v, state_conv, state_ssm, norm_mix, w_in, rel_bias, conv_w, conv_b, dt_bias, a_log, d_skip, ssm_norm, w_branch_a, w_branch_b, w_out, norm_ffn, w_router_group, b_router_group, w_router_expert, b_router_expert, w_gate, w_up, w_down, norm_final):
    assert norm_mix.shape[0] == 1, "single-layer trunk"
    q, k, v, z, xbc, dtc, ga, gb = jnp.split(w_in[0], np.cumsum(
        (ATTN_WIDTH, ATTN_WIDTH, ATTN_WIDTH, D_INNER, CONV_DIM, SSM_HEADS, D_MODEL, D_MODEL))[:-1].tolist(), axis=1)
    w_proj = jnp.concatenate([z, ga, gb, xbc, q, k, v, jnp.pad(dtc, ((0, 0), (0, LANES - SSM_HEADS)))],
                             axis=1).astype(BF16)
    w_router = jnp.pad(jnp.concatenate([w_router_group[0], w_router_expert[0]], axis=1),
                       ((0, 0), (0, LANES - N_EXPERT_GROUPS - N_EXPERTS)))
    w_router_hi = w_router.astype(BF16)
    w_router_lo = (w_router - w_router_hi.astype(F32)).astype(BF16)
    b_router = _pad_lanes(jnp.concatenate([b_router_group[0], b_router_expert[0]]))
    w = dict(
        norm_mix=norm_mix[0].reshape(1, D_MODEL), w_proj=w_proj, rel_table=rel_bias[0],
        conv_w=conv_w[0], conv_b=conv_b[0].reshape(1, CONV_DIM), dt_bias=_pad_lanes(dt_bias[0]),
        a_log=_pad_lanes(a_log[0]), d_exp=jnp.repeat(d_skip[0], SSM_HEAD_DIM).reshape(1, D_INNER),
        ssm_norm=ssm_norm[0].reshape(1, D_INNER),
    )
    mix_w = (w_branch_a[0].astype(BF16), w_branch_b[0].astype(BF16), w_out[0].astype(BF16),
             norm_ffn[0].reshape(1, D_MODEL), jnp.concatenate([w_router_hi, w_router_lo], axis=1), b_router)
    norm_out = norm_final.reshape(1, D_MODEL)

    groups = [_mixers(x_prompt, None, None, None, w),
              _mixers(x_sample, (cache_attn_k[0], cache_attn_v[0]), state_conv[0], state_ssm[0], w)]
    counts = jnp.zeros((1, LANES), F32)
    mixed = []
    routes_t = []
    for acts, _ in groups:
        x1, hn, route, route_t, counts = _mix_and_route(*acts, *mix_w, counts)
        mixed.append((x1, hn, route))
        routes_t.append(route_t)
    dest8, fill_start, fill_n, block_expert, block_valid, n_slots = _dispatch_plan(
        jnp.concatenate(routes_t, axis=1), counts)
    xs = _dispatch(dest8, fill_start, fill_n, [m[1] for m in mixed], n_slots)
    yb = _routed_experts(xs, block_expert, block_valid, w_gate[0], w_up[0], w_down[0])
    outs, first = [], 0
    for (x1, _, route), (acts, _) in zip(mixed, groups):
        n_tiles = x1.shape[0] // ROW_TILE
        y = _combine_and_norm(dest8[first:first + n_tiles], x1, route, norm_out, yb)
        first += n_tiles
        outs.append(y)
    (yp, ys), ((kp, vp, cp, sp), (ks, vs, cs, ss)) = outs, [g[1] for g in groups]
    yp = yp.reshape(x_prompt.shape)
    ys = ys.reshape(x_sample.shape)
    return (yp, ys, kp[None], vp[None], cp[None], sp[None], ks[None], vs[None], cs[None], ss[None])
```

```python
import functools
import math

import numpy as np
import jax
import jax.numpy as jnp
from jax import lax
from jax.experimental import pallas as pl
from jax.experimental.pallas import tpu as pltpu

F32 = jnp.float32
BF16 = jnp.bfloat16

D_MODEL = 1024
CHUNK = 64
LEFT_CHUNKS = 8
ATTN_REACH = LEFT_CHUNKS * CHUNK
BAND = ATTN_REACH + CHUNK
N_HEADS_A = 8
HEAD_DIM_A = 64
ATTN_WIDTH = N_HEADS_A * HEAD_DIM_A
MAX_REL = 256
ATTN_SCALE = 1.0 / math.sqrt(HEAD_DIM_A)
NEG_INF = -1e30
SSM_HEADS = 16
SSM_HEAD_DIM = 64
D_INNER = SSM_HEADS * SSM_HEAD_DIM
SSM_GROUPS = 2
GROUP_WIDTH = D_INNER // SSM_GROUPS
D_STATE = 128
CONV_WIDTH = 4
CONV_DIM = D_INNER + 2 * SSM_GROUPS * D_STATE
N_EXPERT_GROUPS = 4
EXPERTS_PER_GROUP = 8
N_EXPERTS = N_EXPERT_GROUPS * EXPERTS_PER_GROUP
TOP_K = 2
D_EXPERT = 512
MOE_BLOCK = 256
EPS = 1e-6

LANES = 128
SUBLANES = 8
Z0, GA0, GB0, XBC0, Q0, K0, V0 = 0, 1024, 2048, 3072, 4608, 5120, 5632
PROJ_WIDTH = V0 + ATTN_WIDTH
PROJ_TILE = 1024
ROW_TILE = 256
ATTN_CHUNKS_PER_STEP = 2
SSD_STREAMS = 4
VMEM_LIMIT = 56 * 1024 * 1024

_NT = (((1,), (1,)), ((), ()))


def _params(*sem, **kw):
    return pltpu.CompilerParams(dimension_semantics=sem, vmem_limit_bytes=VMEM_LIMIT, **kw)


def _const_spec(shape):
    nd = len(shape)
    return pl.BlockSpec(shape, lambda *_: (0,) * nd)


def _inproj_kernel(x_ref, g_ref, w_ref, o_ref, dt_ref):
    x = x_ref[...]
    ms = jnp.mean(x * x, axis=-1, keepdims=True)
    h = (x * lax.rsqrt(ms + EPS) * g_ref[...]).astype(BF16)
    for j in range(0, PROJ_WIDTH, PROJ_TILE):
        o_ref[:, j:j + PROJ_TILE] = jnp.dot(h, w_ref[:, j:j + PROJ_TILE], preferred_element_type=F32).astype(BF16)
    dt_ref[...] = jnp.dot(h, w_ref[:, PROJ_WIDTH:PROJ_WIDTH + LANES], preferred_element_type=F32)


PROJ_ROWS = 512


def _in_projection(x2d, norm_w, w_proj):
    t = x2d.shape[0]
    rows = PROJ_ROWS if t % PROJ_ROWS == 0 else ROW_TILE
    return pl.pallas_call(
        _inproj_kernel,
        grid=(t // rows,),
        in_specs=[pl.BlockSpec((rows, D_MODEL), lambda i: (i, 0)),
                  _const_spec((1, D_MODEL)),
                  _const_spec((D_MODEL, PROJ_WIDTH + LANES))],
        out_specs=[pl.BlockSpec((rows, PROJ_WIDTH), lambda i: (i, 0)),
                   pl.BlockSpec((rows, LANES), lambda i: (i, 0))],
        out_shape=[jax.ShapeDtypeStruct((t, PROJ_WIDTH), BF16), jax.ShapeDtypeStruct((t, LANES), F32)],
        name="in_projection",
        compiler_params=_params("parallel"),
    )(x2d, norm_w, w_proj)


LOG2E = math.log2(math.e)
ATTN_STAGES = 5


def _attn_kernel(q_ref, *refs, pad_rows, part_rows, qc, roll_out):
    n_parts = len(part_rows)
    k_parts, v_parts = refs[:n_parts], refs[n_parts:2 * n_parts]
    bias_ref, o_ref = refs[2 * n_parts:2 * n_parts + 2]
    rolled = refs[2 * n_parts + 2:-2]
    kpad, vpad = refs[-2:]
    rows = qc * CHUNK
    window = ATTN_REACH + rows
    c = pl.program_id(1)

    @pl.when(c == 0)
    def _():
        if pad_rows:
            kpad[0:pad_rows, :] = jnp.zeros((pad_rows, ATTN_WIDTH), BF16)
            vpad[0:pad_rows, :] = jnp.zeros((pad_rows, ATTN_WIDTH), BF16)
        first = pad_rows
        for k_ref, v_ref, n in zip(k_parts, v_parts, part_rows):
            kpad[first:first + n, :] = k_ref[...].astype(BF16)
            vpad[first:first + n, :] = v_ref[...].astype(BF16)
            first += n
        if roll_out:
            kept, new = part_rows[0] - part_rows[1], part_rows[1]
            for out_ref, parts in zip(rolled, (k_parts, v_parts)):
                out_ref[0:kept, :] = parts[0][new:new + kept, :]
                out_ref[kept:kept + new, :] = parts[1][...].astype(F32)

    start = pl.multiple_of(c * rows, rows)

    def body(mask_start):
        kw = kpad[pl.ds(start, window), :]
        vw = vpad[pl.ds(start, window), :]
        q = (q_ref[...].astype(F32) * (ATTN_SCALE * LOG2E)).astype(BF16)
        low = lax.broadcasted_iota(jnp.int32, (rows, LANES), 1) < HEAD_DIM_A
        if mask_start:
            valid = lax.broadcasted_iota(jnp.int32, (rows, window), 1) + start >= pad_rows
        outs = {}

        def head(h):
            hp, sub = divmod(h, 2)
            sl = slice(LANES * hp, LANES * (hp + 1))
            q2 = q[:, sl]
            qm = jnp.where(low if sub == 0 else jnp.logical_not(low), q2, jnp.zeros_like(q2))
            s = lax.dot_general(qm, kw[:, sl], _NT, preferred_element_type=F32) + bias_ref[h]
            if mask_start:
                s = jnp.where(valid, s, NEG_INF)
            yield
            m = jnp.max(s, axis=-1, keepdims=True)
            yield
            p = jnp.exp2(s - m)
            yield
            l = jnp.sum(p, axis=-1, keepdims=True)
            pb = p.astype(BF16)
            yield
            outs[h] = jnp.dot(pb, vw[:, sl], preferred_element_type=F32) / l
            if sub == 1:
                o_ref[:, sl] = jnp.where(low, outs[h - 1], outs[h]).astype(o_ref.dtype)

        heads = [head(h) for h in range(N_HEADS_A)]
        for t in range(N_HEADS_A + ATTN_STAGES - 1):
            for h in range(min(t, N_HEADS_A - 1), max(t - ATTN_STAGES, -1), -1):
                next(heads[h], None)

    if pad_rows:
        pl.when(start < pad_rows)(lambda: body(True))
        pl.when(start >= pad_rows)(lambda: body(False))
    else:
        body(False)


def _band_attention(q_arr, q_col, k_parts, v_parts, bias, *, batch, nc, pad_rows, qc, roll_out=False):
    steps = nc // qc
    rows = qc * CHUNK
    part_rows = tuple(p[2] for p in k_parts)
    lk = sum(part_rows)
    kern = functools.partial(_attn_kernel, pad_rows=pad_rows, part_rows=part_rows, qc=qc, roll_out=roll_out)
    out_specs = [pl.BlockSpec((rows, ATTN_WIDTH), lambda b, c: (b * steps + c, 0))]
    out_shape = [jax.ShapeDtypeStruct((batch * nc * CHUNK, ATTN_WIDTH), BF16)]
    if roll_out:
        assert len(part_rows) == 2 and nc == 1
        out_specs += [pl.BlockSpec((None, part_rows[0], ATTN_WIDTH), lambda b, c: (b, 0, 0))] * 2
        out_shape += [jax.ShapeDtypeStruct((batch, part_rows[0], ATTN_WIDTH), F32)] * 2
    return pl.pallas_call(
        kern,
        grid=(batch, steps),
        in_specs=[pl.BlockSpec((rows, ATTN_WIDTH), lambda b, c: (b * steps + c, q_col))]
        + [p[1] for p in k_parts] + [p[1] for p in v_parts]
        + [_const_spec((N_HEADS_A, rows, ATTN_REACH + rows))],
        out_specs=out_specs,
        out_shape=out_shape,
        scratch_shapes=[pltpu.VMEM((pad_rows + lk, ATTN_WIDTH), BF16),
                        pltpu.VMEM((pad_rows + lk, ATTN_WIDTH), BF16)],
        name="band_attention",
        compiler_params=_params("parallel", "arbitrary"),
    )(q_arr, *[p[0] for p in k_parts], *[p[0] for p in v_parts], bias)


CARRY_ROWS = SUBLANES
SHIFT_TAPS = CONV_WIDTH - 1
SHIFT_K = 2 * CARRY_ROWS + CHUNK


def _split_bf16(x, parts):
    out = []
    for _ in range(parts - 1):
        h = x.astype(BF16)
        out.append(h)
        x = x - h.astype(F32)
    out.append(x.astype(BF16))
    return out


def _ssd_kernel(xbc_ref, z_ref, dt_ref, conv0_ref, st0_ref, cw_ref, cb_ref, dtb_ref, alog_ref,
                dexp_ref, nw_ref, tri_ref, exp_ref, sel_ref, eye_ref, shift_ref,
                y_ref, stout_ref, carry, st, *, has_init, streams, nc):
    c = pl.program_id(1)

    @pl.when(c == 0)
    def _():
        if has_init:
            carry[...] = conv0_ref[...]
            st[...] = st0_ref[...]
        else:
            carry[...] = jnp.zeros((streams, CARRY_ROWS, CONV_DIM), F32)
            st[...] = jnp.zeros((streams, D_STATE, D_INNER), F32)

    chunks = [_ssd_chunk(xbc_ref.at[s], z_ref.at[s], dt_ref.at[s], cw_ref, cb_ref, dtb_ref, alog_ref, dexp_ref,
                         nw_ref, tri_ref, exp_ref, sel_ref, eye_ref, shift_ref, y_ref.at[s], carry.at[s], st.at[s])
              for s in range(streams)]
    while chunks:
        chunks = [g for g in chunks if next(g, True) is None]

    @pl.when(c == nc - 1)
    def _():
        stout_ref[...] = st[...]


def _ssd_chunk(xbc_ref, z_ref, dt_ref, cw_ref, cb_ref, dtb_ref, alog_ref, dexp_ref, nw_ref,
               tri_ref, exp_ref, sel_ref, eye_ref, shift_ref, y_ref, carry, st):
    prev = carry[...]
    prev_hi = prev.astype(BF16).astype(F32)
    prev_parts = jnp.concatenate([prev_hi, prev - prev_hi], axis=0).astype(BF16)
    taps = jnp.dot(shift_ref[...], jnp.concatenate([prev_parts, xbc_ref[...]], axis=0), preferred_element_type=F32)

    dt_in = dt_ref[...] + dtb_ref[...]
    dt = jnp.maximum(dt_in, 0.0) + jnp.log1p(jnp.exp(-jnp.abs(dt_in)))
    da = dt * (-jnp.exp(alog_ref[...]))
    tri = tri_ref[...]
    cs3 = jnp.dot(tri, jnp.concatenate(_split_bf16(da, 3), axis=1), preferred_element_type=F32)
    yield

    cs = cs3[:, 0:LANES] + cs3[:, LANES:2 * LANES] + cs3[:, 2 * LANES:3 * LANES]
    cs_last = cs[CHUNK - 1:CHUNK, :]
    ecs = jnp.exp(cs)
    to_end = jnp.exp(cs_last - cs)

    stack = jnp.concatenate(_split_bf16(dt, 1) + _split_bf16(to_end, 1) + _split_bf16(ecs, 2)
                            + _split_bf16(cs, 3), axis=0)
    wide = jnp.dot(stack, exp_ref[...], preferred_element_type=F32)

    lane = lax.broadcasted_iota(jnp.int32, (CHUNK, LANES), 1)
    even = (lane % 2) == 0
    csm = jnp.concatenate([jnp.where(even, cs, 0.0), jnp.where(even, 0.0, cs)], axis=0)
    srow3 = lax.dot_general(sel_ref[...], jnp.concatenate(_split_bf16(csm, 3), axis=0), _NT,
                            preferred_element_type=F32)
    yield

    rows = [wide[i * CHUNK:(i + 1) * CHUNK, :] for i in range(7)]
    dt_w = rows[0]
    to_end_w = rows[1]
    ecs_w = rows[2] + rows[3]
    cs_w = rows[4] + rows[5] + rows[6]
    srow = srow3[:, 0:LANES] + srow3[:, LANES:2 * LANES] + srow3[:, 2 * LANES:3 * LANES]

    xdt_b, xw_b, skip, bc = [], [], [], []
    for j in range(CONV_DIM // LANES):
        sl = slice(j * LANES, (j + 1) * LANES)
        conv = cb_ref[:, sl] + xbc_ref[:, sl].astype(F32) * cw_ref[CONV_WIDTH - 1:CONV_WIDTH, sl]
        for tap in range(SHIFT_TAPS):
            conv = conv + taps[tap * CHUNK:(tap + 1) * CHUNK, sl] * cw_ref[tap:tap + 1, sl]
        u = conv * jax.nn.sigmoid(conv)
        if j < D_INNER // LANES:
            xdt = u * dt_w[:, sl]
            xdt_b.append(xdt.astype(BF16))
            xw_b.append((xdt * to_end_w[:, sl]).astype(BF16))
            skip.append(u * dexp_ref[:, sl])
        else:
            bc.append(u.astype(BF16))
    low = lane < SSM_HEAD_DIM
    t_idx = lax.broadcasted_iota(jnp.int32, (CHUNK, LANES), 0)
    causal = t_idx >= (lane % SSM_HEAD_DIM)
    eye = eye_ref[...]
    pairs = SSM_HEADS // SSM_GROUPS // 2
    yield

    for g in range(SSM_GROUPS):
        b_g, c_g = bc[g], bc[SSM_GROUPS + g]
        gsl = slice(g * GROUP_WIDTH, (g + 1) * GROUP_WIDTH)
        st_g = st[:, gsl]
        y_off = jnp.dot(c_g, st_g.astype(BF16), preferred_element_type=F32)
        b_t = lax.dot_general(eye, b_g, _NT, preferred_element_type=F32).astype(BF16)
        xw_g = jnp.concatenate(xw_b[g * pairs:(g + 1) * pairs], axis=1)
        st[:, gsl] = st_g * ecs_w[CHUNK - 1:CHUNK, gsl] + jnp.dot(b_t, xw_g, preferred_element_type=F32)
        cb2 = lax.dot_general(c_g, jnp.concatenate([b_g, b_g], axis=0), _NT, preferred_element_type=F32)
        yield
        gated, sumsq = [], 0.0
        for kk in range(pairs):
            k = g * pairs + kk
            psl = slice(k * LANES, (k + 1) * LANES)
            decay = jnp.where(causal, jnp.exp(cs_w[:, psl] - srow[k:k + 1, :]), 0.0)
            gmat = (cb2 * decay).astype(BF16)
            xp = xdt_b[k]
            xblk = jnp.concatenate([jnp.where(low, xp, jnp.zeros_like(xp)),
                                    jnp.where(low, jnp.zeros_like(xp), xp)], axis=0)
            y = (jnp.dot(gmat, xblk, preferred_element_type=F32)
                 + y_off[:, kk * LANES:(kk + 1) * LANES] * ecs_w[:, psl] + skip[k])
            zz = z_ref[:, psl].astype(F32)
            y = y * (zz * jax.nn.sigmoid(zz))
            gated.append(y)
            sumsq = sumsq + jnp.sum(y * y, axis=-1, keepdims=True)
        scale = lax.rsqrt(sumsq * (1.0 / GROUP_WIDTH) + EPS)
        for kk in range(pairs):
            psl = slice((g * pairs + kk) * LANES, (g * pairs + kk + 1) * LANES)
            y_ref[:, psl] = (gated[kk] * scale * nw_ref[:, psl]).astype(y_ref.dtype)
        yield
    carry[...] = xbc_ref[CHUNK - 2 * CARRY_ROWS:CHUNK, :].astype(F32)[CARRY_ROWS:, :]


def _ssd_constants():
    tri = np.tril(np.ones((CHUNK, CHUNK), np.float32))
    expand = np.zeros((LANES, D_INNER), np.float32)
    for e in range(SSM_HEADS):
        expand[e, e * SSM_HEAD_DIM:(e + 1) * SSM_HEAD_DIM] = 1.0
    sel = np.zeros((16, LANES), np.float32)
    for e in range(SSM_HEADS):
        sel[e // 2, e] = 1.0
    eye = np.eye(D_STATE, dtype=np.float32)
    shift = np.zeros((SHIFT_TAPS * CHUNK, SHIFT_K), np.float32)
    for j in range(SHIFT_TAPS):
        for t in range(CHUNK):
            m = t - (CONV_WIDTH - 1) + j
            if m >= 0:
                shift[j * CHUNK + t, 2 * CARRY_ROWS + m] = 1.0
            else:
                shift[j * CHUNK + t, CARRY_ROWS + m] = 1.0
                shift[j * CHUNK + t, 2 * CARRY_ROWS + m] = 1.0
    return tuple(jnp.asarray(a, BF16) for a in (tri, expand, sel, eye, shift))


def _ssd_mixer(proj, dt_raw, conv0, st0, conv_w, conv_b, dt_bias_p, a_log_p, d_exp, norm_w, *, batch, nc):
    streams = SSD_STREAMS if batch % SSD_STREAMS == 0 else 1
    groups = batch // streams
    l = nc * CHUNK
    has_init = conv0 is not None
    if not has_init:
        conv0 = jnp.zeros((1, streams, CARRY_ROWS, CONV_DIM), F32)
        st0 = jnp.zeros((1, streams, D_STATE, D_INNER), F32)
        init_map = lambda b, c: (0, 0, 0, 0)
    else:
        conv0 = conv0.reshape(groups, streams, CARRY_ROWS, CONV_DIM)
        st0 = st0.reshape(groups, streams, D_STATE, D_INNER)
        init_map = lambda b, c: (b, 0, 0, 0)
    tri, expand, sel, eye, shift = _ssd_constants()
    proj4 = proj.reshape(groups, streams, l, PROJ_WIDTH)
    dt4 = dt_raw.reshape(groups, streams, l, LANES)
    chunk_of = lambda w, col: pl.BlockSpec((None, streams, CHUNK, w), lambda b, c: (b, 0, c, col))
    state_spec = lambda m: pl.BlockSpec((None, streams, D_STATE, D_INNER), m)
    kern = functools.partial(_ssd_kernel, has_init=has_init, streams=streams, nc=nc)
    y, st_out = pl.pallas_call(
        kern,
        grid=(groups, nc),
        in_specs=[chunk_of(CONV_DIM, XBC0 // CONV_DIM), chunk_of(D_INNER, Z0 // D_INNER), chunk_of(LANES, 0),
                  pl.BlockSpec((None, streams, CARRY_ROWS, CONV_DIM), init_map), state_spec(init_map),
                  _const_spec((CONV_WIDTH, CONV_DIM)), _const_spec((1, CONV_DIM)),
                  _const_spec((1, LANES)), _const_spec((1, LANES)),
                  _const_spec((1, D_INNER)), _const_spec((1, D_INNER)),
                  _const_spec((CHUNK, CHUNK)), _const_spec((LANES, D_INNER)),
                  _const_spec((16, LANES)), _const_spec((D_STATE, D_STATE)),
                  _const_spec((SHIFT_TAPS * CHUNK, SHIFT_K))],
        out_specs=[chunk_of(D_INNER, 0), state_spec(lambda b, c: (b, 0, 0, 0))],
        out_shape=[jax.ShapeDtypeStruct((groups, streams, l, D_INNER), BF16),
                   jax.ShapeDtypeStruct((groups, streams, D_STATE, D_INNER), F32)],
        scratch_shapes=[pltpu.VMEM((streams, CARRY_ROWS, CONV_DIM), F32),
                        pltpu.VMEM((streams, D_STATE, D_INNER), F32)],
        name="ssd_mixer",
        compiler_params=_params("parallel", "arbitrary"),
    )(proj4, proj4, dt4, conv0, st0, conv_w, conv_b, dt_bias_p, a_log_p, d_exp, norm_w, tri, expand, sel, eye, shift)
    return y.reshape(batch * l, D_INNER), st_out.reshape(batch, D_STATE, D_INNER)


ROW_SUBLANES = D_MODEL // LANES
assert ROW_SUBLANES == SUBLANES


def _store_rows(ref, x, rows, offset=0):
    for s in range(ROW_SUBLANES):
        ref[pl.ds(offset + s, rows, stride=ROW_SUBLANES), :] = x[:, s * LANES:(s + 1) * LANES]


def _load_rows(ref, rows, offset=0):
    return jnp.concatenate([ref[pl.ds(offset + s, rows, stride=ROW_SUBLANES), :] for s in range(ROW_SUBLANES)],
                           axis=1)


def _row_tile(ref, idx8):
    return ref.at[pl.ds(pl.multiple_of(idx8, ROW_SUBLANES), ROW_SUBLANES)]


def _mix_kernel(x_ref, attn_ref, ssm_ref, ga_ref, gb_ref, wa_ref, wb_ref, wo_ref, g_ref, wr_ref, br_ref, slt_ref,
                count0_ref, x1_ref, hn_ref, route_ref, route_t_ref, counts_ref, carry, *, subtiles):
    @pl.when(pl.program_id(0) == 0)
    def _():
        carry[...] = count0_ref[...]

    running = {"counts": carry[...]}
    tiles = [_mix_tile(j * ROW_TILE, running, x_ref, attn_ref, ssm_ref, ga_ref, gb_ref, wa_ref, wb_ref, wo_ref, g_ref,
                       wr_ref, br_ref, slt_ref, x1_ref, hn_ref, route_ref, route_t_ref) for j in range(subtiles)]
    while tiles:
        tiles = [g for g in tiles if next(g, True) is None]
    carry[...] = running["counts"]
    counts_ref[...] = running["counts"]


def _mix_tile(r0, running, x_ref, attn_ref, ssm_ref, ga_ref, gb_ref, wa_ref, wb_ref, wo_ref, g_ref, wr_ref, br_ref,
              slt_ref, x1_ref, hn_ref, route_ref, route_t_ref):
    rows = slice(r0, r0 + ROW_TILE)
    a = jnp.dot(attn_ref[rows, :], wa_ref[...], preferred_element_type=F32)
    s = jnp.dot(ssm_ref[rows, :], wb_ref[...], preferred_element_type=F32)
    yield
    mixed = jax.nn.sigmoid(ga_ref[rows, :].astype(F32)) * a + jax.nn.sigmoid(gb_ref[rows, :].astype(F32)) * s
    x1 = x_ref[rows, :] + jnp.dot(mixed.astype(BF16), wo_ref[...], preferred_element_type=F32)
    yield
    x1_ref[rows, :] = x1
    ms = jnp.mean(x1 * x1, axis=-1, keepdims=True)
    hn = x1 * lax.rsqrt(ms + EPS) * g_ref[...]
    _store_rows(hn_ref, hn, ROW_TILE, offset=r0 * ROW_SUBLANES)

    hi = hn.astype(BF16)
    lo = (hn - hi.astype(F32)).astype(BF16)
    both = jnp.dot(hi, wr_ref[...], preferred_element_type=F32)
    lo_part = jnp.dot(lo, wr_ref[:, 0:LANES], preferred_element_type=F32)
    yield
    logits = both[:, 0:LANES] + both[:, LANES:2 * LANES] + lo_part + br_ref[...]
    lane = lax.broadcasted_iota(jnp.int32, logits.shape, 1)
    lane_f = lane.astype(F32)
    big = float(LANES)
    is_g = lane < N_EXPERT_GROUPS
    gl = jnp.where(is_g, logits, -jnp.inf)
    gmax = jnp.max(gl, axis=-1, keepdims=True)
    gidx = jnp.min(jnp.where(gl == gmax, lane_f, big), axis=-1, keepdims=True)
    gsum = jnp.sum(jnp.where(is_g, jnp.exp(gl - gmax), 0.0), axis=-1, keepdims=True)
    g_top = 1.0 / gsum
    first = N_EXPERT_GROUPS + gidx * EXPERTS_PER_GROUP
    in_group = (lane_f >= first) & (lane_f < first + EXPERTS_PER_GROUP)
    el = jnp.where(in_group, logits, -jnp.inf)
    m1 = jnp.max(el, axis=-1, keepdims=True)
    i1 = jnp.min(jnp.where(el == m1, lane_f, big), axis=-1, keepdims=True)
    el2 = jnp.where(lane_f == i1, -jnp.inf, el)
    m2 = jnp.max(el2, axis=-1, keepdims=True)
    i2 = jnp.min(jnp.where(el2 == m2, lane_f, big), axis=-1, keepdims=True)
    r = jnp.exp(m2 - m1)
    w1 = g_top / (1.0 + r)
    w2 = g_top * r / (1.0 + r)
    e1 = i1 - N_EXPERT_GROUPS
    e2 = i2 - N_EXPERT_GROUPS

    oh1 = lane_f == e1
    oh2 = lane_f == e2
    hot1 = jnp.where(oh1, 1.0, 0.0)
    hot2 = jnp.where(oh2, 1.0, 0.0)
    onehots = jnp.concatenate([hot1, hot2], axis=1).astype(BF16)
    before = jnp.dot(slt_ref[...], onehots, preferred_element_type=F32)
    cnt1 = jnp.sum(hot1, axis=0, keepdims=True)
    cnt2 = jnp.sum(hot2, axis=0, keepdims=True)
    yield
    base = running["counts"]
    rank1 = jnp.sum(jnp.where(oh1, before[:, 0:LANES] + base, 0.0), axis=-1, keepdims=True)
    rank2 = jnp.sum(jnp.where(oh2, before[:, LANES:2 * LANES] + (base + cnt1), 0.0), axis=-1, keepdims=True)
    running["counts"] = base + cnt1 + cnt2

    route = jnp.zeros_like(logits)
    for col, val in enumerate((e1, e2, w1, w2, rank1, rank2)):
        route = jnp.where(lane == col, val, route)
    route_ref[rows, :] = route
    route_t_ref[:, rows] = jnp.transpose(route)[0:SUBLANES, :]


MIX_SUBTILES = 2


def _mix_and_route(x2d, attn, ssm, proj, wa, wb, wo, norm_w, w_router, b_router, count0):
    t = x2d.shape[0]
    subtiles = MIX_SUBTILES if t % (MIX_SUBTILES * ROW_TILE) == 0 else 1
    step_rows = subtiles * ROW_TILE
    tile = lambda w, col: pl.BlockSpec((step_rows, w), lambda i: (i, col))
    slt = jnp.asarray(np.tril(np.ones((ROW_TILE, ROW_TILE), np.float32), k=-1), BF16)
    return pl.pallas_call(
        functools.partial(_mix_kernel, subtiles=subtiles),
        grid=(t // step_rows,),
        in_specs=[tile(D_MODEL, 0), tile(ATTN_WIDTH, 0), tile(D_INNER, 0),
                  tile(D_MODEL, GA0 // D_MODEL), tile(D_MODEL, GB0 // D_MODEL),
                  _const_spec((ATTN_WIDTH, D_MODEL)), _const_spec((D_INNER, D_MODEL)),
                  _const_spec((D_MODEL, D_MODEL)), _const_spec((1, D_MODEL)),
                  _const_spec((D_MODEL, 2 * LANES)), _const_spec((1, LANES)),
                  _const_spec((ROW_TILE, ROW_TILE)), _const_spec((1, LANES))],
        out_specs=[tile(D_MODEL, 0), pl.BlockSpec((step_rows * ROW_SUBLANES, LANES), lambda i: (i, 0)),
                   tile(LANES, 0), pl.BlockSpec((SUBLANES, step_rows), lambda i: (0, i)), _const_spec((1, LANES))],
        out_shape=[jax.ShapeDtypeStruct((t, D_MODEL), F32),
                   jax.ShapeDtypeStruct((t * ROW_SUBLANES, LANES), F32),
                   jax.ShapeDtypeStruct((t, LANES), F32),
                   jax.ShapeDtypeStruct((SUBLANES, t), F32),
                   jax.ShapeDtypeStruct((1, LANES), F32)],
        scratch_shapes=[pltpu.VMEM((1, LANES), F32)],
        name="mix_and_route",
        compiler_params=_params("arbitrary"),
    )(x2d, attn, ssm, proj, proj, wa, wb, wo, norm_w, w_router, b_router, slt, count0)


TILE_ROW_COPIES = ROW_TILE * TOP_K
INDEX_BATCH = 8
DISPATCH_BUFFERS = 3
DMA_THREADS = 2


def _tile_wait(ref_a, ref_b, sem):
    n = TILE_ROW_COPIES * ROW_SUBLANES
    pltpu.make_async_copy(ref_a.at[pl.ds(0, n)], ref_b.at[pl.ds(0, n)], sem).wait()


def _dispatch_kernel(fill_start_ref, fill_n_ref, dest_ref, *refs, tiles_per_source, fill_rows):
    n_src = len(tiles_per_source)
    sources = refs[:n_src]
    xs_hbm, hbuf, zbuf, in_sem, out_sem, fill_sem = refs[n_src:]
    i = pl.program_id(0)
    n = pl.num_programs(0)
    tile_rows = ROW_TILE * ROW_SUBLANES

    def tile_copy(src, j, b):
        return pltpu.make_async_copy(src.at[pl.ds(pl.multiple_of(j * tile_rows, tile_rows), tile_rows)],
                                     hbuf.at[b], in_sem.at[b])

    def tile_in_start(j):
        b = j % DISPATCH_BUFFERS
        first = 0
        for src, count in zip(sources, tiles_per_source):
            pl.when((j >= first) & (j < first + count))(lambda src=src, first=first: tile_copy(src, j - first, b).start())
            first += count

    @pl.when(i == 0)
    def _():
        zbuf[...] = jnp.zeros((ROW_SUBLANES, LANES), F32)
        tile_in_start(i)
        pl.when(n > 1)(lambda: tile_in_start(i + 1))

    buf = i % DISPATCH_BUFFERS
    tile_copy(sources[0], 0, buf).wait()
    for r0 in range(0, ROW_TILE, INDEX_BATCH):
        dests = [[dest_ref[0, k * ROW_TILE + r] for r in range(r0, r0 + INDEX_BATCH)] for k in range(TOP_K)]
        for r in range(r0, r0 + INDEX_BATCH):
            src = hbuf.at[buf, pl.ds(r * ROW_SUBLANES, ROW_SUBLANES)]
            for k in range(TOP_K):
                dst = _row_tile(xs_hbm, dests[k][r - r0])
                pltpu.make_async_copy(src, dst, out_sem.at[buf]).start(priority=k % DMA_THREADS)

    @pl.when(i > 0)
    def _():
        _tile_wait(xs_hbm, xs_hbm, out_sem.at[(i - 1) % DISPATCH_BUFFERS])

    pl.when(i + 2 < n)(lambda: tile_in_start(i + 2))

    @pl.when(i == n - 1)
    def _():
        _tile_wait(xs_hbm, xs_hbm, out_sem.at[buf])

        def per_range(e, carry):
            first = fill_start_ref[e]

            def start(j, c):
                pltpu.make_async_copy(zbuf, _row_tile(xs_hbm, (first + j) * ROW_SUBLANES), fill_sem.at[0]).start()
                return c

            lax.fori_loop(0, fill_n_ref[e], start, 0)
            return carry

        lax.fori_loop(0, N_EXPERTS + 1, per_range, 0)
        filled = pl.ds(0, fill_rows * ROW_SUBLANES)
        pltpu.make_async_copy(xs_hbm.at[filled], xs_hbm.at[filled], fill_sem.at[0]).wait()


def _dispatch(dest8, fill_start, fill_n, hn_list, n_slots):
    n_tiles = dest8.shape[0]
    tiles_per_source = tuple(h.shape[0] // (ROW_TILE * ROW_SUBLANES) for h in hn_list)
    assert sum(tiles_per_source) == n_tiles
    grid_spec = pltpu.PrefetchScalarGridSpec(
        num_scalar_prefetch=2,
        grid=(n_tiles,),
        in_specs=[pl.BlockSpec((None, 1, TILE_ROW_COPIES), lambda i, fs, fn: (i, 0, 0), memory_space=pltpu.SMEM)]
        + [pl.BlockSpec(memory_space=pl.ANY)] * len(hn_list),
        out_specs=pl.BlockSpec(memory_space=pl.ANY),
        scratch_shapes=[pltpu.VMEM((DISPATCH_BUFFERS, ROW_TILE * ROW_SUBLANES, LANES), F32),
                        pltpu.VMEM((ROW_SUBLANES, LANES), F32),
                        pltpu.SemaphoreType.DMA((DISPATCH_BUFFERS,)), pltpu.SemaphoreType.DMA((DISPATCH_BUFFERS,)),
                        pltpu.SemaphoreType.DMA((1,))],
    )
    return pl.pallas_call(
        functools.partial(_dispatch_kernel, tiles_per_source=tiles_per_source,
                          fill_rows=n_slots - n_tiles * TILE_ROW_COPIES),
        grid_spec=grid_spec,
        out_shape=jax.ShapeDtypeStruct((n_slots * ROW_SUBLANES, LANES), F32),
        name="dispatch_rows",
        compiler_params=_params("arbitrary", disable_bounds_checks=True),
    )(fill_start, fill_n, dest8, *hn_list)


EXPERT_ROW_BUFFERS = 3
EXPERT_SPLIT = 2


def _expert_kernel(bexp_ref, bval_ref, fetch_ref, cast_ref, cset_ref, set_ref, next_ref, xs_hbm, wg_hbm, wu_hbm, wd_hbm, y_hbm,
                   wg_f, wu_f, wd_f, wg_b, wu_b, wd_b, xbuf, ybuf, in_sem, out_sem, w_sem, *, n):
    i = pl.program_id(0)
    block_rows = MOE_BLOCK * ROW_SUBLANES
    buf = i % EXPERT_ROW_BUFFERS

    def weight_copies(e):
        return [pltpu.make_async_copy(src.at[e], dst, w_sem.at[j])
                for j, (src, dst) in enumerate(((wg_hbm, wg_f), (wu_hbm, wu_f), (wd_hbm, wd_f)))]

    def cast_staged(which):
        for c in weight_copies(0):
            c.wait()
        wg_b[which] = wg_f[...].astype(BF16)
        wu_b[which] = wu_f[...].astype(BF16)
        wd_b[which] = wd_f[...].astype(BF16)

    def block_of(ref, j):
        return ref.at[pl.ds(pl.multiple_of(j * block_rows, block_rows), block_rows)]

    def copy_in(j):
        b = j % EXPERT_ROW_BUFFERS
        return pltpu.make_async_copy(block_of(xs_hbm, j), xbuf.at[b], in_sem.at[b])

    def copy_out(j):
        b = j % EXPERT_ROW_BUFFERS
        return pltpu.make_async_copy(ybuf.at[b], block_of(y_hbm, j), out_sem.at[b])

    @pl.when(i == 0)
    def _():
        for j in range(min(EXPERT_ROW_BUFFERS - 1, n)):
            copy_in(j).start()

    ahead = i + EXPERT_ROW_BUFFERS - 1
    pl.when(ahead < n)(lambda: copy_in(ahead).start())

    @pl.when(i == 0)
    def _():
        for c in weight_copies(bexp_ref[0]):
            c.start()
        cast_staged(0)

    pl.when(cast_ref[i] == 1)(lambda: cast_staged(cset_ref[i]))

    @pl.when(fetch_ref[i] == 1)
    def _():
        for c in weight_copies(next_ref[i]):
            c.start(priority=1)

    which = set_ref[i]
    copy_in(i).wait()
    behind = i - (EXPERT_ROW_BUFFERS - 1)
    pl.when(behind >= 0)(lambda: copy_out(behind).wait())

    @pl.when(bval_ref[i] > 0)
    def _():
        def half(j):
            rows = MOE_BLOCK // EXPERT_SPLIT
            off = j * rows * ROW_SUBLANES
            xb = _load_rows(xbuf.at[buf], rows, offset=off).astype(BF16)
            gate = jnp.dot(xb, wg_b[which], preferred_element_type=F32)
            up = jnp.dot(xb, wu_b[which], preferred_element_type=F32)
            yield
            hid = (gate * jax.nn.sigmoid(gate) * up).astype(BF16)
            out = jnp.dot(hid, wd_b[which], preferred_element_type=F32)
            yield
            _store_rows(ybuf.at[buf], out, rows, offset=off)

        parts = [half(j) for j in range(EXPERT_SPLIT)]
        while parts:
            parts = [g for g in parts if next(g, True) is None]

    @pl.when(bval_ref[i] == 0)
    def _():
        ybuf[buf] = jnp.zeros((block_rows, LANES), F32)

    copy_out(i).start()

    @pl.when(i == n - 1)
    def _():
        for back in range(EXPERT_ROW_BUFFERS - 1):
            pl.when(i - back >= 0)(lambda back=back: copy_out(i - back).wait())


WEIGHT_CAST_LAG = 3


def _weight_schedule(block_expert):
    n = block_expert.shape[0]
    idx = jnp.arange(n, dtype=jnp.int32)
    run_start = jnp.concatenate([jnp.ones((1,), bool), block_expert[1:] != block_expert[:-1]])
    run_id = jnp.cumsum(run_start.astype(jnp.int32)) - 1
    run_first = jnp.min(jnp.where(run_id[None, :] == idx[:, None], idx[None, :], n), axis=1)
    next_first = jnp.concatenate([run_first[1:], jnp.full((1,), n, jnp.int32)])
    has_next = next_first < n
    cast_at = jnp.where(has_next, jnp.minimum(run_first + WEIGHT_CAST_LAG, next_first), -1)
    hit = idx[:, None] == cast_at[None, :]
    cast = jnp.any(hit, axis=1).astype(jnp.int32)
    cast_set = jnp.sum(jnp.where(hit, (idx[None, :] + 1) % 2, 0), axis=1).astype(jnp.int32)
    fetch = (run_start & has_next[run_id]).astype(jnp.int32)
    next_expert = block_expert[jnp.minimum(next_first[run_id], n - 1)]
    return fetch, cast, cast_set, run_id % 2, next_expert


def _routed_experts(xs, block_expert, block_valid, wg, wu, wd):
    n_blocks = block_expert.shape[0]
    shape = (EXPERT_ROW_BUFFERS, MOE_BLOCK * ROW_SUBLANES, LANES)
    any_spec = pl.BlockSpec(memory_space=pl.ANY)
    grid_spec = pltpu.PrefetchScalarGridSpec(
        num_scalar_prefetch=7,
        grid=(n_blocks,),
        in_specs=[any_spec, any_spec, any_spec, any_spec],
        out_specs=any_spec,
        scratch_shapes=[pltpu.VMEM((D_MODEL, D_EXPERT), F32), pltpu.VMEM((D_MODEL, D_EXPERT), F32),
                        pltpu.VMEM((D_EXPERT, D_MODEL), F32),
                        pltpu.VMEM((2, D_MODEL, D_EXPERT), BF16), pltpu.VMEM((2, D_MODEL, D_EXPERT), BF16),
                        pltpu.VMEM((2, D_EXPERT, D_MODEL), BF16),
                        pltpu.VMEM(shape, F32), pltpu.VMEM(shape, F32),
                        pltpu.SemaphoreType.DMA((EXPERT_ROW_BUFFERS,)),
                        pltpu.SemaphoreType.DMA((EXPERT_ROW_BUFFERS,)),
                        pltpu.SemaphoreType.DMA((3,))],
    )
    return pl.pallas_call(
        functools.partial(_expert_kernel, n=n_blocks),
        grid_spec=grid_spec,
        out_shape=jax.ShapeDtypeStruct(xs.shape, F32),
        name="routed_experts",
        compiler_params=_params("arbitrary"),
    )(block_expert, block_valid, *_weight_schedule(block_expert), xs, wg, wu, wd)


def _combine_kernel(dest_ref, dest_next_ref, x1_ref, route_ref, g_ref, yb_hbm, o_ref, ybuf, sem, *, n):
    i = pl.program_id(0)
    slot = i % 2

    def row_copy(idx8, r, k, s):
        dst = ybuf.at[s, pl.ds((k * ROW_TILE + r) * ROW_SUBLANES, ROW_SUBLANES)]
        return pltpu.make_async_copy(_row_tile(yb_hbm, idx8), dst, sem.at[s])

    @pl.when(i == 0)
    def _():
        def body(r, c):
            idx = [dest_ref[0, k * ROW_TILE + r] for k in range(TOP_K)]
            for k in range(TOP_K):
                row_copy(idx[k], r, k, 0).start()
            return c
        lax.fori_loop(0, ROW_TILE, body, 0)

    @pl.when(i + 1 < n)
    def _():
        for r0 in range(0, ROW_TILE, INDEX_BATCH):
            idx = [[dest_next_ref[0, k * ROW_TILE + r] for r in range(r0, r0 + INDEX_BATCH)] for k in range(TOP_K)]
            for r in range(r0, r0 + INDEX_BATCH):
                for k in range(TOP_K):
                    row_copy(idx[k][r - r0], r, k, 1 - slot).start(priority=k % DMA_THREADS)

    _tile_wait(yb_hbm, ybuf.at[slot], sem.at[slot])
    route = route_ref[...]
    buf = ybuf.at[slot]
    y = (_load_rows(buf, ROW_TILE) * route[:, 2:3]
         + _load_rows(buf, ROW_TILE, offset=ROW_TILE * ROW_SUBLANES) * route[:, 3:4])
    x2 = x1_ref[...] + y
    ms = jnp.mean(x2 * x2, axis=-1, keepdims=True)
    o_ref[...] = x2 * lax.rsqrt(ms + EPS) * g_ref[...]


def _combine_and_norm(dest8, x1, route, norm_w, yb):
    t = x1.shape[0]
    n_tiles = t // ROW_TILE
    dest_spec = lambda f: pl.BlockSpec((None, 1, TILE_ROW_COPIES), lambda i: (f(i), 0, 0), memory_space=pltpu.SMEM)
    return pl.pallas_call(
        functools.partial(_combine_kernel, n=n_tiles),
        grid=(n_tiles,),
        in_specs=[dest_spec(lambda i: i), dest_spec(lambda i: jnp.minimum(i + 1, n_tiles - 1)),
                  pl.BlockSpec((ROW_TILE, D_MODEL), lambda i: (i, 0)),
                  pl.BlockSpec((ROW_TILE, LANES), lambda i: (i, 0)),
                  _const_spec((1, D_MODEL)),
                  pl.BlockSpec(memory_space=pl.ANY)],
        out_specs=pl.BlockSpec((ROW_TILE, D_MODEL), lambda i: (i, 0)),
        out_shape=jax.ShapeDtypeStruct((t, D_MODEL), F32),
        scratch_shapes=[pltpu.VMEM((2, TILE_ROW_COPIES * ROW_SUBLANES, LANES), F32),
                        pltpu.SemaphoreType.DMA((2,))],
        name="combine_and_norm",
        compiler_params=_params("arbitrary", disable_bounds_checks=True),
    )(dest8, dest8, x1, route, norm_w, yb)


def _dispatch_plan(route_t, counts):
    t = route_t.shape[1]
    counts = counts[0, 0:N_EXPERTS].astype(jnp.int32)
    padded = (counts + MOE_BLOCK - 1) // MOE_BLOCK * MOE_BLOCK
    pad_end = jnp.cumsum(padded)
    pad_start = pad_end - padded
    expert = route_t[0:TOP_K].astype(jnp.int32)
    rank = route_t[4:4 + TOP_K].astype(jnp.int32)
    start_of = sum(jnp.where(expert == e, pad_start[e], 0) for e in range(N_EXPERTS))
    dest8 = (start_of + rank) * ROW_SUBLANES
    dest8 = dest8.reshape(TOP_K, t // ROW_TILE, ROW_TILE).transpose(1, 0, 2).reshape(t // ROW_TILE, 1, TILE_ROW_COPIES)
    n_blocks = -(-(t * TOP_K + N_EXPERTS * (MOE_BLOCK - 1)) // MOE_BLOCK)
    block_start = jnp.arange(n_blocks, dtype=jnp.int32) * MOE_BLOCK
    block_expert = jnp.minimum(jnp.sum((pad_end[None, :] <= block_start[:, None]).astype(jnp.int32), axis=1),
                               N_EXPERTS - 1)
    block_valid = (block_start < pad_end[-1]).astype(jnp.int32)
    n_slots = n_blocks * MOE_BLOCK
    fill_start = jnp.concatenate([pad_start + counts, pad_end[-1:]])
    fill_n = jnp.concatenate([padded - counts, n_slots - pad_end[-1:]])
    return dest8, fill_start, fill_n, block_expert, block_valid, n_slots


def _rel_bias_band(rel_table, qc):
    rows = qc * CHUNK
    window = ATTN_REACH + rows
    n = window + rows - 1
    diag = rel_table[:, np.clip(ATTN_REACH + rows - 1 - np.arange(n), -MAX_REL, MAX_REL) + MAX_REL].astype(F32)
    shifted = jnp.tile(diag, (1, rows + 1))[:, :rows * (n + 1)].reshape(N_HEADS_A, rows, n + 1)[:, :, :window]
    bias = shifted[:, ::-1, :] * LOG2E
    i = np.arange(rows)[:, None]
    j = np.arange(window)[None, :]
    first = (i // CHUNK) * CHUNK
    return jnp.where((j >= first) & (j < first + BAND), bias, NEG_INF)


def _pad_lanes(v, width=LANES):
    return jnp.pad(v.astype(F32), (0, width - v.shape[0])).reshape(1, width)


def _mixers(x, kv_cache, conv_state, ssm_state, w):
    batch, l, _ = x.shape
    nc = l // CHUNK
    t = batch * l
    x2d = x.reshape(t, D_MODEL)
    proj, dt_raw = _in_projection(x2d, w["norm_mix"], w["w_proj"])
    proj3 = proj.reshape(batch, l, PROJ_WIDTH)
    k_new = lambda rows: proj3[:, rows, K0:K0 + ATTN_WIDTH].astype(F32)
    v_new = lambda rows: proj3[:, rows, V0:V0 + ATTN_WIDTH].astype(F32)
    new_part = lambda col: (proj, pl.BlockSpec((l, ATTN_WIDTH), lambda b, c: (b, col // ATTN_WIDTH)), l)
    if kv_cache is None:
        qc = ATTN_CHUNKS_PER_STEP if nc % ATTN_CHUNKS_PER_STEP == 0 else 1
        attn, = _band_attention(proj, Q0 // ATTN_WIDTH, [new_part(K0)], [new_part(V0)],
                                _rel_bias_band(w["rel_table"], qc), batch=batch, nc=nc, pad_rows=ATTN_REACH, qc=qc)
        keep = slice(l - min(ATTN_REACH, l), l)
        k_keep, v_keep = k_new(keep), v_new(keep)
    else:
        assert nc == 1
        wlen = kv_cache[0].shape[1]
        cache_k, cache_v = (u.reshape(batch, wlen, ATTN_WIDTH) for u in kv_cache)
        cache_part = lambda u: (u, pl.BlockSpec((None, wlen, ATTN_WIDTH), lambda b, c: (b, 0, 0)), wlen)
        attn, k_keep, v_keep = _band_attention(
            proj, Q0 // ATTN_WIDTH, [cache_part(cache_k), new_part(K0)], [cache_part(cache_v), new_part(V0)],
            _rel_bias_band(w["rel_table"], 1), batch=batch, nc=nc, pad_rows=BAND - (wlen + l), qc=1, roll_out=True)
    tail = proj3[:, l - (CONV_WIDTH - 1):, XBC0:XBC0 + CONV_DIM].astype(F32)
    if conv_state is None:
        conv0 = st0 = None
        conv_new = tail
    else:
        conv0 = jnp.pad(conv_state, ((0, 0), (CARRY_ROWS - (CONV_WIDTH - 1), 0), (0, 0)))
        st0 = jnp.transpose(ssm_state.reshape(batch, D_INNER, D_STATE), (0, 2, 1))
        conv_new = jnp.concatenate([conv_state, tail], axis=1)[:, -(CONV_WIDTH - 1):]
    ssm, st_t = _ssd_mixer(proj, dt_raw, conv0, st0, w["conv_w"], w["conv_b"], w["dt_bias"], w["a_log"], w["d_exp"],
                           w["ssm_norm"], batch=batch, nc=nc)
    ssm_new = jnp.transpose(st_t, (0, 2, 1)).reshape(batch, SSM_HEADS, SSM_HEAD_DIM, D_STATE)
    heads = lambda u: u.reshape(batch, u.shape[1], N_HEADS_A, HEAD_DIM_A)
    return (x2d, attn, ssm, proj), (heads(k_keep), heads(v_keep), conv_new, ssm_new)


def kernel(x_prompt, x_sample, cache_attn_k, cache_attn_v, state_conv, state_ssm, norm_mix, w_in, rel_bias, conv_w, conv_b, dt_bias, a_log, d_skip, ssm_norm, w_branch_a, w_branch_b, w_out, norm_ffn, w_router_group, b_router_group, w_router_expert, b_router_expert, w_gate, w_up, w_down, norm_final):
    assert norm_mix.shape[0] == 1, "single-layer trunk"
    q, k, v, z, xbc, dtc, ga, gb = jnp.split(w_in[0], np.cumsum(
        (ATTN_WIDTH, ATTN_WIDTH, ATTN_WIDTH, D_INNER, CONV_DIM, SSM_HEADS, D_MODEL, D_MODEL))[:-1].tolist(), axis=1)
    w_proj = jnp.concatenate([z, ga, gb, xbc, q, k, v, jnp.pad(dtc, ((0, 0), (0, LANES - SSM_HEADS)))],
                             axis=1).astype(BF16)
    w_router = jnp.pad(jnp.concatenate([w_router_group[0], w_router_expert[0]], axis=1),
                       ((0, 0), (0, LANES - N_EXPERT_GROUPS - N_EXPERTS)))
    w_router_hi = w_router.astype(BF16)
    w_router_lo = (w_router - w_router_hi.astype(F32)).astype(BF16)
    b_router = _pad_lanes(jnp.concatenate([b_router_group[0], b_router_expert[0]]))
    w = dict(
        norm_mix=norm_mix[0].reshape(1, D_MODEL), w_proj=w_proj, rel_table=rel_bias[0],
        conv_w=conv_w[0], conv_b=conv_b[0].reshape(1, CONV_DIM), dt_bias=_pad_lanes(dt_bias[0]),
        a_log=_pad_lanes(a_log[0]), d_exp=jnp.repeat(d_skip[0], SSM_HEAD_DIM).reshape(1, D_INNER),
        ssm_norm=ssm_norm[0].reshape(1, D_INNER),
    )
    mix_w = (w_branch_a[0].astype(BF16), w_branch_b[0].astype(BF16), w_out[0].astype(BF16),
             norm_ffn[0].reshape(1, D_MODEL), jnp.concatenate([w_router_hi, w_router_lo], axis=1), b_router)
    norm_out = norm_final.reshape(1, D_MODEL)

    groups = [_mixers(x_prompt, None, None, None, w),
              _mixers(x_sample, (cache_attn_k[0], cache_attn_v[0]), state_conv[0], state_ssm[0], w)]
    counts = jnp.zeros((1, LANES), F32)
    mixed = []
    routes_t = []
    for acts, _ in groups:
        x1, hn, route, route_t, counts = _mix_and_route(*acts, *mix_w, counts)
        mixed.append((x1, hn, route))
        routes_t.append(route_t)
    dest8, fill_start, fill_n, block_expert, block_valid, n_slots = _dispatch_plan(
        jnp.concatenate(routes_t, axis=1), counts)
    xs = _dispatch(dest8, fill_start, fill_n, [m[1] for m in mixed], n_slots)
    yb = _routed_experts(xs, block_expert, block_valid, w_gate[0], w_up[0], w_down[0])
    outs, first = [], 0
    for (x1, _, route), (acts, _) in zip(mixed, groups):
        n_tiles = x1.shape[0] // ROW_TILE
        y = _combine_and_norm(dest8[first:first + n_tiles], x1, route, norm_out, yb)
        first += n_tiles
        outs.append(y)
    (yp, ys), ((kp, vp, cp, sp), (ks, vs, cs, ss)) = outs, [g[1] for g in groups]
    yp = yp.reshape(x_prompt.shape)
    ys = ys.reshape(x_sample.shape)
    return (yp, ys, kp[None], vp[None], cp[None], sp[None], ks[None], vs[None], cs[None], ss[None])
```
